```python
import math
import jax, jax.numpy as jnp
from jax import lax
import numpy as np

D_MODEL = 2048
BATCH = 2
SEQ = 16384
DEPTH = 1

HEAD_DIM = 128
A_HEADS = 8
A_WIDTH = A_HEADS * HEAD_DIM
IDX_HEADS = 16
IDX_DIM = 64
TOPK_MAX = 256
B_HEADS = 4
B_WIDTH = B_HEADS * 2 * HEAD_DIM
Q_BLOCK = 128
ROPE_THETA = 10000.0
NORM_EPS = 1e-6
SUBLN_EPS = 1e-5

SPLIT_SIZES = (
    A_WIDTH, A_WIDTH, A_WIDTH, A_WIDTH,
    IDX_HEADS * IDX_DIM, IDX_DIM, IDX_HEADS,
    B_WIDTH, B_WIDTH, B_WIDTH, B_WIDTH,
    D_MODEL, D_MODEL,
)
IN_COLS = sum(SPLIT_SIZES)

kernel_name = 'hybrid_dsa_diffattn_gated_merge'


def rms_norm(x, gain, eps=NORM_EPS):
    xf = x.astype(jnp.float32)
    y = xf * lax.rsqrt(jnp.mean(xf * xf, axis=-1, keepdims=True) + eps)
    return (y * gain.astype(jnp.float32)).astype(x.dtype)


def rope_tables(positions, dim):
    inv_freq = ROPE_THETA ** (-jnp.arange(0, dim, 2, dtype=jnp.float32) / dim)
    ang = positions.astype(jnp.float32)[..., None] * inv_freq
    return jnp.cos(ang), jnp.sin(ang)


def apply_rope(x, cos, sin):
    shape = cos.shape[:2] + (1,) * (x.ndim - 3) + cos.shape[-1:]
    c = cos.reshape(shape).astype(x.dtype)
    s = sin.reshape(shape).astype(x.dtype)
    x1, x2 = jnp.split(x, 2, axis=-1)
    return jnp.concatenate([x1 * c - x2 * s, x2 * c + x1 * s], axis=-1)


def setup_inputs(seed: int = 0) -> dict:
    key = jax.random.key(seed)
    ks = jax.random.split(key, 16)
    f32 = jnp.float32
    x = jax.random.normal(ks[0], (BATCH, SEQ, D_MODEL), f32)
    positions = jnp.broadcast_to(jnp.arange(SEQ, dtype=jnp.int32), (BATCH, SEQ))
    norm_gain = 1.0 + 0.02 * jax.random.normal(ks[1], (DEPTH, D_MODEL), f32)
    w_in = jax.random.normal(ks[2], (DEPTH, D_MODEL, IN_COLS), f32) * D_MODEL ** -0.5
    a_q_gain = 1.0 + 0.02 * jax.random.normal(ks[3], (DEPTH, HEAD_DIM), f32)
    a_k_gain = 1.0 + 0.02 * jax.random.normal(ks[4], (DEPTH, HEAD_DIM), f32)
    b_q_gain = 1.0 + 0.02 * jax.random.normal(ks[5], (DEPTH, HEAD_DIM), f32)
    b_k_gain = 1.0 + 0.02 * jax.random.normal(ks[6], (DEPTH, HEAD_DIM), f32)
    lambda_q1 = 0.1 * jax.random.normal(ks[7], (DEPTH, HEAD_DIM), f32)
    lambda_k1 = 0.1 * jax.random.normal(ks[8], (DEPTH, HEAD_DIM), f32)
    lambda_q2 = 0.1 * jax.random.normal(ks[9], (DEPTH, HEAD_DIM), f32)
    lambda_k2 = 0.1 * jax.random.normal(ks[10], (DEPTH, HEAD_DIM), f32)
    b_subln_gain = 1.0 + 0.02 * jax.random.normal(ks[11], (DEPTH, 2 * HEAD_DIM), f32)
    w_o_a = jax.random.normal(ks[12], (DEPTH, A_WIDTH, D_MODEL), f32) * A_WIDTH ** -0.5
    w_o_b = jax.random.normal(ks[13], (DEPTH, B_WIDTH, D_MODEL), f32) * B_WIDTH ** -0.5
    w_out = jax.random.normal(ks[14], (DEPTH, D_MODEL, D_MODEL), f32) * D_MODEL ** -0.5
    return {'x': x, 'positions': positions, 'norm_gain': norm_gain, 'w_in': w_in,
            'a_q_gain': a_q_gain, 'a_k_gain': a_k_gain, 'b_q_gain': b_q_gain, 'b_k_gain': b_k_gain,
            'lambda_q1': lambda_q1, 'lambda_k1': lambda_k1, 'lambda_q2': lambda_q2, 'lambda_k2': lambda_k2,
            'b_subln_gain': b_subln_gain, 'w_o_a': w_o_a, 'w_o_b': w_o_b, 'w_out': w_out}


def reference(x, positions, norm_gain, w_in, a_q_gain, a_k_gain, b_q_gain, b_k_gain,
              lambda_q1, lambda_k1, lambda_q2, lambda_k2, b_subln_gain, w_o_a, w_o_b, w_out):
    B, S, _ = x.shape
    nb = S // Q_BLOCK
    topk = min(TOPK_MAX, S // 4)
    offsets = np.cumsum(SPLIT_SIZES)[:-1].tolist()
    cos_h, sin_h = rope_tables(positions, HEAD_DIM)
    cos_i, sin_i = rope_tables(positions, IDX_DIM)
    key_idx = jnp.arange(S)
    q_idx_blocks = jnp.arange(S).reshape(nb, Q_BLOCK)

    def to_blocks(t):
        return jnp.swapaxes(t.reshape((B, nb, Q_BLOCK) + t.shape[2:]), 0, 1)

    def from_blocks(t):
        return jnp.swapaxes(t, 0, 1).reshape((B, S) + t.shape[3:])

    for l in range(DEPTH):
        h = rms_norm(x, norm_gain[l])
        proj = jnp.einsum('bsd,dn->bsn', h, w_in[l])
        (qa, ka, va, ga, qi, ki, wi, qb, kb, vb, gb, ma, mb) = jnp.split(proj, offsets, axis=-1)

        qa = apply_rope(rms_norm(qa.reshape(B, S, A_HEADS, HEAD_DIM), a_q_gain[l]), cos_h, sin_h)
        ka = apply_rope(rms_norm(ka.reshape(B, S, A_HEADS, HEAD_DIM), a_k_gain[l]), cos_h, sin_h)
        va = va.reshape(B, S, A_HEADS, HEAD_DIM)
        qi = apply_rope(qi.reshape(B, S, IDX_HEADS, IDX_DIM), cos_i, sin_i)
        ki = apply_rope(ki, cos_i, sin_i)
        wi = wi * (IDX_HEADS ** -0.5 * IDX_DIM ** -0.5)

        def a_block(args):
            qa_b, qi_b, wi_b, t_b = args
            rel = jax.nn.relu(jnp.einsum('bqhd,bsd->bqhs', qi_b, ki).astype(jnp.float32))
            score = jnp.einsum('bqhs,bqh->bqs', rel, wi_b.astype(jnp.float32))
            causal = key_idx[None, :] <= t_b[:, None]
            score = jnp.where(causal[None], score, -jnp.inf)
            _, sel = lax.top_k(score, topk)
            k_sel = jax.vmap(lambda kk, ii: kk[ii])(ka, sel)
            v_sel = jax.vmap(lambda vv, ii: vv[ii])(va, sel)
            s = jnp.einsum('bqhd,bqkhd->bqhk', qa_b, k_sel).astype(jnp.float32) * HEAD_DIM ** -0.5
            valid = (sel <= t_b[None, :, None])[:, :, None, :]
            p = jax.nn.softmax(jnp.where(valid, s, -jnp.inf), axis=-1)
            return jnp.einsum('bqhk,bqkhd->bqhd', p.astype(v_sel.dtype), v_sel)

        oa = lax.map(a_block, (to_blocks(qa), to_blocks(qi), to_blocks(wi), q_idx_blocks))
        oa = from_blocks(oa).reshape(B, S, A_WIDTH)

        lambda_init = 0.8 - 0.6 * math.exp(-0.3 * l)
        lam = (jnp.exp(jnp.sum(lambda_q1[l].astype(jnp.float32) * lambda_k1[l].astype(jnp.float32)))
               - jnp.exp(jnp.sum(lambda_q2[l].astype(jnp.float32) * lambda_k2[l].astype(jnp.float32)))
               + lambda_init)
        qb = apply_rope(rms_norm(qb.reshape(B, S, B_HEADS, 2, HEAD_DIM), b_q_gain[l]), cos_h, sin_h)
        kb = apply_rope(rms_norm(kb.reshape(B, S, B_HEADS, 2, HEAD_DIM), b_k_gain[l]), cos_h, sin_h)
        vb = vb.reshape(B, S, B_HEADS, 2 * HEAD_DIM)

        def b_block(args):
            q_b, t_b = args
            s = jnp.einsum('bqhcd,bshcd->bhcqs', q_b, kb).astype(jnp.float32) * HEAD_DIM ** -0.5
            causal = key_idx[None, :] <= t_b[:, None]
            p = jax.nn.softmax(jnp.where(causal[None, None, None], s, -jnp.inf), axis=-1)
            attn = p[:, :, 0] - lam * p[:, :, 1]
            return jnp.einsum('bhqs,bshe->bqhe', attn.astype(vb.dtype), vb)

        ob = from_blocks(lax.map(b_block, (to_blocks(qb), q_idx_blocks)))
        ob = (rms_norm(ob, b_subln_gain[l], SUBLN_EPS) * (1.0 - lambda_init)).reshape(B, S, B_WIDTH)

        ya = jnp.einsum('bsc,cd->bsd', oa * jax.nn.silu(ga), w_o_a[l])
        yb = jnp.einsum('bsc,cd->bsd', ob * jax.nn.silu(gb), w_o_b[l])
        merged = jax.nn.sigmoid(ma) * ya + jax.nn.sigmoid(mb) * yb
        x = x + jnp.einsum('bsd,de->bse', merged, w_out[l])
    return x
```

```python
import functools
import math

import jax
import jax.numpy as jnp
import numpy as np
from jax import lax
from jax.experimental import pallas as pl
from jax.experimental.pallas import tpu as pltpu

F32 = jnp.float32
BF16 = jnp.bfloat16

HEAD_DIM = 128
A_HEADS = 8
A_WIDTH = A_HEADS * HEAD_DIM
IDX_HEADS = 16
IDX_DIM = 64
TOPK_MAX = 256
B_HEADS = 4
B_WIDTH = B_HEADS * 2 * HEAD_DIM
ROPE_THETA = 10000.0
NORM_EPS = 1e-6
SUBLN_EPS = 1e-5

LANES = 128
INT_MIN = -(2 ** 31)
NEG_BIG = -1e30
VMEM_LIMIT_BYTES = 56 * 1024 * 1024


def _nt_dot(a, b):
    return lax.dot_general(a, b, (((1,), (1,)), ((), ())), preferred_element_type=F32)


def _tile_lanes(v, width):
    reps = width // LANES
    return v if reps == 1 else jnp.concatenate([v] * reps, axis=1)


def _params(semantics):
    return pltpu.CompilerParams(dimension_semantics=semantics, vmem_limit_bytes=VMEM_LIMIT_BYTES)


def _norm_kernel(x_ref, g_ref, h_ref):
    x = x_ref[...]
    ms = jnp.mean(x * x, axis=-1, keepdims=True)
    h_ref[...] = (x * lax.rsqrt(ms + NORM_EPS) * g_ref[...]).astype(BF16)


def _norm(x2d, gain, tm):
    m, d = x2d.shape
    return pl.pallas_call(
        _norm_kernel,
        grid=(m // tm,),
        in_specs=[pl.BlockSpec((tm, d), lambda i: (i, 0)), pl.BlockSpec((1, d), lambda i: (0, 0))],
        out_specs=pl.BlockSpec((tm, d), lambda i: (i, 0)),
        out_shape=jax.ShapeDtypeStruct((m, d), BF16),
        compiler_params=_params(("parallel",)),
        name="norm",
    )(x2d, gain)


def _proj_body(h_ref, w_ref, *rest, n_aux, epilogue):
    acc = jnp.dot(h_ref[...], w_ref[...], preferred_element_type=F32)
    epilogue(acc, rest[:n_aux], rest[n_aux:])


def _proj(h, w, aux, aux_specs, epilogue, out_shape, out_specs, tm, tn, name):
    m, k = h.shape
    n = w.shape[1]
    return pl.pallas_call(
        functools.partial(_proj_body, n_aux=len(aux), epilogue=epilogue),
        grid=(m // tm, n // tn),
        in_specs=[pl.BlockSpec((tm, k), lambda i, j: (i, 0)),
                  pl.BlockSpec((k, tn), lambda i, j: (0, j))] + aux_specs,
        out_specs=out_specs,
        out_shape=out_shape,
        compiler_params=_params(("parallel", "arbitrary")),
        name=name,
    )(h, w, *aux)


def _rope128(y, cos_full, sin_signed):
    return y * cos_full + pltpu.roll(y, 64, 1) * sin_signed


def _ep_qk(acc, aux, outs):
    g_ref, c_ref, s_ref = aux
    (o_ref,) = outs
    c = c_ref[...]
    s = s_ref[...]
    for t in range(acc.shape[1] // LANES):
        r = acc[:, t * LANES:(t + 1) * LANES]
        ms = jnp.mean(r * r, axis=-1, keepdims=True)
        y = r * lax.rsqrt(ms + NORM_EPS) * g_ref[:, t * LANES:(t + 1) * LANES]
        o_ref[:, t * LANES:(t + 1) * LANES] = _rope128(y, c, s).astype(BF16)


def _ep_qi(acc, aux, outs):
    c_ref, s_ref = aux
    (o_ref,) = outs
    c = c_ref[...]
    s = s_ref[...]
    for t in range(acc.shape[1] // LANES):
        o_ref[t] = _rope128(acc[:, t * LANES:(t + 1) * LANES], c, s).astype(BF16)


def _ep_kw(acc, aux, outs):
    c_ref, s_ref = aux
    kab_ref, w_ref = outs
    j = pl.program_id(1)

    @pl.when(j == 0)
    def _():
        c = c_ref[...]
        s = s_ref[...]
        for t in range(2):
            kab_ref[t] = _rope128(acc[:, t * LANES:(t + 1) * LANES], c, s).astype(BF16)

    @pl.when(j == 1)
    def _():
        w_ref[...] = acc[:, :LANES] * (IDX_HEADS ** -0.5 * IDX_DIM ** -0.5)


def _ep_vg(acc, aux, outs, *, n_plain, n_silu):
    (o_ref,) = outs
    j = pl.program_id(1)

    @pl.when(j < n_plain)
    def _():
        o_ref[...] = acc.astype(BF16)

    @pl.when(jnp.logical_and(j >= n_plain, j < n_plain + n_silu))
    def _():
        o_ref[...] = (acc * jax.nn.sigmoid(acc)).astype(BF16)

    @pl.when(j >= n_plain + n_silu)
    def _():
        o_ref[...] = jax.nn.sigmoid(acc).astype(BF16)


def _a_kernel(qi_ref, wi_ref, kab_ref, qa_ref, ka_ref, va_ref, ga_ref, o_ref,
              sc_ref, tau_ref, m_ref, l_ref, acc_ref, *, tq, tk, topk):
    iq = pl.program_id(1)
    kj = pl.program_id(2)
    last = ((iq + 1) * tq - 1) // tk

    @pl.when(kj == 0)
    def _index_and_select():
        w = wi_ref[...]
        row = iq * tq + lax.broadcasted_iota(jnp.int32, (tq, tk), 0)
        col0 = lax.broadcasted_iota(jnp.int32, (tq, tk), 1)

        def score_tile(j, carry):
            start = pl.multiple_of(j * tk, tk)
            k_even = kab_ref[0, pl.ds(start, tk), :]
            k_odd = kab_ref[1, pl.ds(start, tk), :]
            acc = jnp.zeros((tq, tk), F32)
            for p in range(IDX_HEADS // 2):
                q_p = qi_ref[p]
                s_even = jnp.maximum(_nt_dot(q_p, k_even), 0.0)
                s_odd = jnp.maximum(_nt_dot(q_p, k_odd), 0.0)
                acc = acc + w[:, 2 * p:2 * p + 1] * s_even + w[:, 2 * p + 1:2 * p + 2] * s_odd
            bits = pltpu.bitcast(acc, jnp.int32)
            key = bits ^ ((bits >> 31) & 0x7FFFFFFF)
            sc_ref[j] = jnp.where(col0 + j * tk <= row, key, INT_MIN)
            return carry

        lax.fori_loop(0, last + 1, score_tile, 0)

        def bit_step(i, thr):
            cand = thr + jnp.left_shift(jnp.int32(1), 31 - i)

            def count_tile(j, c):
                keys = sc_ref[j]
                for t in range(tk // LANES):
                    c = c + jnp.where(keys[:, t * LANES:(t + 1) * LANES] >= cand, 1.0, 0.0)
                return c

            c = lax.fori_loop(0, last + 1, count_tile, jnp.zeros((tq, LANES), F32))
            cnt = jnp.sum(c, axis=1, keepdims=True)
            return jnp.where(cnt >= float(topk), cand, thr)

        thr = lax.fori_loop(0, 32, bit_step, jnp.full((tq, LANES), INT_MIN, jnp.int32))
        tau_ref[...] = jnp.maximum(thr, INT_MIN + 1)
        m_ref[...] = jnp.full(m_ref.shape, NEG_BIG, F32)
        l_ref[...] = jnp.zeros(l_ref.shape, F32)
        acc_ref[...] = jnp.zeros(acc_ref.shape, F32)

    @pl.when(kj <= last)
    def _attend():
        sel = sc_ref[kj] >= _tile_lanes(tau_ref[...], tk)
        for h in range(A_HEADS):
            hs = slice(h * HEAD_DIM, (h + 1) * HEAD_DIM)
            s = jnp.where(sel, _nt_dot(qa_ref[:, hs], ka_ref[:, hs]), NEG_BIG)
            m_prev = m_ref[h]
            m_new = jnp.maximum(m_prev, jnp.max(s, axis=1, keepdims=True))
            alpha = jnp.exp(m_prev - m_new)
            p = jnp.where(sel, jnp.exp(s - _tile_lanes(m_new, tk)), 0.0)
            l_ref[h] = alpha * l_ref[h] + jnp.sum(p, axis=1, keepdims=True)
            acc_ref[h] = alpha * acc_ref[h] + jnp.dot(p.astype(BF16), va_ref[:, hs],
                                                      preferred_element_type=F32)
            m_ref[h] = m_new

    @pl.when(kj == last)
    def _finish():
        for h in range(A_HEADS):
            hs = slice(h * HEAD_DIM, (h + 1) * HEAD_DIM)
            o = acc_ref[h] / l_ref[h]
            o_ref[:, hs] = (o * ga_ref[:, hs].astype(F32)).astype(BF16)


def _attn_a(qi, wi, kab, qkr, vg, batch, seq, tq, tk, topk):
    m = batch * seq
    nq = seq // tq
    nk = seq // tk

    def qrow(b, i, j):
        return b * nq + i

    def krow(b, i, j):
        return b * nk + jnp.minimum(j, ((i + 1) * tq - 1) // tk)

    return pl.pallas_call(
        functools.partial(_a_kernel, tq=tq, tk=tk, topk=topk),
        grid=(batch, nq, nk),
        in_specs=[
            pl.BlockSpec((IDX_HEADS // 2, tq, LANES), lambda b, i, j: (0, qrow(b, i, j), 0)),
            pl.BlockSpec((tq, LANES), lambda b, i, j: (qrow(b, i, j), 0)),
            pl.BlockSpec((2, seq, LANES), lambda b, i, j: (0, b, 0)),
            pl.BlockSpec((tq, A_WIDTH), lambda b, i, j: (qrow(b, i, j), 0)),
            pl.BlockSpec((tk, A_WIDTH), lambda b, i, j: (krow(b, i, j), 1)),
            pl.BlockSpec((tk, A_WIDTH), lambda b, i, j: (krow(b, i, j), 0)),
            pl.BlockSpec((tq, A_WIDTH), lambda b, i, j: (qrow(b, i, j), 2)),
        ],
        out_specs=pl.BlockSpec((tq, A_WIDTH), lambda b, i, j: (qrow(b, i, j), 0)),
        out_shape=jax.ShapeDtypeStruct((m, A_WIDTH), BF16),
        scratch_shapes=[
            pltpu.VMEM((nk, tq, tk), jnp.int32),
            pltpu.VMEM((tq, LANES), jnp.int32),
            pltpu.VMEM((A_HEADS, tq, LANES), F32),
            pltpu.VMEM((A_HEADS, tq, LANES), F32),
            pltpu.VMEM((A_HEADS, tq, HEAD_DIM), F32),
        ],
        compiler_params=_params(("parallel", "arbitrary", "arbitrary")),
        name="attn_a",
    )(qi, wi, kab, qkr, qkr, vg, vg)


def _b_kernel(q_ref, k_ref, v_ref, sg_ref, gain_ref, lq1_ref, lk1_ref, lq2_ref, lk2_ref, o_ref,
              m_ref, l_ref, acc_ref, *, tq, tk, lambda_init):
    iq = pl.program_id(2)
    kj = pl.program_id(3)
    last = ((iq + 1) * tq - 1) // tk

    @pl.when(kj == 0)
    def _init():
        m_ref[...] = jnp.full(m_ref.shape, NEG_BIG, F32)
        l_ref[...] = jnp.zeros(l_ref.shape, F32)
        acc_ref[...] = jnp.zeros(acc_ref.shape, F32)

    @pl.when(kj <= last)
    def _attend():
        row = iq * tq + lax.broadcasted_iota(jnp.int32, (tq, tk), 0)
        col = kj * tk + lax.broadcasted_iota(jnp.int32, (tq, tk), 1)
        causal = col <= row
        v = v_ref[...]
        for c in range(2):
            cs = slice(c * HEAD_DIM, (c + 1) * HEAD_DIM)
            s = jnp.where(causal, _nt_dot(q_ref[:, cs], k_ref[:, cs]), NEG_BIG)
            m_prev = m_ref[c]
            m_new = jnp.maximum(m_prev, jnp.max(s, axis=1, keepdims=True))
            alpha = jnp.exp(m_prev - m_new)
            p = jnp.exp(s - _tile_lanes(m_new, tk))
            l_ref[c] = alpha * l_ref[c] + jnp.sum(p, axis=1, keepdims=True)
            acc_ref[c] = _tile_lanes(alpha, 2 * HEAD_DIM) * acc_ref[c] + jnp.dot(
                p.astype(BF16), v, preferred_element_type=F32)
            m_ref[c] = m_new

    @pl.when(kj == last)
    def _finish():
        lam = (jnp.exp(jnp.sum(lq1_ref[...] * lk1_ref[...], axis=1, keepdims=True))
               - jnp.exp(jnp.sum(lq2_ref[...] * lk2_ref[...], axis=1, keepdims=True))
               + lambda_init)
        o0 = acc_ref[0] / _tile_lanes(l_ref[0], 2 * HEAD_DIM)
        o1 = acc_ref[1] / _tile_lanes(l_ref[1], 2 * HEAD_DIM)
        o = o0 - lam * o1
        ms = jnp.mean(o * o, axis=-1, keepdims=True)
        y = o * lax.rsqrt(ms + SUBLN_EPS) * gain_ref[...] * (1.0 - lambda_init)
        o_ref[...] = (y * sg_ref[...].astype(F32)).astype(BF16)


def _attn_b(qkr, vg, gain, lq1, lk1, lq2, lk2, batch, seq, tq, tk, lambda_init):
    m = batch * seq
    nq = seq // tq
    nk = seq // tk
    hw = 2 * HEAD_DIM

    def qrow(b, h, i, j):
        return b * nq + i

    def krow(b, h, i, j):
        return b * nk + jnp.minimum(j, ((i + 1) * tq - 1) // tk)

    vec = pl.BlockSpec((1, HEAD_DIM), lambda b, h, i, j: (0, 0))
    return pl.pallas_call(
        functools.partial(_b_kernel, tq=tq, tk=tk, lambda_init=lambda_init),
        grid=(batch, B_HEADS, nq, nk),
        in_specs=[
            pl.BlockSpec((tq, hw), lambda b, h, i, j: (qrow(b, h, i, j), 2 * B_HEADS + h)),
            pl.BlockSpec((tk, hw), lambda b, h, i, j: (krow(b, h, i, j), 3 * B_HEADS + h)),
            pl.BlockSpec((tk, hw), lambda b, h, i, j: (krow(b, h, i, j), B_HEADS + h)),
            pl.BlockSpec((tq, hw), lambda b, h, i, j: (qrow(b, h, i, j), 3 * B_HEADS + h)),
            pl.BlockSpec((1, hw), lambda b, h, i, j: (0, 0)),
            vec, vec, vec, vec,
        ],
        out_specs=pl.BlockSpec((tq, hw), lambda b, h, i, j: (qrow(b, h, i, j), h)),
        out_shape=jax.ShapeDtypeStruct((m, B_WIDTH), BF16),
        scratch_shapes=[
            pltpu.VMEM((2, tq, LANES), F32),
            pltpu.VMEM((2, tq, LANES), F32),
            pltpu.VMEM((2, tq, hw), F32),
        ],
        compiler_params=_params(("parallel", "parallel", "arbitrary", "arbitrary")),
        name="attn_b",
    )(qkr, qkr, vg, vg, gain, lq1, lk1, lq2, lk2)


def _merge_kernel(oa_ref, ob_ref, wa_ref, wb_ref, ma_ref, mb_ref, o_ref):
    ya = jnp.dot(oa_ref[...], wa_ref[...], preferred_element_type=F32)
    yb = jnp.dot(ob_ref[...], wb_ref[...], preferred_element_type=F32)
    o_ref[...] = (ma_ref[...].astype(F32) * ya + mb_ref[...].astype(F32) * yb).astype(BF16)


def _merge(oa, ob, w_a, w_b, vg, d_model, tm, tn):
    m = oa.shape[0]
    gate_a0 = (A_WIDTH + B_WIDTH) * 2 // tn
    gate_b0 = gate_a0 + d_model // tn
    return pl.pallas_call(
        _merge_kernel,
        grid=(m // tm, d_model // tn),
        in_specs=[
            pl.BlockSpec((tm, A_WIDTH), lambda i, j: (i, 0)),
            pl.BlockSpec((tm, B_WIDTH), lambda i, j: (i, 0)),
            pl.BlockSpec((A_WIDTH, tn), lambda i, j: (0, j)),
            pl.BlockSpec((B_WIDTH, tn), lambda i, j: (0, j)),
            pl.BlockSpec((tm, tn), lambda i, j: (i, gate_a0 + j)),
            pl.BlockSpec((tm, tn), lambda i, j: (i, gate_b0 + j)),
        ],
        out_specs=pl.BlockSpec((tm, tn), lambda i, j: (i, j)),
        out_shape=jax.ShapeDtypeStruct((m, d_model), BF16),
        compiler_params=_params(("parallel", "arbitrary")),
        name="merge",
    )(oa, ob, w_a, w_b, vg, vg)


def _out_kernel(mg_ref, w_ref, x_ref, o_ref):
    o_ref[...] = x_ref[...] + jnp.dot(mg_ref[...], w_ref[...], preferred_element_type=F32)


def _out_proj(merged, w_out, x2d, tm, tn):
    m, d = x2d.shape
    return pl.pallas_call(
        _out_kernel,
        grid=(m // tm, d // tn),
        in_specs=[
            pl.BlockSpec((tm, d), lambda i, j: (i, 0)),
            pl.BlockSpec((d, tn), lambda i, j: (0, j)),
            pl.BlockSpec((tm, tn), lambda i, j: (i, j)),
        ],
        out_specs=pl.BlockSpec((tm, tn), lambda i, j: (i, j)),
        out_shape=jax.ShapeDtypeStruct((m, d), F32),
        compiler_params=_params(("parallel", "arbitrary")),
        name="out_proj",
    )(merged, w_out, x2d)


def _qi_column_order():
    lane = np.arange(LANES)
    quarter, r = lane // 32, lane % 32
    dim = r + 32 * (quarter // 2)
    q_cols = np.concatenate([(2 * p + quarter % 2) * IDX_DIM + dim for p in range(IDX_HEADS // 2)])
    return q_cols, dim, (quarter % 2 == 0)


def _layer(x2d, tabs, norm_gain, w_in, a_q_gain, a_k_gain, b_q_gain, b_k_gain,
           lq1, lk1, lq2, lk2, b_subln_gain, w_o_a, w_o_b, w_out, layer, batch, seq):
    d_model = x2d.shape[1]
    cos_h, sin_h, cos_i, sin_i = tabs
    sizes = (A_WIDTH,) * 4 + (IDX_HEADS * IDX_DIM, IDX_DIM, IDX_HEADS) + (B_WIDTH,) * 4 + (d_model,) * 2
    off = np.concatenate([[0], np.cumsum(sizes)])
    seg = lambda i: w_in[:, off[i]:off[i + 1]]
    (w_qa, w_ka, w_va, w_ga, w_qi, w_ki, w_wi, w_qb, w_kb, w_vb, w_gb, w_ma, w_mb) = [seg(i) for i in range(13)]

    scale = HEAD_DIM ** -0.5
    w_qk = jnp.concatenate([w_qa, w_ka, w_qb, w_kb], axis=1).astype(BF16)
    gain_qk = jnp.concatenate([jnp.tile(a_q_gain * scale, A_HEADS), jnp.tile(a_k_gain, A_HEADS),
                               jnp.tile(b_q_gain * scale, 2 * B_HEADS), jnp.tile(b_k_gain, 2 * B_HEADS)])[None, :]
    q_cols, k_dim, even = _qi_column_order()
    w_qi_p = w_qi[:, q_cols].astype(BF16)
    w_k_rep = w_ki[:, k_dim]
    zeros = jnp.zeros_like(w_k_rep)
    w_kw = jnp.concatenate([
        jnp.where(even[None, :], w_k_rep, zeros), jnp.where(even[None, :], zeros, w_k_rep),
        jnp.pad(w_wi, ((0, 0), (0, 2 * LANES - IDX_HEADS)))], axis=1).astype(BF16)
    w_vg = jnp.concatenate([w_va, w_vb, w_ga, w_gb, w_ma, w_mb], axis=1).astype(BF16)

    m = x2d.shape[0]
    tm, tn = 1024, 256
    h = _norm(x2d, norm_gain[None, :], 512)

    tab_spec = pl.BlockSpec((tm, LANES), lambda i, j: (i, 0))
    qkr = _proj(h, w_qk, (gain_qk, cos_h, sin_h),
                [pl.BlockSpec((1, tn), lambda i, j: (0, j)), tab_spec, tab_spec], _ep_qk,
                jax.ShapeDtypeStruct((m, w_qk.shape[1]), BF16),
                pl.BlockSpec((tm, tn), lambda i, j: (i, j)), tm, tn, "proj_qk")
    qi = _proj(h, w_qi_p, (cos_i, sin_i), [tab_spec, tab_spec], _ep_qi,
               jax.ShapeDtypeStruct((IDX_HEADS // 2, m, LANES), BF16),
               pl.BlockSpec((tn // LANES, tm, LANES), lambda i, j: (j, i, 0)), tm, tn, "proj_qi")
    kab, wi = _proj(h, w_kw, (cos_i, sin_i), [tab_spec, tab_spec], _ep_kw,
                    (jax.ShapeDtypeStruct((2, m, LANES), BF16), jax.ShapeDtypeStruct((m, LANES), F32)),
                    (pl.BlockSpec((2, tm, LANES), lambda i, j: (0, i, 0)),
                     pl.BlockSpec((tm, LANES), lambda i, j: (i, 0))), tm, tn, "proj_kw")
    n_plain = (A_WIDTH + B_WIDTH) // tn
    vg = _proj(h, w_vg, (), [], functools.partial(_ep_vg, n_plain=n_plain, n_silu=n_plain),
               jax.ShapeDtypeStruct((m, w_vg.shape[1]), BF16),
               pl.BlockSpec((tm, tn), lambda i, j: (i, j)), tm, tn, "proj_vg")

    topk = min(TOPK_MAX, seq // 4)
    oa = _attn_a(qi, wi, kab, qkr, vg, batch, seq, 256, 512, topk)
    lambda_init = 0.8 - 0.6 * math.exp(-0.3 * layer)
    ob = _attn_b(qkr, vg, b_subln_gain[None, :], lq1[None, :], lk1[None, :], lq2[None, :], lk2[None, :],
                 batch, seq, 512, 512, lambda_init)
    merged = _merge(oa, ob, w_o_a.astype(BF16), w_o_b.astype(BF16), vg, d_model, 1024, 512)
    return _out_proj(merged, w_out.astype(BF16), x2d, 1024, 512)


def kernel(x, positions, norm_gain, w_in, a_q_gain, a_k_gain, b_q_gain, b_k_gain, lambda_q1, lambda_k1,
           lambda_q2, lambda_k2, b_subln_gain, w_o_a, w_o_b, w_out):
    batch, seq, d_model = x.shape
    m = batch * seq
    pos = positions.astype(F32).reshape(m, 1)
    ang_h = pos * ROPE_THETA ** (-jnp.arange(0, HEAD_DIM, 2, dtype=F32) / HEAD_DIM)
    ang_i = pos * ROPE_THETA ** (-jnp.arange(0, IDX_DIM, 2, dtype=F32) / IDX_DIM)
    ch, sh, ci, si = jnp.cos(ang_h), jnp.sin(ang_h), jnp.cos(ang_i), jnp.sin(ang_i)
    tabs = (jnp.concatenate([ch, ch], axis=1), jnp.concatenate([-sh, sh], axis=1),
            jnp.concatenate([ci] * 4, axis=1), jnp.concatenate([-si, -si, si, si], axis=1))
    x2d = x.reshape(m, d_model)
    for layer in range(norm_gain.shape[0]):
        x2d = _layer(x2d, tabs, norm_gain[layer], w_in[layer], a_q_gain[layer], a_k_gain[layer],
                     b_q_gain[layer], b_k_gain[layer], lambda_q1[layer], lambda_k1[layer],
                     lambda_q2[layer], lambda_k2[layer], b_subln_gain[layer], w_o_a[layer], w_o_b[layer],
                     w_out[layer], layer, batch, seq)
    return x2d.reshape(batch, seq, d_model)
```

```python
import functools
import math

import jax
import jax.numpy as jnp
import numpy as np
from jax import lax
from jax.experimental import pallas as pl
from jax.experimental.pallas import tpu as pltpu

F32 = jnp.float32
BF16 = jnp.bfloat16

HEAD_DIM = 128
A_HEADS = 8
A_WIDTH = A_HEADS * HEAD_DIM
IDX_HEADS = 16
IDX_DIM = 64
TOPK_MAX = 256
B_HEADS = 4
B_WIDTH = B_HEADS * 2 * HEAD_DIM
ROPE_THETA = 10000.0
NORM_EPS = 1e-6
SUBLN_EPS = 1e-5

LANES = 128
INT_MIN = -(2 ** 31)
NEG_BIG = -1e30
M_INIT = -5e29
LOG2E = math.log2(math.e)
SELECT_ROWS = 64
VMEM_LIMIT_BYTES = 56 * 1024 * 1024


def _nt_dot(a, b):
    return lax.dot_general(a, b, (((1,), (1,)), ((), ())), preferred_element_type=F32)


def _tile_lanes(v, width):
    reps = width // LANES
    return v if reps == 1 else jnp.concatenate([v] * reps, axis=1)


def _params(semantics):
    return pltpu.CompilerParams(dimension_semantics=semantics, vmem_limit_bytes=VMEM_LIMIT_BYTES)


def _norm_kernel(x_ref, g_ref, h_ref):
    x = x_ref[...]
    ms = jnp.mean(x * x, axis=-1, keepdims=True)
    h_ref[...] = (x * lax.rsqrt(ms + NORM_EPS) * g_ref[...]).astype(BF16)


def _norm(x2d, gain, tm):
    m, d = x2d.shape
    return pl.pallas_call(
        _norm_kernel,
        grid=(m // tm,),
        in_specs=[pl.BlockSpec((tm, d), lambda i: (i, 0)), pl.BlockSpec((1, d), lambda i: (0, 0))],
        out_specs=pl.BlockSpec((tm, d), lambda i: (i, 0)),
        out_shape=jax.ShapeDtypeStruct((m, d), BF16),
        compiler_params=_params(("parallel",)),
        name="norm",
    )(x2d, gain)


def _proj_body(h_ref, w_ref, *rest, n_aux, epilogue):
    acc = jnp.dot(h_ref[...], w_ref[...], preferred_element_type=F32)
    epilogue(acc, rest[:n_aux], rest[n_aux:])


def _proj(h, w, aux, aux_specs, epilogue, out_shape, out_specs, tm, tn, name):
    m, k = h.shape
    n = w.shape[1]
    return pl.pallas_call(
        functools.partial(_proj_body, n_aux=len(aux), epilogue=epilogue),
        grid=(m // tm, n // tn),
        in_specs=[pl.BlockSpec((tm, k), lambda i, j: (i, 0)),
                  pl.BlockSpec((k, tn), lambda i, j: (0, j))] + aux_specs,
        out_specs=out_specs,
        out_shape=out_shape,
        compiler_params=_params(("parallel", "arbitrary")),
        name=name,
    )(h, w, *aux)


def _rope128(y, cos_full, sin_signed):
    return y * cos_full + pltpu.roll(y, 64, 1) * sin_signed


def _ep_qk(acc, aux, outs):
    g_ref, c_ref, s_ref = aux
    (o_ref,) = outs
    c = c_ref[...]
    s = s_ref[...]
    for t in range(acc.shape[1] // LANES):
        r = acc[:, t * LANES:(t + 1) * LANES]
        ms = jnp.mean(r * r, axis=-1, keepdims=True)
        y = r * lax.rsqrt(ms + NORM_EPS) * g_ref[:, t * LANES:(t + 1) * LANES]
        o_ref[:, t * LANES:(t + 1) * LANES] = _rope128(y, c, s).astype(BF16)


def _ep_qi(acc, aux, outs):
    c_ref, s_ref = aux
    (o_ref,) = outs
    c = c_ref[...]
    s = s_ref[...]
    for t in range(acc.shape[1] // LANES):
        o_ref[t] = _rope128(acc[:, t * LANES:(t + 1) * LANES], c, s).astype(BF16)


def _ep_kw(acc, aux, outs):
    c_ref, s_ref = aux
    kab_ref, w_ref = outs
    j = pl.program_id(1)

    @pl.when(j == 0)
    def _():
        c = c_ref[...]
        s = s_ref[...]
        for t in range(2):
            kab_ref[t] = _rope128(acc[:, t * LANES:(t + 1) * LANES], c, s).astype(BF16)

    @pl.when(j == 1)
    def _():
        w_ref[...] = acc[:, :LANES] * (IDX_HEADS ** -0.5 * IDX_DIM ** -0.5)


def _ep_vg(acc, aux, outs, *, n_plain, n_silu):
    (o_ref,) = outs
    j = pl.program_id(1)

    @pl.when(j < n_plain)
    def _():
        o_ref[...] = acc.astype(BF16)

    @pl.when(jnp.logical_and(j >= n_plain, j < n_plain + n_silu))
    def _():
        o_ref[...] = (acc * jax.nn.sigmoid(acc)).astype(BF16)

    @pl.when(j >= n_plain + n_silu)
    def _():
        o_ref[...] = jax.nn.sigmoid(acc).astype(BF16)


def _f32_key(v):
    bits = pltpu.bitcast(v, jnp.int32)
    return bits ^ ((bits >> 31) & 0x7FFFFFFF)


def _smear_right(x):
    for sh in (1, 2, 4, 8, 16):
        x = x | lax.shift_right_logical(x, jnp.int32(sh))
    return x


def _a_kernel(qt_ref, kt_ref, qi_ref, wi_ref, kab_ref, qa_ref, ka_ref, va_ref, ga_ref, o_ref,
              sc_ref, gmax_ref, tau_ref, bias_ref, m_ref, l_ref, acc_ref, *, tq, tk, topk, rows):
    step = pl.program_id(1)
    iq = qt_ref[step]
    kj = kt_ref[step]
    last = ((iq + 1) * tq - 1) // tk

    @pl.when(kj == 0)
    def _index_and_select():
        w = wi_ref[...]
        row = iq * tq + lax.broadcasted_iota(jnp.int32, (tq, tk), 0)
        col0 = lax.broadcasted_iota(jnp.int32, (tq, tk), 1)
        gmax_ref[...] = jnp.full(gmax_ref.shape, -jnp.inf, F32)

        def score_tile(j, carry):
            start = pl.multiple_of(j * tk, tk)
            k_even = kab_ref[0, pl.ds(start, tk), :]
            k_odd = kab_ref[1, pl.ds(start, tk), :]
            acc = jnp.zeros((tq, tk), F32)
            for p in range(IDX_HEADS // 2):
                q_p = qi_ref[p]
                s_even = jnp.maximum(_nt_dot(q_p, k_even), 0.0)
                s_odd = jnp.maximum(_nt_dot(q_p, k_odd), 0.0)
                acc = acc + w[:, 2 * p:2 * p + 1] * s_even + w[:, 2 * p + 1:2 * p + 2] * s_odd
            causal = col0 + j * tk <= row
            sc_ref[j] = jnp.where(causal, _f32_key(acc), INT_MIN)
            accm = jnp.where(causal, acc, -jnp.inf)
            half = tk // 2
            for g in range(2):
                gm = gmax_ref[g]
                for t in range(half // LANES):
                    lo = g * half + t * LANES
                    gm = jnp.maximum(gm, accm[:, lo:lo + LANES])
                gmax_ref[g] = gm
            return carry

        lax.fori_loop(0, last + 1, score_tile, 0)

        for rc in range(tq // rows):
            rs = pl.ds(rc * rows, rows)
            g0 = gmax_ref[0, rs, :]
            g1 = gmax_ref[1, rs, :]
            row_max = jnp.max(jnp.maximum(g0, g1), axis=1, keepdims=True)
            row_low = jnp.min(jnp.minimum(g0, g1), axis=1, keepdims=True)
            hi_u = jnp.broadcast_to(_f32_key(row_max), (rows, LANES)) ^ INT_MIN
            lo_u = jnp.broadcast_to(_f32_key(row_low), (rows, LANES)) ^ INT_MIN
            open_bits = _smear_right(hi_u ^ lo_u)
            thr0 = lo_u & ~open_bits
            done0 = jnp.where(open_bits == 0, 1.0, 0.0)
            undecided = 32 - lax.clz(open_bits)
            first_bit = jnp.max(undecided.astype(F32)).astype(jnp.int32) - 1

            def search_cond(c):
                b, stop, _, _ = c
                return jnp.logical_and(b >= 0, stop == 0)

            def search_step(c):
                b, _, thr_u, done = c
                stop = (jnp.min(done) > 0.0).astype(jnp.int32)
                bit = jnp.left_shift(jnp.int32(1), b)
                active = jnp.logical_and((open_bits & bit) != 0, done == 0.0)
                cand_u = thr_u | bit
                cand = cand_u ^ INT_MIN

                def count_tile(j, cacc):
                    keys = sc_ref[j, rs, :]
                    for t in range(tk // LANES):
                        cacc = cacc + jnp.where(keys[:, t * LANES:(t + 1) * LANES] >= cand, 1.0, 0.0)
                    return cacc

                cacc = lax.fori_loop(0, last + 1, count_tile, jnp.zeros((rows, LANES), F32))
                cnt = jnp.sum(cacc, axis=1, keepdims=True)
                take = jnp.logical_and(active, cnt >= float(topk))
                thr_u = jnp.where(take, cand_u, thr_u)
                finished = jnp.logical_or(jnp.logical_and(active, cnt == float(topk)),
                                          (open_bits & (bit - 1)) == 0)
                done = jnp.where(finished, 1.0, done)
                return b - 1, stop, thr_u, done

            _, _, thr_u, _ = lax.while_loop(search_cond, search_step,
                                            (first_bit, jnp.int32(0), thr0, done0))
            tau_ref[rs, :] = jnp.maximum(thr_u ^ INT_MIN, INT_MIN + 1)

        m_ref[...] = jnp.full(m_ref.shape, M_INIT, F32)
        l_ref[...] = jnp.zeros(l_ref.shape, F32)
        acc_ref[...] = jnp.zeros(acc_ref.shape, F32)

    bias_ref[...] = jnp.where(sc_ref[kj] >= _tile_lanes(tau_ref[...], tk), 0.0, NEG_BIG)
    for h in range(A_HEADS):
        hs = slice(h * HEAD_DIM, (h + 1) * HEAD_DIM)
        s = _nt_dot(qa_ref[:, hs], ka_ref[:, hs]) + bias_ref[...]
        m_prev = m_ref[h]
        m_new = jnp.maximum(m_prev, jnp.max(s, axis=1, keepdims=True))
        alpha = jnp.exp2(m_prev - m_new)
        p = jnp.exp2(s - _tile_lanes(m_new, tk))
        l_ref[h] = alpha * l_ref[h] + jnp.sum(p, axis=1, keepdims=True)
        acc_ref[h] = alpha * acc_ref[h] + jnp.dot(p.astype(BF16), va_ref[:, hs],
                                                  preferred_element_type=F32)
        m_ref[h] = m_new

    @pl.when(kj == last)
    def _finish():
        for h in range(A_HEADS):
            hs = slice(h * HEAD_DIM, (h + 1) * HEAD_DIM)
            o = acc_ref[h] / l_ref[h]
            o_ref[:, hs] = (o * ga_ref[:, hs].astype(F32)).astype(BF16)


def _causal_steps(nq, tq, tk):
    qt, kt = [], []
    for i in range(nq):
        for j in range(((i + 1) * tq - 1) // tk + 1):
            qt.append(i)
            kt.append(j)
    return jnp.asarray(qt, jnp.int32), jnp.asarray(kt, jnp.int32)


def _attn_a(qi, wi, kab, qkr, vg, batch, seq, tq, tk, topk):
    m = batch * seq
    nq = seq // tq
    nk = seq // tk
    qt, kt = _causal_steps(nq, tq, tk)

    def qrow(b, s, qt_ref, kt_ref):
        return b * nq + qt_ref[s]

    def krow(b, s, qt_ref, kt_ref):
        return b * nk + kt_ref[s]

    grid_spec = pltpu.PrefetchScalarGridSpec(
        num_scalar_prefetch=2,
        grid=(batch, int(qt.shape[0])),
        in_specs=[
            pl.BlockSpec((IDX_HEADS // 2, tq, LANES), lambda b, s, q, k: (0, qrow(b, s, q, k), 0)),
            pl.BlockSpec((tq, LANES), lambda b, s, q, k: (qrow(b, s, q, k), 0)),
            pl.BlockSpec((2, seq, LANES), lambda b, s, q, k: (0, b, 0)),
            pl.BlockSpec((tq, A_WIDTH), lambda b, s, q, k: (qrow(b, s, q, k), 0)),
            pl.BlockSpec((tk, A_WIDTH), lambda b, s, q, k: (krow(b, s, q, k), 1)),
            pl.BlockSpec((tk, A_WIDTH), lambda b, s, q, k: (krow(b, s, q, k), 0)),
            pl.BlockSpec((tq, A_WIDTH), lambda b, s, q, k: (qrow(b, s, q, k), 2)),
        ],
        out_specs=pl.BlockSpec((tq, A_WIDTH), lambda b, s, q, k: (qrow(b, s, q, k), 0)),
        scratch_shapes=[
            pltpu.VMEM((nk, tq, tk), jnp.int32),
            pltpu.VMEM((2, tq, LANES), F32),
            pltpu.VMEM((tq, LANES), jnp.int32),
            pltpu.VMEM((tq, tk), F32),
            pltpu.VMEM((A_HEADS, tq, LANES), F32),
            pltpu.VMEM((A_HEADS, tq, LANES), F32),
            pltpu.VMEM((A_HEADS, tq, HEAD_DIM), F32),
        ],
    )
    return pl.pallas_call(
        functools.partial(_a_kernel, tq=tq, tk=tk, topk=topk, rows=SELECT_ROWS),
        grid_spec=grid_spec,
        out_shape=jax.ShapeDtypeStruct((m, A_WIDTH), BF16),
        compiler_params=_params(("parallel", "arbitrary")),
        name="attn_a",
    )(qt, kt, qi, wi, kab, qkr, qkr, vg, vg)


def _b_kernel(qt_ref, kt_ref, q_ref, k_ref, v_ref, sg_ref, gain_ref, lq1_ref, lk1_ref, lq2_ref, lk2_ref,
              o_ref, m_ref, l_ref, acc_ref, *, tq, lambda_init):
    step = pl.program_id(2)
    iq = qt_ref[step]
    kj = kt_ref[step]

    @pl.when(kj == 0)
    def _init():
        m_ref[...] = jnp.full(m_ref.shape, M_INIT, F32)
        l_ref[...] = jnp.zeros(l_ref.shape, F32)
        acc_ref[...] = jnp.zeros(acc_ref.shape, F32)

    def attend(diagonal):
        v = v_ref[...]
        if diagonal:
            causal = (lax.broadcasted_iota(jnp.int32, (tq, tq), 1)
                      <= lax.broadcasted_iota(jnp.int32, (tq, tq), 0))
        for c in range(2):
            cs = slice(c * HEAD_DIM, (c + 1) * HEAD_DIM)
            s = _nt_dot(q_ref[:, cs], k_ref[:, cs])
            if diagonal:
                s = jnp.where(causal, s, NEG_BIG)
            m_prev = m_ref[c]
            m_new = jnp.maximum(m_prev, jnp.max(s, axis=1, keepdims=True))
            alpha = jnp.exp2(m_prev - m_new)
            p = jnp.exp2(s - _tile_lanes(m_new, tq))
            l_ref[c] = alpha * l_ref[c] + jnp.sum(p, axis=1, keepdims=True)
            acc_ref[c] = _tile_lanes(alpha, 2 * HEAD_DIM) * acc_ref[c] + jnp.dot(
                p.astype(BF16), v, preferred_element_type=F32)
            m_ref[c] = m_new

    @pl.when(kj < iq)
    def _below_diagonal():
        attend(False)

    @pl.when(kj == iq)
    def _diagonal_and_finish():
        attend(True)
        lam = (jnp.exp(jnp.sum(lq1_ref[...] * lk1_ref[...], axis=1, keepdims=True))
               - jnp.exp(jnp.sum(lq2_ref[...] * lk2_ref[...], axis=1, keepdims=True))
               + lambda_init)
        o0 = acc_ref[0] / _tile_lanes(l_ref[0], 2 * HEAD_DIM)
        o1 = acc_ref[1] / _tile_lanes(l_ref[1], 2 * HEAD_DIM)
        o = o0 - lam * o1
        ms = jnp.mean(o * o, axis=-1, keepdims=True)
        y = o * lax.rsqrt(ms + SUBLN_EPS) * gain_ref[...] * (1.0 - lambda_init)
        o_ref[...] = (y * sg_ref[...].astype(F32)).astype(BF16)


def _attn_b(qkr, vg, gain, lq1, lk1, lq2, lk2, batch, seq, tq, lambda_init):
    m = batch * seq
    nq = seq // tq
    hw = 2 * HEAD_DIM
    qt, kt = _causal_steps(nq, tq, tq)

    def qrow(b, s, qt_ref):
        return b * nq + qt_ref[s]

    vec = pl.BlockSpec((1, HEAD_DIM), lambda b, h, s, q, k: (0, 0))
    grid_spec = pltpu.PrefetchScalarGridSpec(
        num_scalar_prefetch=2,
        grid=(batch, B_HEADS, int(qt.shape[0])),
        in_specs=[
            pl.BlockSpec((tq, hw), lambda b, h, s, q, k: (qrow(b, s, q), 2 * B_HEADS + h)),
            pl.BlockSpec((tq, hw), lambda b, h, s, q, k: (qrow(b, s, k), 3 * B_HEADS + h)),
            pl.BlockSpec((tq, hw), lambda b, h, s, q, k: (qrow(b, s, k), B_HEADS + h)),
            pl.BlockSpec((tq, hw), lambda b, h, s, q, k: (qrow(b, s, q), 3 * B_HEADS + h)),
            pl.BlockSpec((1, hw), lambda b, h, s, q, k: (0, 0)),
            vec, vec, vec, vec,
        ],
        out_specs=pl.BlockSpec((tq, hw), lambda b, h, s, q, k: (qrow(b, s, q), h)),
        scratch_shapes=[
            pltpu.VMEM((2, tq, LANES), F32),
            pltpu.VMEM((2, tq, LANES), F32),
            pltpu.VMEM((2, tq, hw), F32),
        ],
    )
    return pl.pallas_call(
        functools.partial(_b_kernel, tq=tq, lambda_init=lambda_init),
        grid_spec=grid_spec,
        out_shape=jax.ShapeDtypeStruct((m, B_WIDTH), BF16),
        compiler_params=_params(("parallel", "parallel", "arbitrary")),
        name="attn_b",
    )(qt, kt, qkr, qkr, vg, vg, gain, lq1, lk1, lq2, lk2)


def _merge_kernel(oa_ref, ob_ref, wa_ref, wb_ref, ma_ref, mb_ref, o_ref):
    ya = jnp.dot(oa_ref[...], wa_ref[...], preferred_element_type=F32)
    yb = jnp.dot(ob_ref[...], wb_ref[...], preferred_element_type=F32)
    o_ref[...] = (ma_ref[...].astype(F32) * ya + mb_ref[...].astype(F32) * yb).astype(BF16)


def _merge(oa, ob, w_a, w_b, vg, d_model, tm, tn):
    m = oa.shape[0]
    gate_a0 = (A_WIDTH + B_WIDTH) * 2 // tn
    gate_b0 = gate_a0 + d_model // tn
    return pl.pallas_call(
        _merge_kernel,
        grid=(m // tm, d_model // tn),
        in_specs=[
            pl.BlockSpec((tm, A_WIDTH), lambda i, j: (i, 0)),
            pl.BlockSpec((tm, B_WIDTH), lambda i, j: (i, 0)),
            pl.BlockSpec((A_WIDTH, tn), lambda i, j: (0, j)),
            pl.BlockSpec((B_WIDTH, tn), lambda i, j: (0, j)),
            pl.BlockSpec((tm, tn), lambda i, j: (i, gate_a0 + j)),
            pl.BlockSpec((tm, tn), lambda i, j: (i, gate_b0 + j)),
        ],
        out_specs=pl.BlockSpec((tm, tn), lambda i, j: (i, j)),
        out_shape=jax.ShapeDtypeStruct((m, d_model), BF16),
        compiler_params=_params(("parallel", "arbitrary")),
        name="merge",
    )(oa, ob, w_a, w_b, vg, vg)


def _out_kernel(mg_ref, w_ref, x_ref, o_ref):
    o_ref[...] = x_ref[...] + jnp.dot(mg_ref[...], w_ref[...], preferred_element_type=F32)


def _out_proj(merged, w_out, x2d, tm, tn):
    m, d = x2d.shape
    return pl.pallas_call(
        _out_kernel,
        grid=(m // tm, d // tn),
        in_specs=[
            pl.BlockSpec((tm, d), lambda i, j: (i, 0)),
            pl.BlockSpec((d, tn), lambda i, j: (0, j)),
            pl.BlockSpec((tm, tn), lambda i, j: (i, j)),
        ],
        out_specs=pl.BlockSpec((tm, tn), lambda i, j: (i, j)),
        out_shape=jax.ShapeDtypeStruct((m, d), F32),
        compiler_params=_params(("parallel", "arbitrary")),
        name="out_proj",
    )(merged, w_out, x2d)


def _qi_column_order():
    lane = np.arange(LANES)
    quarter, r = lane // 32, lane % 32
    dim = r + 32 * (quarter // 2)
    q_cols = np.concatenate([(2 * p + quarter % 2) * IDX_DIM + dim for p in range(IDX_HEADS // 2)])
    return q_cols, dim, (quarter % 2 == 0)


def _layer(x2d, tabs, norm_gain, w_in, a_q_gain, a_k_gain, b_q_gain, b_k_gain,
           lq1, lk1, lq2, lk2, b_subln_gain, w_o_a, w_o_b, w_out, layer, batch, seq):
    d_model = x2d.shape[1]
    cos_h, sin_h, cos_i, sin_i = tabs
    sizes = (A_WIDTH,) * 4 + (IDX_HEADS * IDX_DIM, IDX_DIM, IDX_HEADS) + (B_WIDTH,) * 4 + (d_model,) * 2
    off = np.concatenate([[0], np.cumsum(sizes)])
    seg = lambda i: w_in[:, off[i]:off[i + 1]]
    (w_qa, w_ka, w_va, w_ga, w_qi, w_ki, w_wi, w_qb, w_kb, w_vb, w_gb, w_ma, w_mb) = [seg(i) for i in range(13)]

    scale = HEAD_DIM ** -0.5 * LOG2E
    w_qk = jnp.concatenate([w_qa, w_ka, w_qb, w_kb], axis=1).astype(BF16)
    gain_qk = jnp.concatenate([jnp.tile(a_q_gain * scale, A_HEADS), jnp.tile(a_k_gain, A_HEADS),
                               jnp.tile(b_q_gain * scale, 2 * B_HEADS), jnp.tile(b_k_gain, 2 * B_HEADS)])[None, :]
    q_cols, k_dim, even = _qi_column_order()
    w_qi_p = w_qi[:, q_cols].astype(BF16)
    w_k_rep = w_ki[:, k_dim]
    zeros = jnp.zeros_like(w_k_rep)
    w_kw = jnp.concatenate([
        jnp.where(even[None, :], w_k_rep, zeros), jnp.where(even[None, :], zeros, w_k_rep),
        jnp.pad(w_wi, ((0, 0), (0, 2 * LANES - IDX_HEADS)))], axis=1).astype(BF16)
    w_vg = jnp.concatenate([w_va, w_vb, w_ga, w_gb, w_ma, w_mb], axis=1).astype(BF16)

    m = x2d.shape[0]
    tm, tn = 1024, 256
    h = _norm(x2d, norm_gain[None, :], 512)

    tab_spec = pl.BlockSpec((tm, LANES), lambda i, j: (i, 0))
    qkr = _proj(h, w_qk, (gain_qk, cos_h, sin_h),
                [pl.BlockSpec((1, tn), lambda i, j: (0, j)), tab_spec, tab_spec], _ep_qk,
                jax.ShapeDtypeStruct((m, w_qk.shape[1]), BF16),
                pl.BlockSpec((tm, tn), lambda i, j: (i, j)), tm, tn, "proj_qk")
    qi = _proj(h, w_qi_p, (cos_i, sin_i), [tab_spec, tab_spec], _ep_qi,
               jax.ShapeDtypeStruct((IDX_HEADS // 2, m, LANES), BF16),
               pl.BlockSpec((tn // LANES, tm, LANES), lambda i, j: (j, i, 0)), tm, tn, "proj_qi")
    kab, wi = _proj(h, w_kw, (cos_i, sin_i), [tab_spec, tab_spec], _ep_kw,
                    (jax.ShapeDtypeStruct((2, m, LANES), BF16), jax.ShapeDtypeStruct((m, LANES), F32)),
                    (pl.BlockSpec((2, tm, LANES), lambda i, j: (0, i, 0)),
                     pl.BlockSpec((tm, LANES), lambda i, j: (i, 0))), tm, tn, "proj_kw")
    n_plain = (A_WIDTH + B_WIDTH) // tn
    vg = _proj(h, w_vg, (), [], functools.partial(_ep_vg, n_plain=n_plain, n_silu=n_plain),
               jax.ShapeDtypeStruct((m, w_vg.shape[1]), BF16),
               pl.BlockSpec((tm, tn), lambda i, j: (i, j)), tm, tn, "proj_vg")

    topk = min(TOPK_MAX, seq // 4)
    oa = _attn_a(qi, wi, kab, qkr, vg, batch, seq, 256, 512, topk)
    lambda_init = 0.8 - 0.6 * math.exp(-0.3 * layer)
    ob = _attn_b(qkr, vg, b_subln_gain[None, :], lq1[None, :], lk1[None, :], lq2[None, :], lk2[None, :],
                 batch, seq, 512, lambda_init)
    merged = _merge(oa, ob, w_o_a.astype(BF16), w_o_b.astype(BF16), vg, d_model, 1024, 512)
    return _out_proj(merged, w_out.astype(BF16), x2d, 1024, 512)


def kernel(x, positions, norm_gain, w_in, a_q_gain, a_k_gain, b_q_gain, b_k_gain, lambda_q1, lambda_k1,
           lambda_q2, lambda_k2, b_subln_gain, w_o_a, w_o_b, w_out):
    batch, seq, d_model = x.shape
    m = batch * seq
    pos = positions.astype(F32).reshape(m, 1)
    ang_h = pos * ROPE_THETA ** (-jnp.arange(0, HEAD_DIM, 2, dtype=F32) / HEAD_DIM)
    ang_i = pos * ROPE_THETA ** (-jnp.arange(0, IDX_DIM, 2, dtype=F32) / IDX_DIM)
    ch, sh, ci, si = jnp.cos(ang_h), jnp.sin(ang_h), jnp.cos(ang_i), jnp.sin(ang_i)
    tabs = (jnp.concatenate([ch, ch], axis=1), jnp.concatenate([-sh, sh], axis=1),
            jnp.concatenate([ci] * 4, axis=1), jnp.concatenate([-si, -si, si, si], axis=1))
    x2d = x.reshape(m, d_model)
    for layer in range(norm_gain.shape[0]):
        x2d = _layer(x2d, tabs, norm_gain[layer], w_in[layer], a_q_gain[layer], a_k_gain[layer],
                     b_q_gain[layer], b_k_gain[layer], lambda_q1[layer], lambda_k1[layer],
                     lambda_q2[layer], lambda_k2[layer], b_subln_gain[layer], w_o_a[layer], w_o_b[layer],
                     w_out[layer], layer, batch, seq)
    return x2d.reshape(batch, seq, d_model)
```

```python
import functools
import math

import jax
import jax.numpy as jnp
import numpy as np
from jax import lax
from jax.experimental import pallas as pl
from jax.experimental.pallas import tpu as pltpu

F32 = jnp.float32
BF16 = jnp.bfloat16

HEAD_DIM = 128
A_HEADS = 8
A_WIDTH = A_HEADS * HEAD_DIM
IDX_HEADS = 16
IDX_DIM = 64
TOPK_MAX = 256
B_HEADS = 4
B_WIDTH = B_HEADS * 2 * HEAD_DIM
ROPE_THETA = 10000.0
NORM_EPS = 1e-6
SUBLN_EPS = 1e-5

LANES = 128
INT_MIN = -(2 ** 31)
NEG_BIG = -1e30
M_INIT = -5e29
LOG2E = math.log2(math.e)
SELECT_ROWS = 64
SOFTMAX_ROWS = 32
PROJ_SUB = 256
VMEM_LIMIT_BYTES = 56 * 1024 * 1024


def _nt_dot(a, b):
    return lax.dot_general(a, b, (((1,), (1,)), ((), ())), preferred_element_type=F32)


def _tile_lanes(v, width):
    reps = width // LANES
    return v if reps == 1 else jnp.concatenate([v] * reps, axis=1)


def _params(semantics):
    return pltpu.CompilerParams(dimension_semantics=semantics, vmem_limit_bytes=VMEM_LIMIT_BYTES)


def _norm_kernel(x_ref, g_ref, h_ref):
    x = x_ref[...]
    ms = jnp.mean(x * x, axis=-1, keepdims=True)
    h_ref[...] = (x * lax.rsqrt(ms + NORM_EPS) * g_ref[...]).astype(BF16)


def _norm(x2d, gain, tm):
    m, d = x2d.shape
    return pl.pallas_call(
        _norm_kernel,
        grid=(m // tm,),
        in_specs=[pl.BlockSpec((tm, d), lambda i: (i, 0)), pl.BlockSpec((1, d), lambda i: (0, 0))],
        out_specs=pl.BlockSpec((tm, d), lambda i: (i, 0)),
        out_shape=jax.ShapeDtypeStruct((m, d), BF16),
        compiler_params=_params(("parallel",)),
        name="norm",
    )(x2d, gain)


def _proj_body(h_ref, w_ref, *rest, n_aux, epilogue):
    tn = w_ref.shape[1]

    def sub_dot(t):
        return jnp.dot(h_ref[...], w_ref[:, t * PROJ_SUB:(t + 1) * PROJ_SUB], preferred_element_type=F32)

    epilogue(sub_dot, tn // PROJ_SUB, rest[:n_aux], rest[n_aux:])


def _proj(h, w, aux, aux_specs, epilogue, out_shape, out_specs, tm, tn, name):
    m, k = h.shape
    n = w.shape[1]
    return pl.pallas_call(
        functools.partial(_proj_body, n_aux=len(aux), epilogue=epilogue),
        grid=(m // tm, n // tn),
        in_specs=[pl.BlockSpec((tm, k), lambda i, j: (i, 0)),
                  pl.BlockSpec((k, tn), lambda i, j: (0, j))] + aux_specs,
        out_specs=out_specs,
        out_shape=out_shape,
        compiler_params=_params(("parallel", "arbitrary")),
        name=name,
    )(h, w, *aux)


def _rope128(y, cos_full, sin_signed):
    return y * cos_full + pltpu.roll(y, 64, 1) * sin_signed


def _proj_qk_kernel(h_ref, w_ref, g_ref, c_ref, s_ref, o_ref, acc_ref):
    acc_ref[...] = jnp.dot(h_ref[...], w_ref[...], preferred_element_type=F32)
    same_head = (lax.broadcasted_iota(jnp.int32, (PROJ_SUB, PROJ_SUB), 0) // HEAD_DIM
                 == lax.broadcasted_iota(jnp.int32, (PROJ_SUB, PROJ_SUB), 1) // HEAD_DIM)
    head_mean = jnp.where(same_head, 1.0 / HEAD_DIM, 0.0).astype(BF16)

    def sub_tile(t, carry):
        cs = pl.ds(pl.multiple_of(t * PROJ_SUB, PROJ_SUB), PROJ_SUB)
        acc = acc_ref[:, cs]
        ms = jnp.dot((acc * acc).astype(BF16), head_mean, preferred_element_type=F32)
        y = acc * lax.rsqrt(ms + NORM_EPS) * g_ref[:, cs]
        c = c_ref[...]
        s = s_ref[...]
        for u in range(PROJ_SUB // LANES):
            ls = pl.ds(pl.multiple_of(t * PROJ_SUB + u * LANES, LANES), LANES)
            o_ref[:, ls] = _rope128(y[:, u * LANES:(u + 1) * LANES], c, s).astype(BF16)
        return carry

    lax.fori_loop(0, acc_ref.shape[1] // PROJ_SUB, sub_tile, 0)


def _proj_qk(h, w, gain, cos_full, sin_signed, tm, tn):
    m, k = h.shape
    n = w.shape[1]
    tab_spec = pl.BlockSpec((tm, LANES), lambda i, j: (i, 0))
    return pl.pallas_call(
        _proj_qk_kernel,
        grid=(m // tm, n // tn),
        in_specs=[pl.BlockSpec((tm, k), lambda i, j: (i, 0)),
                  pl.BlockSpec((k, tn), lambda i, j: (0, j)),
                  pl.BlockSpec((1, tn), lambda i, j: (0, j)), tab_spec, tab_spec],
        out_specs=pl.BlockSpec((tm, tn), lambda i, j: (i, j)),
        out_shape=jax.ShapeDtypeStruct((m, n), BF16),
        scratch_shapes=[pltpu.VMEM((tm, tn), F32)],
        compiler_params=_params(("parallel", "arbitrary")),
        name="proj_qk",
    )(h, w, gain, cos_full, sin_signed)


def _ep_qi(sub_dot, n_sub, aux, outs):
    c_ref, s_ref = aux
    (o_ref,) = outs
    c = c_ref[...]
    s = s_ref[...]
    per = PROJ_SUB // LANES
    for t in range(n_sub):
        acc = sub_dot(t)
        for u in range(per):
            o_ref[t * per + u] = _rope128(acc[:, u * LANES:(u + 1) * LANES], c, s).astype(BF16)


def _ep_kw(sub_dot, n_sub, aux, outs):
    c_ref, s_ref = aux
    kab_ref, w_ref = outs
    c = c_ref[...]
    s = s_ref[...]
    acc = sub_dot(0)
    for u in range(2):
        kab_ref[u] = _rope128(acc[:, u * LANES:(u + 1) * LANES], c, s).astype(BF16)
    w_ref[...] = sub_dot(1)[:, :LANES] * (IDX_HEADS ** -0.5 * IDX_DIM ** -0.5)


def _ep_vg(sub_dot, n_sub, aux, outs, *, n_plain, n_silu):
    (o_ref,) = outs
    j = pl.program_id(1)

    def store(fn):
        for t in range(n_sub):
            o_ref[:, t * PROJ_SUB:(t + 1) * PROJ_SUB] = fn(sub_dot(t)).astype(BF16)

    @pl.when(j < n_plain)
    def _():
        store(lambda a: a)

    @pl.when(jnp.logical_and(j >= n_plain, j < n_plain + n_silu))
    def _():
        store(lambda a: a * jax.nn.sigmoid(a))

    @pl.when(j >= n_plain + n_silu)
    def _():
        store(jax.nn.sigmoid)


def _f32_key(v):
    bits = pltpu.bitcast(v, jnp.int32)
    return bits ^ ((bits >> 31) & 0x7FFFFFFF)


def _smear_right(x):
    for sh in (1, 2, 4, 8, 16):
        x = x | lax.shift_right_logical(x, jnp.int32(sh))
    return x


def _softmax_rows(s_ref, p_ref, alpha_ref, m_ref, l_ref, adjust):
    tq, tk = s_ref.shape
    for r in range(tq // SOFTMAX_ROWS):
        rs = slice(r * SOFTMAX_ROWS, (r + 1) * SOFTMAX_ROWS)
        lanes = [slice(t * LANES, (t + 1) * LANES) for t in range(tk // LANES)]
        s = [adjust(s_ref[rs, ls], rs, ls) for ls in lanes]
        m_prev = m_ref[rs, :]
        m_new = jnp.maximum(m_prev, jnp.max(functools.reduce(jnp.maximum, s), axis=1, keepdims=True))
        alpha = jnp.exp2(m_prev - m_new)
        p = [jnp.exp2(x - m_new) for x in s]
        l_ref[rs, :] = alpha * l_ref[rs, :] + jnp.sum(functools.reduce(jnp.add, p), axis=1, keepdims=True)
        for ls, x in zip(lanes, p):
            p_ref[rs, ls] = x.astype(BF16)
        alpha_ref[rs, :] = alpha
        m_ref[rs, :] = m_new


def _a_kernel(qt_ref, kt_ref, qi_ref, wi_ref, kab_ref, qa_ref, ka_ref, va_ref, ga_ref, o_ref,
              sc_ref, gmax_ref, tau_ref, open_ref, thr_ref, cand_ref, done_ref, cnt_ref, bias_ref,
              s_ref, p_ref, alpha_ref, m_ref, l_ref, acc_ref, *, tq, tk, topk, rows):
    step = pl.program_id(1)
    iq = qt_ref[step]
    kj = kt_ref[step]
    last = ((iq + 1) * tq - 1) // tk

    @pl.when(kj == 0)
    def _index_and_select():
        w = wi_ref[...]
        row = iq * tq + lax.broadcasted_iota(jnp.int32, (tq, tk), 0)
        col0 = lax.broadcasted_iota(jnp.int32, (tq, tk), 1)
        gmax_ref[...] = jnp.full(gmax_ref.shape, -jnp.inf, F32)

        def score_tile(j, carry):
            start = pl.multiple_of(j * tk, tk)
            k_even = kab_ref[0, pl.ds(start, tk), :]
            k_odd = kab_ref[1, pl.ds(start, tk), :]
            acc = jnp.zeros((tq, tk), F32)
            for p in range(IDX_HEADS // 2):
                q_p = qi_ref[p]
                s_even = jnp.maximum(_nt_dot(q_p, k_even), 0.0)
                s_odd = jnp.maximum(_nt_dot(q_p, k_odd), 0.0)
                acc = acc + w[:, 2 * p:2 * p + 1] * s_even + w[:, 2 * p + 1:2 * p + 2] * s_odd
            causal = col0 + j * tk <= row
            sc_ref[j] = jnp.where(causal, _f32_key(acc), INT_MIN)
            accm = jnp.where(causal, acc, -jnp.inf)
            half = tk // 2
            for g in range(2):
                gm = gmax_ref[g]
                for t in range(half // LANES):
                    lo = g * half + t * LANES
                    gm = jnp.maximum(gm, accm[:, lo:lo + LANES])
                gmax_ref[g] = gm
            return carry

        lax.fori_loop(0, last + 1, score_tile, 0)

        g0 = gmax_ref[0]
        g1 = gmax_ref[1]
        row_max = jnp.max(jnp.maximum(g0, g1), axis=1, keepdims=True)
        row_low = jnp.min(jnp.minimum(g0, g1), axis=1, keepdims=True)
        hi_u = jnp.broadcast_to(_f32_key(row_max), (tq, LANES)) ^ INT_MIN
        lo_u = jnp.broadcast_to(_f32_key(row_low), (tq, LANES)) ^ INT_MIN
        open_bits = _smear_right(hi_u ^ lo_u)
        open_ref[...] = open_bits
        thr_ref[...] = lo_u & ~open_bits
        done_ref[...] = jnp.where(open_bits == 0, 1.0, 0.0)
        undecided = 32 - lax.clz(open_bits)
        first_bit = jnp.max(undecided.astype(F32)).astype(jnp.int32) - 1

        def search_cond(c):
            b, stop = c
            return jnp.logical_and(b >= 0, stop == 0)

        def search_step(c):
            b, _ = c
            stop = (jnp.min(done_ref[...]) > 0.0).astype(jnp.int32)
            bit = jnp.left_shift(jnp.int32(1), b)
            cand_ref[...] = (thr_ref[...] | bit) ^ INT_MIN
            for rc in range(tq // rows):
                rs = pl.ds(rc * rows, rows)
                cand = cand_ref[rs, :]

                def count_tile(j, cacc):
                    keys = sc_ref[j, rs, :]
                    for t in range(tk // LANES):
                        cacc = cacc + jnp.where(keys[:, t * LANES:(t + 1) * LANES] >= cand, 1.0, 0.0)
                    return cacc

                cnt_ref[rs, :] = lax.fori_loop(0, last + 1, count_tile, jnp.zeros((rows, LANES), F32))
            cnt = jnp.sum(cnt_ref[...], axis=1, keepdims=True)
            opn = open_ref[...]
            done = done_ref[...]
            active = jnp.logical_and((opn & bit) != 0, done == 0.0)
            take = jnp.logical_and(active, cnt >= float(topk))
            thr_ref[...] = jnp.where(take, cand_ref[...] ^ INT_MIN, thr_ref[...])
            finished = jnp.logical_or(jnp.logical_and(active, cnt == float(topk)), (opn & (bit - 1)) == 0)
            done_ref[...] = jnp.where(finished, 1.0, done)
            return b - 1, stop

        lax.while_loop(search_cond, search_step, (first_bit, jnp.int32(0)))
        tau_ref[...] = jnp.maximum(thr_ref[...] ^ INT_MIN, INT_MIN + 1)

        m_ref[...] = jnp.full(m_ref.shape, M_INIT, F32)
        l_ref[...] = jnp.zeros(l_ref.shape, F32)
        acc_ref[...] = jnp.zeros(acc_ref.shape, F32)

    tau = tau_ref[...]
    for t in range(tk // LANES):
        ls = slice(t * LANES, (t + 1) * LANES)
        bias_ref[:, ls] = jnp.where(sc_ref[kj, :, ls] >= tau, 0.0, NEG_BIG)

    def add_bias(x, rs, ls):
        return x + bias_ref[rs, ls]

    for h in range(A_HEADS):
        buf = h % 2
        hs = slice(h * HEAD_DIM, (h + 1) * HEAD_DIM)
        s_ref[buf] = _nt_dot(qa_ref[:, hs], ka_ref[:, hs])
        _softmax_rows(s_ref.at[buf], p_ref.at[buf], alpha_ref.at[buf], m_ref.at[h], l_ref.at[h], add_bias)
        acc_ref[h] = alpha_ref[buf] * acc_ref[h] + jnp.dot(p_ref[buf], va_ref[:, hs],
                                                           preferred_element_type=F32)

    @pl.when(kj == last)
    def _finish():
        for h in range(A_HEADS):
            hs = slice(h * HEAD_DIM, (h + 1) * HEAD_DIM)
            o = acc_ref[h] / l_ref[h]
            o_ref[:, hs] = (o * ga_ref[:, hs].astype(F32)).astype(BF16)


def _causal_steps(nq, tq, tk):
    qt, kt = [], []
    for i in range(nq):
        for j in range(((i + 1) * tq - 1) // tk + 1):
            qt.append(i)
            kt.append(j)
    return jnp.asarray(qt, jnp.int32), jnp.asarray(kt, jnp.int32)


def _attn_a(qi, wi, kab, qkr, vg, batch, seq, tq, tk, topk):
    m = batch * seq
    nq = seq // tq
    nk = seq // tk
    qt, kt = _causal_steps(nq, tq, tk)

    def qrow(b, s, qt_ref, kt_ref):
        return b * nq + qt_ref[s]

    def krow(b, s, qt_ref, kt_ref):
        return b * nk + kt_ref[s]

    grid_spec = pltpu.PrefetchScalarGridSpec(
        num_scalar_prefetch=2,
        grid=(batch, int(qt.shape[0])),
        in_specs=[
            pl.BlockSpec((IDX_HEADS // 2, tq, LANES), lambda b, s, q, k: (0, qrow(b, s, q, k), 0)),
            pl.BlockSpec((tq, LANES), lambda b, s, q, k: (qrow(b, s, q, k), 0)),
            pl.BlockSpec((2, seq, LANES), lambda b, s, q, k: (0, b, 0)),
            pl.BlockSpec((tq, A_WIDTH), lambda b, s, q, k: (qrow(b, s, q, k), 0)),
            pl.BlockSpec((tk, A_WIDTH), lambda b, s, q, k: (krow(b, s, q, k), 1)),
            pl.BlockSpec((tk, A_WIDTH), lambda b, s, q, k: (krow(b, s, q, k), 0)),
            pl.BlockSpec((tq, A_WIDTH), lambda b, s, q, k: (qrow(b, s, q, k), 2)),
        ],
        out_specs=pl.BlockSpec((tq, A_WIDTH), lambda b, s, q, k: (qrow(b, s, q, k), 0)),
        scratch_shapes=[
            pltpu.VMEM((nk, tq, tk), jnp.int32),
            pltpu.VMEM((2, tq, LANES), F32),
            pltpu.VMEM((tq, LANES), jnp.int32),
            pltpu.VMEM((tq, LANES), jnp.int32),
            pltpu.VMEM((tq, LANES), jnp.int32),
            pltpu.VMEM((tq, LANES), jnp.int32),
            pltpu.VMEM((tq, LANES), F32),
            pltpu.VMEM((tq, LANES), F32),
            pltpu.VMEM((tq, tk), F32),
            pltpu.VMEM((2, tq, tk), F32),
            pltpu.VMEM((2, tq, tk), BF16),
            pltpu.VMEM((2, tq, LANES), F32),
            pltpu.VMEM((A_HEADS, tq, LANES), F32),
            pltpu.VMEM((A_HEADS, tq, LANES), F32),
            pltpu.VMEM((A_HEADS, tq, HEAD_DIM), F32),
        ],
    )
    return pl.pallas_call(
        functools.partial(_a_kernel, tq=tq, tk=tk, topk=topk, rows=SELECT_ROWS),
        grid_spec=grid_spec,
        out_shape=jax.ShapeDtypeStruct((m, A_WIDTH), BF16),
        compiler_params=_params(("parallel", "arbitrary")),
        name="attn_a",
    )(qt, kt, qi, wi, kab, qkr, qkr, vg, vg)


def _b_kernel(qt_ref, kt_ref, q_ref, k_ref, v_ref, sg_ref, gain_ref, lq1_ref, lk1_ref, lq2_ref, lk2_ref,
              o_ref, s_ref, p_ref, alpha_ref, m_ref, l_ref, acc_ref, *, tq, heads, lambda_init):
    step = pl.program_id(2)
    iq = qt_ref[step]
    kj = kt_ref[step]
    hw = 2 * HEAD_DIM
    chains = [(hh, c) for hh in range(heads) for c in range(2)]

    @pl.when(kj == 0)
    def _init():
        m_ref[...] = jnp.full(m_ref.shape, M_INIT, F32)
        l_ref[...] = jnp.zeros(l_ref.shape, F32)
        acc_ref[...] = jnp.zeros(acc_ref.shape, F32)

    def causal_mask(x, rs, ls):
        row = rs.start + lax.broadcasted_iota(jnp.int32, x.shape, 0)
        col = ls.start + lax.broadcasted_iota(jnp.int32, x.shape, 1)
        return jnp.where(col <= row, x, NEG_BIG)

    def attend(diagonal):
        adjust = causal_mask if diagonal else (lambda x, rs, ls: x)
        for i, (hh, c) in enumerate(chains):
            cs = slice(hh * hw + c * HEAD_DIM, hh * hw + (c + 1) * HEAD_DIM)
            s_ref[i] = _nt_dot(q_ref[:, cs], k_ref[:, cs])
        for i in range(len(chains)):
            _softmax_rows(s_ref.at[i], p_ref.at[i], alpha_ref.at[i], m_ref.at[i], l_ref.at[i], adjust)
        for i, (hh, c) in enumerate(chains):
            pv = jnp.dot(p_ref[i], v_ref[:, hh * hw:(hh + 1) * hw], preferred_element_type=F32)
            alpha = alpha_ref[i]
            for t in range(hw // LANES):
                ls = slice(t * LANES, (t + 1) * LANES)
                acc_ref[i, :, ls] = alpha * acc_ref[i, :, ls] + pv[:, ls]

    @pl.when(kj < iq)
    def _below_diagonal():
        attend(False)

    @pl.when(kj == iq)
    def _diagonal_and_finish():
        attend(True)
        lam = (jnp.exp(jnp.sum(lq1_ref[...] * lk1_ref[...], axis=1, keepdims=True))
               - jnp.exp(jnp.sum(lq2_ref[...] * lk2_ref[...], axis=1, keepdims=True))
               + lambda_init)
        for hh in range(heads):
            o0 = acc_ref[2 * hh] / _tile_lanes(l_ref[2 * hh], hw)
            o1 = acc_ref[2 * hh + 1] / _tile_lanes(l_ref[2 * hh + 1], hw)
            o = o0 - lam * o1
            ms = jnp.mean(o * o, axis=-1, keepdims=True)
            y = o * lax.rsqrt(ms + SUBLN_EPS) * gain_ref[...] * (1.0 - lambda_init)
            hs = slice(hh * hw, (hh + 1) * hw)
            o_ref[:, hs] = (y * sg_ref[:, hs].astype(F32)).astype(BF16)


def _attn_b(qkr, vg, gain, lq1, lk1, lq2, lk2, batch, seq, tq, heads, lambda_init):
    m = batch * seq
    nq = seq // tq
    bw = heads * 2 * HEAD_DIM
    groups = B_HEADS // heads
    chains = 2 * heads
    qt, kt = _causal_steps(nq, tq, tq)

    def qrow(b, s, qt_ref):
        return b * nq + qt_ref[s]

    vec = pl.BlockSpec((1, HEAD_DIM), lambda b, h, s, q, k: (0, 0))
    grid_spec = pltpu.PrefetchScalarGridSpec(
        num_scalar_prefetch=2,
        grid=(batch, groups, int(qt.shape[0])),
        in_specs=[
            pl.BlockSpec((tq, bw), lambda b, h, s, q, k: (qrow(b, s, q), 2 * groups + h)),
            pl.BlockSpec((tq, bw), lambda b, h, s, q, k: (qrow(b, s, k), 3 * groups + h)),
            pl.BlockSpec((tq, bw), lambda b, h, s, q, k: (qrow(b, s, k), groups + h)),
            pl.BlockSpec((tq, bw), lambda b, h, s, q, k: (qrow(b, s, q), 3 * groups + h)),
            pl.BlockSpec((1, 2 * HEAD_DIM), lambda b, h, s, q, k: (0, 0)),
            vec, vec, vec, vec,
        ],
        out_specs=pl.BlockSpec((tq, bw), lambda b, h, s, q, k: (qrow(b, s, q), h)),
        scratch_shapes=[
            pltpu.VMEM((chains, tq, tq), F32),
            pltpu.VMEM((chains, tq, tq), BF16),
            pltpu.VMEM((chains, tq, LANES), F32),
            pltpu.VMEM((chains, tq, LANES), F32),
            pltpu.VMEM((chains, tq, LANES), F32),
            pltpu.VMEM((chains, tq, 2 * HEAD_DIM), F32),
        ],
    )
    return pl.pallas_call(
        functools.partial(_b_kernel, tq=tq, heads=heads, lambda_init=lambda_init),
        grid_spec=grid_spec,
        out_shape=jax.ShapeDtypeStruct((m, B_WIDTH), BF16),
        compiler_params=_params(("parallel", "parallel", "arbitrary")),
        name="attn_b",
    )(qt, kt, qkr, qkr, vg, vg, gain, lq1, lk1, lq2, lk2)


def _merge_kernel(oa_ref, ob_ref, wa_ref, wb_ref, ma_ref, mb_ref, o_ref):
    ya = jnp.dot(oa_ref[...], wa_ref[...], preferred_element_type=F32)
    yb = jnp.dot(ob_ref[...], wb_ref[...], preferred_element_type=F32)
    o_ref[...] = (ma_ref[...].astype(F32) * ya + mb_ref[...].astype(F32) * yb).astype(BF16)


def _merge(oa, ob, w_a, w_b, vg, d_model, tm, tn):
    m = oa.shape[0]
    gate_a0 = (A_WIDTH + B_WIDTH) * 2 // tn
    gate_b0 = gate_a0 + d_model // tn
    return pl.pallas_call(
        _merge_kernel,
        grid=(m // tm, d_model // tn),
        in_specs=[
            pl.BlockSpec((tm, A_WIDTH), lambda i, j: (i, 0)),
            pl.BlockSpec((tm, B_WIDTH), lambda i, j: (i, 0)),
            pl.BlockSpec((A_WIDTH, tn), lambda i, j: (0, j)),
            pl.BlockSpec((B_WIDTH, tn), lambda i, j: (0, j)),
            pl.BlockSpec((tm, tn), lambda i, j: (i, gate_a0 + j)),
            pl.BlockSpec((tm, tn), lambda i, j: (i, gate_b0 + j)),
        ],
        out_specs=pl.BlockSpec((tm, tn), lambda i, j: (i, j)),
        out_shape=jax.ShapeDtypeStruct((m, d_model), BF16),
        compiler_params=_params(("parallel", "arbitrary")),
        name="merge",
    )(oa, ob, w_a, w_b, vg, vg)


def _out_kernel(mg_ref, w_ref, x_ref, o_ref):
    o_ref[...] = x_ref[...] + jnp.dot(mg_ref[...], w_ref[...], preferred_element_type=F32)


def _out_proj(merged, w_out, x2d, tm, tn):
    m, d = x2d.shape
    return pl.pallas_call(
        _out_kernel,
        grid=(m // tm, d // tn),
        in_specs=[
            pl.BlockSpec((tm, d), lambda i, j: (i, 0)),
            pl.BlockSpec((d, tn), lambda i, j: (0, j)),
            pl.BlockSpec((tm, tn), lambda i, j: (i, j)),
        ],
        out_specs=pl.BlockSpec((tm, tn), lambda i, j: (i, j)),
        out_shape=jax.ShapeDtypeStruct((m, d), F32),
        compiler_params=_params(("parallel", "arbitrary")),
        name="out_proj",
    )(merged, w_out, x2d)


def _qi_column_order():
    lane = np.arange(LANES)
    quarter, r = lane // 32, lane % 32
    dim = r + 32 * (quarter // 2)
    q_cols = np.concatenate([(2 * p + quarter % 2) * IDX_DIM + dim for p in range(IDX_HEADS // 2)])
    return q_cols, dim, (quarter % 2 == 0)


def _layer(x2d, tabs, norm_gain, w_in, a_q_gain, a_k_gain, b_q_gain, b_k_gain,
           lq1, lk1, lq2, lk2, b_subln_gain, w_o_a, w_o_b, w_out, layer, batch, seq):
    d_model = x2d.shape[1]
    cos_h, sin_h, cos_i, sin_i = tabs
    sizes = (A_WIDTH,) * 4 + (IDX_HEADS * IDX_DIM, IDX_DIM, IDX_HEADS) + (B_WIDTH,) * 4 + (d_model,) * 2
    off = np.concatenate([[0], np.cumsum(sizes)])
    seg = lambda i: w_in[:, off[i]:off[i + 1]]
    (w_qa, w_ka, w_va, w_ga, w_qi, w_ki, w_wi, w_qb, w_kb, w_vb, w_gb, w_ma, w_mb) = [seg(i) for i in range(13)]

    scale = HEAD_DIM ** -0.5 * LOG2E
    w_qk = jnp.concatenate([w_qa, w_ka, w_qb, w_kb], axis=1).astype(BF16)
    gain_qk = jnp.concatenate([jnp.tile(a_q_gain * scale, A_HEADS), jnp.tile(a_k_gain, A_HEADS),
                               jnp.tile(b_q_gain * scale, 2 * B_HEADS), jnp.tile(b_k_gain, 2 * B_HEADS)])[None, :]
    q_cols, k_dim, even = _qi_column_order()
    w_qi_p = w_qi[:, q_cols].astype(BF16)
    w_k_rep = w_ki[:, k_dim]
    zeros = jnp.zeros_like(w_k_rep)
    w_kw = jnp.concatenate([
        jnp.where(even[None, :], w_k_rep, zeros), jnp.where(even[None, :], zeros, w_k_rep),
        jnp.pad(w_wi, ((0, 0), (0, 2 * LANES - IDX_HEADS)))], axis=1).astype(BF16)
    w_vg = jnp.concatenate([w_va, w_vb, w_ga, w_gb, w_ma, w_mb], axis=1).astype(BF16)

    m = x2d.shape[0]
    tm, tn = 1024, 1024
    h = _norm(x2d, norm_gain[None, :], 512)

    tab_spec = pl.BlockSpec((tm, LANES), lambda i, j: (i, 0))
    qkr = _proj_qk(h, w_qk, gain_qk, cos_h, sin_h, tm, tn)
    qi = _proj(h, w_qi_p, (cos_i, sin_i), [tab_spec, tab_spec], _ep_qi,
               jax.ShapeDtypeStruct((IDX_HEADS // 2, m, LANES), BF16),
               pl.BlockSpec((tn // LANES, tm, LANES), lambda i, j: (j, i, 0)), tm, tn, "proj_qi")
    kab, wi = _proj(h, w_kw, (cos_i, sin_i), [tab_spec, tab_spec], _ep_kw,
                    (jax.ShapeDtypeStruct((2, m, LANES), BF16), jax.ShapeDtypeStruct((m, LANES), F32)),
                    (pl.BlockSpec((2, tm, LANES), lambda i, j: (0, i, 0)),
                     pl.BlockSpec((tm, LANES), lambda i, j: (i, 0))), tm, w_kw.shape[1], "proj_kw")
    n_plain = (A_WIDTH + B_WIDTH) // tn
    vg = _proj(h, w_vg, (), [], functools.partial(_ep_vg, n_plain=n_plain, n_silu=n_plain),
               jax.ShapeDtypeStruct((m, w_vg.shape[1]), BF16),
               pl.BlockSpec((tm, tn), lambda i, j: (i, j)), tm, tn, "proj_vg")

    topk = min(TOPK_MAX, seq // 4)
    oa = _attn_a(qi, wi, kab, qkr, vg, batch, seq, 256, 512, topk)
    lambda_init = 0.8 - 0.6 * math.exp(-0.3 * layer)
    ob = _attn_b(qkr, vg, b_subln_gain[None, :], lq1[None, :], lk1[None, :], lq2[None, :], lk2[None, :],
                 batch, seq, 512, 4, lambda_init)
    merged = _merge(oa, ob, w_o_a.astype(BF16), w_o_b.astype(BF16), vg, d_model, 1024, 512)
    return _out_proj(merged, w_out.astype(BF16), x2d, 1024, 512)


def kernel(x, positions, norm_gain, w_in, a_q_gain, a_k_gain, b_q_gain, b_k_gain, lambda_q1, lambda_k1,
           lambda_q2, lambda_k2, b_subln_gain, w_o_a, w_o_b, w_out):
    batch, seq, d_model = x.shape
    m = batch * seq
    pos = positions.astype(F32).reshape(m, 1)
    ang_h = pos * ROPE_THETA ** (-jnp.arange(0, HEAD_DIM, 2, dtype=F32) / HEAD_DIM)
    ang_i = pos * ROPE_THETA ** (-jnp.arange(0, IDX_DIM, 2, dtype=F32) / IDX_DIM)
    ch, sh, ci, si = jnp.cos(ang_h), jnp.sin(ang_h), jnp.cos(ang_i), jnp.sin(ang_i)
    tabs = (jnp.concatenate([ch, ch], axis=1), jnp.concatenate([-sh, sh], axis=1),
            jnp.concatenate([ci] * 4, axis=1), jnp.concatenate([-si, -si, si, si], axis=1))
    x2d = x.reshape(m, d_model)
    for layer in range(norm_gain.shape[0]):
        x2d = _layer(x2d, tabs, norm_gain[layer], w_in[layer], a_q_gain[layer], a_k_gain[layer],
                     b_q_gain[layer], b_k_gain[layer], lambda_q1[layer], lambda_k1[layer],
                     lambda_q2[layer], lambda_k2[layer], b_subln_gain[layer], w_o_a[layer], w_o_b[layer],
                     w_out[layer], layer, batch, seq)
    return x2d.reshape(batch, seq, d_model)
```

```python
import functools
import math

import jax
import jax.numpy as jnp
import numpy as np
from jax import lax
from jax.experimental import pallas as pl
from jax.experimental.pallas import tpu as pltpu

F32 = jnp.float32
BF16 = jnp.bfloat16

HEAD_DIM = 128
A_HEADS = 8
A_WIDTH = A_HEADS * HEAD_DIM
IDX_HEADS = 16
IDX_DIM = 64
TOPK_MAX = 256
B_HEADS = 4
B_WIDTH = B_HEADS * 2 * HEAD_DIM
ROPE_THETA = 10000.0
NORM_EPS = 1e-6
SUBLN_EPS = 1e-5

LANES = 128
INT_MIN = -(2 ** 31)
KEY_NEG_INF = INT_MIN + 0x7FFFFF
NEG_BIG = -1e30
M_INIT = -5e29
LOG2E = math.log2(math.e)
SELECT_ROWS = 64
SOFTMAX_ROWS = 32
PROJ_SUB = 256
VMEM_LIMIT_BYTES = 56 * 1024 * 1024


def _nt_dot(a, b):
    return lax.dot_general(a, b, (((1,), (1,)), ((), ())), preferred_element_type=F32)


def _tile_lanes(v, width):
    reps = width // LANES
    return v if reps == 1 else jnp.concatenate([v] * reps, axis=1)


def _params(semantics):
    return pltpu.CompilerParams(dimension_semantics=semantics, vmem_limit_bytes=VMEM_LIMIT_BYTES)


def _norm_kernel(x_ref, g_ref, h_ref):
    x = x_ref[...]
    ms = jnp.mean(x * x, axis=-1, keepdims=True)
    h_ref[...] = (x * lax.rsqrt(ms + NORM_EPS) * g_ref[...]).astype(BF16)


def _norm(x2d, gain, tm):
    m, d = x2d.shape
    return pl.pallas_call(
        _norm_kernel,
        grid=(m // tm,),
        in_specs=[pl.BlockSpec((tm, d), lambda i: (i, 0)), pl.BlockSpec((1, d), lambda i: (0, 0))],
        out_specs=pl.BlockSpec((tm, d), lambda i: (i, 0)),
        out_shape=jax.ShapeDtypeStruct((m, d), BF16),
        compiler_params=_params(("parallel",)),
        name="norm",
    )(x2d, gain)


def _proj_body(h_ref, w_ref, *rest, n_aux, epilogue):
    tn = w_ref.shape[1]

    def sub_dot(t):
        return jnp.dot(h_ref[...], w_ref[:, t * PROJ_SUB:(t + 1) * PROJ_SUB], preferred_element_type=F32)

    epilogue(sub_dot, tn // PROJ_SUB, rest[:n_aux], rest[n_aux:])


def _proj(h, w, aux, aux_specs, epilogue, out_shape, out_specs, tm, tn, name):
    m, k = h.shape
    n = w.shape[1]
    return pl.pallas_call(
        functools.partial(_proj_body, n_aux=len(aux), epilogue=epilogue),
        grid=(m // tm, n // tn),
        in_specs=[pl.BlockSpec((tm, k), lambda i, j: (i, 0)),
                  pl.BlockSpec((k, tn), lambda i, j: (0, j))] + aux_specs,
        out_specs=out_specs,
        out_shape=out_shape,
        compiler_params=_params(("parallel", "arbitrary")),
        name=name,
    )(h, w, *aux)


def _rope128(y, cos_full, sin_signed):
    return y * cos_full + pltpu.roll(y, 64, 1) * sin_signed


def _proj_qk_kernel(h_ref, w_ref, g_ref, c_ref, s_ref, o_ref, acc_ref):
    acc_ref[...] = jnp.dot(h_ref[...], w_ref[...], preferred_element_type=F32)
    same_head = (lax.broadcasted_iota(jnp.int32, (PROJ_SUB, PROJ_SUB), 0) // HEAD_DIM
                 == lax.broadcasted_iota(jnp.int32, (PROJ_SUB, PROJ_SUB), 1) // HEAD_DIM)
    head_mean = jnp.where(same_head, 1.0 / HEAD_DIM, 0.0).astype(BF16)

    def sub_tile(t, carry):
        cs = pl.ds(pl.multiple_of(t * PROJ_SUB, PROJ_SUB), PROJ_SUB)
        acc = acc_ref[:, cs]
        ms = jnp.dot((acc * acc).astype(BF16), head_mean, preferred_element_type=F32)
        y = acc * lax.rsqrt(ms + NORM_EPS) * g_ref[:, cs]
        c = c_ref[...]
        s = s_ref[...]
        for u in range(PROJ_SUB // LANES):
            ls = pl.ds(pl.multiple_of(t * PROJ_SUB + u * LANES, LANES), LANES)
            o_ref[:, ls] = _rope128(y[:, u * LANES:(u + 1) * LANES], c, s).astype(BF16)
        return carry

    lax.fori_loop(0, acc_ref.shape[1] // PROJ_SUB, sub_tile, 0)


def _proj_qk(h, w, gain, cos_full, sin_signed, tm, tn):
    m, k = h.shape
    n = w.shape[1]
    tab_spec = pl.BlockSpec((tm, LANES), lambda i, j: (i, 0))
    return pl.pallas_call(
        _proj_qk_kernel,
        grid=(m // tm, n // tn),
        in_specs=[pl.BlockSpec((tm, k), lambda i, j: (i, 0)),
                  pl.BlockSpec((k, tn), lambda i, j: (0, j)),
                  pl.BlockSpec((1, tn), lambda i, j: (0, j)), tab_spec, tab_spec],
        out_specs=pl.BlockSpec((tm, tn), lambda i, j: (i, j)),
        out_shape=jax.ShapeDtypeStruct((m, n), BF16),
        scratch_shapes=[pltpu.VMEM((tm, tn), F32)],
        compiler_params=_params(("parallel", "arbitrary")),
        name="proj_qk",
    )(h, w, gain, cos_full, sin_signed)


def _ep_qi(sub_dot, n_sub, aux, outs):
    c_ref, s_ref = aux
    (o_ref,) = outs
    c = c_ref[...]
    s = s_ref[...]
    per = PROJ_SUB // LANES
    for t in range(n_sub):
        acc = sub_dot(t)
        for u in range(per):
            o_ref[t * per + u] = _rope128(acc[:, u * LANES:(u + 1) * LANES], c, s).astype(BF16)


def _ep_kw(sub_dot, n_sub, aux, outs):
    c_ref, s_ref = aux
    kab_ref, w_ref = outs
    c = c_ref[...]
    s = s_ref[...]
    acc = sub_dot(0)
    for u in range(2):
        kab_ref[u] = _rope128(acc[:, u * LANES:(u + 1) * LANES], c, s).astype(BF16)
    w_ref[...] = sub_dot(1)[:, :LANES] * (IDX_HEADS ** -0.5 * IDX_DIM ** -0.5)


def _ep_vg(sub_dot, n_sub, aux, outs, *, n_plain, n_silu):
    (o_ref,) = outs
    j = pl.program_id(1)

    def store(fn):
        for t in range(n_sub):
            o_ref[:, t * PROJ_SUB:(t + 1) * PROJ_SUB] = fn(sub_dot(t)).astype(BF16)

    @pl.when(j < n_plain)
    def _():
        store(lambda a: a)

    @pl.when(jnp.logical_and(j >= n_plain, j < n_plain + n_silu))
    def _():
        store(lambda a: a * jax.nn.sigmoid(a))

    @pl.when(j >= n_plain + n_silu)
    def _():
        store(jax.nn.sigmoid)


def _f32_key(v):
    bits = pltpu.bitcast(v, jnp.int32)
    return bits ^ ((bits >> 31) & 0x7FFFFFFF)


def _softmax_rows(s_ref, p_ref, alpha_ref, m_ref, l_ref, adjust):
    tq, tk = s_ref.shape
    for r in range(tq // SOFTMAX_ROWS):
        rs = slice(r * SOFTMAX_ROWS, (r + 1) * SOFTMAX_ROWS)
        lanes = [slice(t * LANES, (t + 1) * LANES) for t in range(tk // LANES)]
        s = [adjust(s_ref[rs, ls], rs, ls) for ls in lanes]
        m_prev = m_ref[rs, :]
        m_new = jnp.maximum(m_prev, jnp.max(functools.reduce(jnp.maximum, s), axis=1, keepdims=True))
        alpha = jnp.exp2(m_prev - m_new)
        p = [jnp.exp2(x - m_new) for x in s]
        l_ref[rs, :] = alpha * l_ref[rs, :] + jnp.sum(functools.reduce(jnp.add, p), axis=1, keepdims=True)
        for ls, x in zip(lanes, p):
            p_ref[rs, ls] = x.astype(BF16)
        alpha_ref[rs, :] = alpha
        m_ref[rs, :] = m_new


def _a_kernel(qt_ref, kt_ref, qi_ref, wi_ref, kab_ref, qa_ref, ka_ref, va_ref, ga_ref, o_ref,
              sc_ref, gmax_ref, tau_ref, lo_ref, hi_ref, cand_ref, done_ref, ext_ref, cnt_ref, bias_ref,
              s_ref, p_ref, alpha_ref, m_ref, l_ref, acc_ref, *, tq, tk, topk, rows):
    step = pl.program_id(1)
    iq = qt_ref[step]
    kj = kt_ref[step]
    last = ((iq + 1) * tq - 1) // tk

    @pl.when(kj == 0)
    def _index_and_select():
        w = wi_ref[...]
        row = iq * tq + lax.broadcasted_iota(jnp.int32, (tq, tk), 0)
        col0 = lax.broadcasted_iota(jnp.int32, (tq, tk), 1)
        gmax_ref[...] = jnp.full(gmax_ref.shape, -jnp.inf, F32)

        def score_tile(j, carry):
            start = pl.multiple_of(j * tk, tk)
            k_even = kab_ref[0, pl.ds(start, tk), :]
            k_odd = kab_ref[1, pl.ds(start, tk), :]
            acc = jnp.zeros((tq, tk), F32)
            for p in range(IDX_HEADS // 2):
                q_p = qi_ref[p]
                s_even = jnp.maximum(_nt_dot(q_p, k_even), 0.0)
                s_odd = jnp.maximum(_nt_dot(q_p, k_odd), 0.0)
                acc = acc + w[:, 2 * p:2 * p + 1] * s_even + w[:, 2 * p + 1:2 * p + 2] * s_odd
            causal = col0 + j * tk <= row
            sc_ref[j] = jnp.where(causal, _f32_key(acc), INT_MIN)
            accm = jnp.where(causal, acc, -jnp.inf)
            half = tk // 2
            for g in range(2):
                gm = gmax_ref[g]
                for t in range(half // LANES):
                    lo = g * half + t * LANES
                    gm = jnp.maximum(gm, accm[:, lo:lo + LANES])
                gmax_ref[g] = gm
            return carry

        lax.fori_loop(0, last + 1, score_tile, 0)

        g0 = gmax_ref[0]
        g1 = gmax_ref[1]
        row_max = jnp.max(jnp.maximum(g0, g1), axis=1, keepdims=True)
        row_low = jnp.min(jnp.minimum(g0, g1), axis=1, keepdims=True)
        lo0 = jnp.broadcast_to(_f32_key(row_low), (tq, LANES))
        hi0 = jnp.broadcast_to(_f32_key(row_max), (tq, LANES)) + 1
        lo_ref[...] = lo0
        hi_ref[...] = hi0
        done_ref[...] = jnp.where(hi0 - lo0 == 1, 1.0, 0.0)
        ext_ref[...] = jnp.zeros((tq, LANES), F32)

        def row_pass(ref, body, init, finish):
            for rc in range(tq // rows):
                rs = pl.ds(rc * rows, rows)
                bound = ref[rs, :]

                def tile_step(j, carry):
                    keys = sc_ref[j, rs, :]
                    for t in range(tk // LANES):
                        carry = body(carry, keys[:, t * LANES:(t + 1) * LANES], bound)
                    return carry

                cnt_ref[rs, :] = finish(lax.fori_loop(0, last + 1, tile_step, init))

        def search_cond(c):
            n, stop = c
            return jnp.logical_and(n < 2 * 32 + 2, stop == 0)

        def search_step(c):
            n, _ = c
            stop = (jnp.min(done_ref[...]) > 0.0).astype(jnp.int32)
            lo = lo_ref[...]
            hi = hi_ref[...]
            cand_ref[...] = (lo >> 1) + (hi >> 1) + (lo & hi & 1)
            row_pass(cand_ref, lambda acc, keys, cand: acc + jnp.where(keys >= cand, 1.0, 0.0),
                     jnp.zeros((rows, LANES), F32), lambda acc: acc)
            cnt = jnp.sum(cnt_ref[...], axis=1, keepdims=True)
            cand = cand_ref[...]
            active = done_ref[...] == 0.0
            enough = cnt >= float(topk)
            lo = jnp.where(jnp.logical_and(active, enough), cand, lo)
            hi = jnp.where(jnp.logical_and(active, jnp.logical_not(enough)), cand, hi)
            lo_ref[...] = lo
            hi_ref[...] = hi
            extract = jnp.logical_and(active, cnt == float(topk - 1))
            ext_ref[...] = jnp.where(extract, 1.0, ext_ref[...])
            finished = jnp.logical_or(jnp.logical_or(cnt == float(topk), extract), hi - lo == 1)
            done_ref[...] = jnp.where(jnp.logical_and(active, finished), 1.0, done_ref[...])
            return n + 1, stop

        lax.while_loop(search_cond, search_step, (jnp.int32(0), jnp.int32(0)))

        def key_to_f32(keys):
            return pltpu.bitcast(keys ^ ((keys >> 31) & 0x7FFFFFFF), F32)

        row_pass(hi_ref, lambda acc, keys, hi: jnp.maximum(acc, jnp.where(keys < hi, keys, KEY_NEG_INF)),
                 jnp.full((rows, LANES), KEY_NEG_INF, jnp.int32), key_to_f32)
        below = jnp.broadcast_to(_f32_key(jnp.max(cnt_ref[...], axis=1, keepdims=True)), (tq, LANES))
        thr = jnp.where(ext_ref[...] > 0.0, below, lo_ref[...])
        tau_ref[...] = jnp.maximum(thr, INT_MIN + 1)

        m_ref[...] = jnp.full(m_ref.shape, M_INIT, F32)
        l_ref[...] = jnp.zeros(l_ref.shape, F32)
        acc_ref[...] = jnp.zeros(acc_ref.shape, F32)

    tau = tau_ref[...]
    for t in range(tk // LANES):
        ls = slice(t * LANES, (t + 1) * LANES)
        bias_ref[:, ls] = jnp.where(sc_ref[kj, :, ls] >= tau, 0.0, NEG_BIG)

    def add_bias(x, rs, ls):
        return x + bias_ref[rs, ls]

    heads = [slice(h * HEAD_DIM, (h + 1) * HEAD_DIM) for h in range(A_HEADS)]
    for h, hs in enumerate(heads):
        s_ref[h] = _nt_dot(qa_ref[:, hs], ka_ref[:, hs])
    for h in range(A_HEADS):
        _softmax_rows(s_ref.at[h], p_ref.at[h], alpha_ref.at[h], m_ref.at[h], l_ref.at[h], add_bias)
    for h, hs in enumerate(heads):
        acc_ref[h] = alpha_ref[h] * acc_ref[h] + jnp.dot(p_ref[h], va_ref[:, hs],
                                                         preferred_element_type=F32)

    @pl.when(kj == last)
    def _finish():
        for h in range(A_HEADS):
            hs = slice(h * HEAD_DIM, (h + 1) * HEAD_DIM)
            o = acc_ref[h] / l_ref[h]
            o_ref[:, hs] = (o * ga_ref[:, hs].astype(F32)).astype(BF16)


def _causal_steps(nq, tq, tk):
    qt, kt = [], []
    for i in range(nq):
        for j in range(((i + 1) * tq - 1) // tk + 1):
            qt.append(i)
            kt.append(j)
    return jnp.asarray(qt, jnp.int32), jnp.asarray(kt, jnp.int32)


def _attn_a(qi, wi, kab, qkr, vg, batch, seq, tq, tk, topk):
    m = batch * seq
    nq = seq // tq
    nk = seq // tk
    qt, kt = _causal_steps(nq, tq, tk)

    def qrow(b, s, qt_ref, kt_ref):
        return b * nq + qt_ref[s]

    def krow(b, s, qt_ref, kt_ref):
        return b * nk + kt_ref[s]

    grid_spec = pltpu.PrefetchScalarGridSpec(
        num_scalar_prefetch=2,
        grid=(batch, int(qt.shape[0])),
        in_specs=[
            pl.BlockSpec((IDX_HEADS // 2, tq, LANES), lambda b, s, q, k: (0, qrow(b, s, q, k), 0)),
            pl.BlockSpec((tq, LANES), lambda b, s, q, k: (qrow(b, s, q, k), 0)),
            pl.BlockSpec((2, seq, LANES), lambda b, s, q, k: (0, b, 0)),
            pl.BlockSpec((tq, A_WIDTH), lambda b, s, q, k: (qrow(b, s, q, k), 0)),
            pl.BlockSpec((tk, A_WIDTH), lambda b, s, q, k: (krow(b, s, q, k), 1)),
            pl.BlockSpec((tk, A_WIDTH), lambda b, s, q, k: (krow(b, s, q, k), 0)),
            pl.BlockSpec((tq, A_WIDTH), lambda b, s, q, k: (qrow(b, s, q, k), 2)),
        ],
        out_specs=pl.BlockSpec((tq, A_WIDTH), lambda b, s, q, k: (qrow(b, s, q, k), 0)),
        scratch_shapes=[
            pltpu.VMEM((nk, tq, tk), jnp.int32),
            pltpu.VMEM((2, tq, LANES), F32),
            pltpu.VMEM((tq, LANES), jnp.int32),
            pltpu.VMEM((tq, LANES), jnp.int32),
            pltpu.VMEM((tq, LANES), jnp.int32),
            pltpu.VMEM((tq, LANES), jnp.int32),
            pltpu.VMEM((tq, LANES), F32),
            pltpu.VMEM((tq, LANES), F32),
            pltpu.VMEM((tq, LANES), F32),
            pltpu.VMEM((tq, tk), F32),
            pltpu.VMEM((A_HEADS, tq, tk), F32),
            pltpu.VMEM((A_HEADS, tq, tk), BF16),
            pltpu.VMEM((A_HEADS, tq, LANES), F32),
            pltpu.VMEM((A_HEADS, tq, LANES), F32),
            pltpu.VMEM((A_HEADS, tq, LANES), F32),
            pltpu.VMEM((A_HEADS, tq, HEAD_DIM), F32),
        ],
    )
    return pl.pallas_call(
        functools.partial(_a_kernel, tq=tq, tk=tk, topk=topk, rows=SELECT_ROWS),
        grid_spec=grid_spec,
        out_shape=jax.ShapeDtypeStruct((m, A_WIDTH), BF16),
        compiler_params=_params(("parallel", "arbitrary")),
        name="attn_a",
    )(qt, kt, qi, wi, kab, qkr, qkr, vg, vg)


def _b_kernel(qt_ref, kt_ref, q_ref, k_ref, v_ref, sg_ref, gain_ref, lq1_ref, lk1_ref, lq2_ref, lk2_ref,
              o_ref, s_ref, p_ref, alpha_ref, m_ref, l_ref, acc_ref, *, tq, heads, lambda_init):
    step = pl.program_id(2)
    iq = qt_ref[step]
    kj = kt_ref[step]
    hw = 2 * HEAD_DIM
    chains = [(hh, c) for hh in range(heads) for c in range(2)]

    @pl.when(kj == 0)
    def _init():
        m_ref[...] = jnp.full(m_ref.shape, M_INIT, F32)
        l_ref[...] = jnp.zeros(l_ref.shape, F32)
        acc_ref[...] = jnp.zeros(acc_ref.shape, F32)

    def causal_mask(x, rs, ls):
        row = rs.start + lax.broadcasted_iota(jnp.int32, x.shape, 0)
        col = ls.start + lax.broadcasted_iota(jnp.int32, x.shape, 1)
        return jnp.where(col <= row, x, NEG_BIG)

    def attend(diagonal):
        adjust = causal_mask if diagonal else (lambda x, rs, ls: x)
        for i, (hh, c) in enumerate(chains):
            cs = slice(hh * hw + c * HEAD_DIM, hh * hw + (c + 1) * HEAD_DIM)
            s_ref[i] = _nt_dot(q_ref[:, cs], k_ref[:, cs])
        for i in range(len(chains)):
            _softmax_rows(s_ref.at[i], p_ref.at[i], alpha_ref.at[i], m_ref.at[i], l_ref.at[i], adjust)
        for i, (hh, c) in enumerate(chains):
            pv = jnp.dot(p_ref[i], v_ref[:, hh * hw:(hh + 1) * hw], preferred_element_type=F32)
            alpha = alpha_ref[i]
            for t in range(hw // LANES):
                ls = slice(t * LANES, (t + 1) * LANES)
                acc_ref[i, :, ls] = alpha * acc_ref[i, :, ls] + pv[:, ls]

    @pl.when(kj < iq)
    def _below_diagonal():
        attend(False)

    @pl.when(kj == iq)
    def _diagonal_and_finish():
        attend(True)
        lam = (jnp.exp(jnp.sum(lq1_ref[...] * lk1_ref[...], axis=1, keepdims=True))
               - jnp.exp(jnp.sum(lq2_ref[...] * lk2_ref[...], axis=1, keepdims=True))
               + lambda_init)
        for hh in range(heads):
            o0 = acc_ref[2 * hh] / _tile_lanes(l_ref[2 * hh], hw)
            o1 = acc_ref[2 * hh + 1] / _tile_lanes(l_ref[2 * hh + 1], hw)
            o = o0 - lam * o1
            ms = jnp.mean(o * o, axis=-1, keepdims=True)
            y = o * lax.rsqrt(ms + SUBLN_EPS) * gain_ref[...] * (1.0 - lambda_init)
            hs = slice(hh * hw, (hh + 1) * hw)
            o_ref[:, hs] = (y * sg_ref[:, hs].astype(F32)).astype(BF16)


def _attn_b(qkr, vg, gain, lq1, lk1, lq2, lk2, batch, seq, tq, heads, lambda_init):
    m = batch * seq
    nq = seq // tq
    bw = heads * 2 * HEAD_DIM
    groups = B_HEADS // heads
    chains = 2 * heads
    qt, kt = _causal_steps(nq, tq, tq)

    def qrow(b, s, qt_ref):
        return b * nq + qt_ref[s]

    vec = pl.BlockSpec((1, HEAD_DIM), lambda b, h, s, q, k: (0, 0))
    grid_spec = pltpu.PrefetchScalarGridSpec(
        num_scalar_prefetch=2,
        grid=(batch, groups, int(qt.shape[0])),
        in_specs=[
            pl.BlockSpec((tq, bw), lambda b, h, s, q, k: (qrow(b, s, q), 2 * groups + h)),
            pl.BlockSpec((tq, bw), lambda b, h, s, q, k: (qrow(b, s, k), 3 * groups + h)),
            pl.BlockSpec((tq, bw), lambda b, h, s, q, k: (qrow(b, s, k), groups + h)),
            pl.BlockSpec((tq, bw), lambda b, h, s, q, k: (qrow(b, s, q), 3 * groups + h)),
            pl.BlockSpec((1, 2 * HEAD_DIM), lambda b, h, s, q, k: (0, 0)),
            vec, vec, vec, vec,
        ],
        out_specs=pl.BlockSpec((tq, bw), lambda b, h, s, q, k: (qrow(b, s, q), h)),
        scratch_shapes=[
            pltpu.VMEM((chains, tq, tq), F32),
            pltpu.VMEM((chains, tq, tq), BF16),
            pltpu.VMEM((chains, tq, LANES), F32),
            pltpu.VMEM((chains, tq, LANES), F32),
            pltpu.VMEM((chains, tq, LANES), F32),
            pltpu.VMEM((chains, tq, 2 * HEAD_DIM), F32),
        ],
    )
    return pl.pallas_call(
        functools.partial(_b_kernel, tq=tq, heads=heads, lambda_init=lambda_init),
        grid_spec=grid_spec,
        out_shape=jax.ShapeDtypeStruct((m, B_WIDTH), BF16),
        compiler_params=_params(("parallel", "parallel", "arbitrary")),
        name="attn_b",
    )(qt, kt, qkr, qkr, vg, vg, gain, lq1, lk1, lq2, lk2)


def _merge_kernel(oa_ref, ob_ref, wa_ref, wb_ref, ma_ref, mb_ref, o_ref):
    ya = jnp.dot(oa_ref[...], wa_ref[...], preferred_element_type=F32)
    yb = jnp.dot(ob_ref[...], wb_ref[...], preferred_element_type=F32)
    o_ref[...] = (ma_ref[...].astype(F32) * ya + mb_ref[...].astype(F32) * yb).astype(BF16)


def _merge(oa, ob, w_a, w_b, vg, d_model, tm, tn):
    m = oa.shape[0]
    gate_a0 = (A_WIDTH + B_WIDTH) * 2 // tn
    gate_b0 = gate_a0 + d_model // tn
    return pl.pallas_call(
        _merge_kernel,
        grid=(m // tm, d_model // tn),
        in_specs=[
            pl.BlockSpec((tm, A_WIDTH), lambda i, j: (i, 0)),
            pl.BlockSpec((tm, B_WIDTH), lambda i, j: (i, 0)),
            pl.BlockSpec((A_WIDTH, tn), lambda i, j: (0, j)),
            pl.BlockSpec((B_WIDTH, tn), lambda i, j: (0, j)),
            pl.BlockSpec((tm, tn), lambda i, j: (i, gate_a0 + j)),
            pl.BlockSpec((tm, tn), lambda i, j: (i, gate_b0 + j)),
        ],
        out_specs=pl.BlockSpec((tm, tn), lambda i, j: (i, j)),
        out_shape=jax.ShapeDtypeStruct((m, d_model), BF16),
        compiler_params=_params(("parallel", "arbitrary")),
        name="merge",
    )(oa, ob, w_a, w_b, vg, vg)


def _out_kernel(mg_ref, w_ref, x_ref, o_ref):
    o_ref[...] = x_ref[...] + jnp.dot(mg_ref[...], w_ref[...], preferred_element_type=F32)


def _out_proj(merged, w_out, x2d, tm, tn):
    m, d = x2d.shape
    return pl.pallas_call(
        _out_kernel,
        grid=(m // tm, d // tn),
        in_specs=[
            pl.BlockSpec((tm, d), lambda i, j: (i, 0)),
            pl.BlockSpec((d, tn), lambda i, j: (0, j)),
            pl.BlockSpec((tm, tn), lambda i, j: (i, j)),
        ],
        out_specs=pl.BlockSpec((tm, tn), lambda i, j: (i, j)),
        out_shape=jax.ShapeDtypeStruct((m, d), F32),
        compiler_params=_params(("parallel", "arbitrary")),
        name="out_proj",
    )(merged, w_out, x2d)


def _qi_column_order():
    lane = np.arange(LANES)
    quarter, r = lane // 32, lane % 32
    dim = r + 32 * (quarter // 2)
    q_cols = np.concatenate([(2 * p + quarter % 2) * IDX_DIM + dim for p in range(IDX_HEADS // 2)])
    return q_cols, dim, (quarter % 2 == 0)


def _layer(x2d, tabs, norm_gain, w_in, a_q_gain, a_k_gain, b_q_gain, b_k_gain,
           lq1, lk1, lq2, lk2, b_subln_gain, w_o_a, w_o_b, w_out, layer, batch, seq):
    d_model = x2d.shape[1]
    cos_h, sin_h, cos_i, sin_i = tabs
    sizes = (A_WIDTH,) * 4 + (IDX_HEADS * IDX_DIM, IDX_DIM, IDX_HEADS) + (B_WIDTH,) * 4 + (d_model,) * 2
    off = np.concatenate([[0], np.cumsum(sizes)])
    seg = lambda i: w_in[:, off[i]:off[i + 1]]
    (w_qa, w_ka, w_va, w_ga, w_qi, w_ki, w_wi, w_qb, w_kb, w_vb, w_gb, w_ma, w_mb) = [seg(i) for i in range(13)]

    scale = HEAD_DIM ** -0.5 * LOG2E
    w_qk = jnp.concatenate([w_qa, w_ka, w_qb, w_kb], axis=1).astype(BF16)
    gain_qk = jnp.concatenate([jnp.tile(a_q_gain * scale, A_HEADS), jnp.tile(a_k_gain, A_HEADS),
                               jnp.tile(b_q_gain * scale, 2 * B_HEADS), jnp.tile(b_k_gain, 2 * B_HEADS)])[None, :]
    q_cols, k_dim, even = _qi_column_order()
    w_qi_p = w_qi[:, q_cols].astype(BF16)
    w_k_rep = w_ki[:, k_dim]
    zeros = jnp.zeros_like(w_k_rep)
    w_kw = jnp.concatenate([
        jnp.where(even[None, :], w_k_rep, zeros), jnp.where(even[None, :], zeros, w_k_rep),
        jnp.pad(w_wi, ((0, 0), (0, 2 * LANES - IDX_HEADS)))], axis=1).astype(BF16)
    w_vg = jnp.concatenate([w_va, w_vb, w_ga, w_gb, w_ma, w_mb], axis=1).astype(BF16)

    m = x2d.shape[0]
    tm, tn = 1024, 1024
    h = _norm(x2d, norm_gain[None, :], 512)

    tab_spec = pl.BlockSpec((tm, LANES), lambda i, j: (i, 0))
    qkr = _proj_qk(h, w_qk, gain_qk, cos_h, sin_h, tm, tn)
    qi = _proj(h, w_qi_p, (cos_i, sin_i), [tab_spec, tab_spec], _ep_qi,
               jax.ShapeDtypeStruct((IDX_HEADS // 2, m, LANES), BF16),
               pl.BlockSpec((tn // LANES, tm, LANES), lambda i, j: (j, i, 0)), tm, tn, "proj_qi")
    kab, wi = _proj(h, w_kw, (cos_i, sin_i), [tab_spec, tab_spec], _ep_kw,
                    (jax.ShapeDtypeStruct((2, m, LANES), BF16), jax.ShapeDtypeStruct((m, LANES), F32)),
                    (pl.BlockSpec((2, tm, LANES), lambda i, j: (0, i, 0)),
                     pl.BlockSpec((tm, LANES), lambda i, j: (i, 0))), tm, w_kw.shape[1], "proj_kw")
    n_plain = (A_WIDTH + B_WIDTH) // tn
    vg = _proj(h, w_vg, (), [], functools.partial(_ep_vg, n_plain=n_plain, n_silu=n_plain),
               jax.ShapeDtypeStruct((m, w_vg.shape[1]), BF16),
               pl.BlockSpec((tm, tn), lambda i, j: (i, j)), tm, tn, "proj_vg")

    topk = min(TOPK_MAX, seq // 4)
    oa = _attn_a(qi, wi, kab, qkr, vg, batch, seq, 256, 512, topk)
    lambda_init = 0.8 - 0.6 * math.exp(-0.3 * layer)
    ob = _attn_b(qkr, vg, b_subln_gain[None, :], lq1[None, :], lk1[None, :], lq2[None, :], lk2[None, :],
                 batch, seq, 512, 4, lambda_init)
    merged = _merge(oa, ob, w_o_a.astype(BF16), w_o_b.astype(BF16), vg, d_model, 1024, 512)
    return _out_proj(merged, w_out.astype(BF16), x2d, 1024, 512)


def kernel(x, positions, norm_gain, w_in, a_q_gain, a_k_gain, b_q_gain, b_k_gain, lambda_q1, lambda_k1,
           lambda_q2, lambda_k2, b_subln_gain, w_o_a, w_o_b, w_out):
    batch, seq, d_model = x.shape
    m = batch * seq
    pos = positions.astype(F32).reshape(m, 1)
    ang_h = pos * ROPE_THETA ** (-jnp.arange(0, HEAD_DIM, 2, dtype=F32) / HEAD_DIM)
    ang_i = pos * ROPE_THETA ** (-jnp.arange(0, IDX_DIM, 2, dtype=F32) / IDX_DIM)
    ch, sh, ci, si = jnp.cos(ang_h), jnp.sin(ang_h), jnp.cos(ang_i), jnp.sin(ang_i)
    tabs = (jnp.concatenate([ch, ch], axis=1), jnp.concatenate([-sh, sh], axis=1),
            jnp.concatenate([ci] * 4, axis=1), jnp.concatenate([-si, -si, si, si], axis=1))
    x2d = x.reshape(m, d_model)
    for layer in range(norm_gain.shape[0]):
        x2d = _layer(x2d, tabs, norm_gain[layer], w_in[layer], a_q_gain[layer], a_k_gain[layer],
                     b_q_gain[layer], b_k_gain[layer], lambda_q1[layer], lambda_k1[layer],
                     lambda_q2[layer], lambda_k2[layer], b_subln_gain[layer], w_o_a[layer], w_o_b[layer],
                     w_out[layer], layer, batch, seq)
    return x2d.reshape(batch, seq, d_model)
```

```python
import functools
import math

import jax
import jax.numpy as jnp
import numpy as np
from jax import lax
from jax.experimental import pallas as pl
from jax.experimental.pallas import tpu as pltpu

F32 = jnp.float32
BF16 = jnp.bfloat16

HEAD_DIM = 128
A_HEADS = 8
A_WIDTH = A_HEADS * HEAD_DIM
IDX_HEADS = 16
IDX_DIM = 64
TOPK_MAX = 256
B_HEADS = 4
B_WIDTH = B_HEADS * 2 * HEAD_DIM
ROPE_THETA = 10000.0
NORM_EPS = 1e-6
SUBLN_EPS = 1e-5

LANES = 128
INT_MIN = -(2 ** 31)
KEY_NEG_INF = INT_MIN + 0x7FFFFF
NEG_BIG = -1e30
M_INIT = -5e29
LOG2E = math.log2(math.e)
MAX_LOGIT_BOUND = 60.0
SELECT_ROWS = 64
SOFTMAX_ROWS = 32
PROJ_SUB = 256
VMEM_LIMIT_BYTES = 56 * 1024 * 1024


def _nt_dot(a, b):
    return lax.dot_general(a, b, (((1,), (1,)), ((), ())), preferred_element_type=F32)


def _tile_lanes(v, width):
    reps = width // LANES
    return v if reps == 1 else jnp.concatenate([v] * reps, axis=1)


def _params(semantics):
    return pltpu.CompilerParams(dimension_semantics=semantics, vmem_limit_bytes=VMEM_LIMIT_BYTES)


def _norm_kernel(x_ref, g_ref, h_ref):
    x = x_ref[...]
    ms = jnp.mean(x * x, axis=-1, keepdims=True)
    h_ref[...] = (x * lax.rsqrt(ms + NORM_EPS) * g_ref[...]).astype(BF16)


def _norm(x2d, gain, tm):
    m, d = x2d.shape
    return pl.pallas_call(
        _norm_kernel,
        grid=(m // tm,),
        in_specs=[pl.BlockSpec((tm, d), lambda i: (i, 0)), pl.BlockSpec((1, d), lambda i: (0, 0))],
        out_specs=pl.BlockSpec((tm, d), lambda i: (i, 0)),
        out_shape=jax.ShapeDtypeStruct((m, d), BF16),
        compiler_params=_params(("parallel",)),
        name="norm",
    )(x2d, gain)


def _proj_body(h_ref, w_ref, *rest, n_aux, epilogue):
    tn = w_ref.shape[1]

    def sub_dot(t):
        return jnp.dot(h_ref[...], w_ref[:, t * PROJ_SUB:(t + 1) * PROJ_SUB], preferred_element_type=F32)

    epilogue(sub_dot, tn // PROJ_SUB, rest[:n_aux], rest[n_aux:])


def _proj(h, w, aux, aux_specs, epilogue, out_shape, out_specs, tm, tn, name):
    m, k = h.shape
    n = w.shape[1]
    return pl.pallas_call(
        functools.partial(_proj_body, n_aux=len(aux), epilogue=epilogue),
        grid=(m // tm, n // tn),
        in_specs=[pl.BlockSpec((tm, k), lambda i, j: (i, 0)),
                  pl.BlockSpec((k, tn), lambda i, j: (0, j))] + aux_specs,
        out_specs=out_specs,
        out_shape=out_shape,
        compiler_params=_params(("parallel", "arbitrary")),
        name=name,
    )(h, w, *aux)


def _rope128(y, cos_full, sin_signed):
    return y * cos_full + pltpu.roll(y, 64, 1) * sin_signed


def _proj_qk_kernel(h_ref, w_ref, g_ref, c_ref, s_ref, o_ref, acc_ref):
    acc_ref[...] = jnp.dot(h_ref[...], w_ref[...], preferred_element_type=F32)
    same_head = (lax.broadcasted_iota(jnp.int32, (PROJ_SUB, PROJ_SUB), 0) // HEAD_DIM
                 == lax.broadcasted_iota(jnp.int32, (PROJ_SUB, PROJ_SUB), 1) // HEAD_DIM)
    head_mean = jnp.where(same_head, 1.0 / HEAD_DIM, 0.0).astype(BF16)

    def sub_tile(t, carry):
        cs = pl.ds(pl.multiple_of(t * PROJ_SUB, PROJ_SUB), PROJ_SUB)
        acc = acc_ref[:, cs]
        ms = jnp.dot((acc * acc).astype(BF16), head_mean, preferred_element_type=F32)
        y = acc * lax.rsqrt(ms + NORM_EPS) * g_ref[:, cs]
        c = c_ref[...]
        s = s_ref[...]
        for u in range(PROJ_SUB // LANES):
            ls = pl.ds(pl.multiple_of(t * PROJ_SUB + u * LANES, LANES), LANES)
            o_ref[:, ls] = _rope128(y[:, u * LANES:(u + 1) * LANES], c, s).astype(BF16)
        return carry

    lax.fori_loop(0, acc_ref.shape[1] // PROJ_SUB, sub_tile, 0)


def _proj_qk(h, w, gain, cos_full, sin_signed, tm, tn):
    m, k = h.shape
    n = w.shape[1]
    tab_spec = pl.BlockSpec((tm, LANES), lambda i, j: (i, 0))
    return pl.pallas_call(
        _proj_qk_kernel,
        grid=(m // tm, n // tn),
        in_specs=[pl.BlockSpec((tm, k), lambda i, j: (i, 0)),
                  pl.BlockSpec((k, tn), lambda i, j: (0, j)),
                  pl.BlockSpec((1, tn), lambda i, j: (0, j)), tab_spec, tab_spec],
        out_specs=pl.BlockSpec((tm, tn), lambda i, j: (i, j)),
        out_shape=jax.ShapeDtypeStruct((m, n), BF16),
        scratch_shapes=[pltpu.VMEM((tm, tn), F32)],
        compiler_params=_params(("parallel", "arbitrary")),
        name="proj_qk",
    )(h, w, gain, cos_full, sin_signed)


def _ep_qi(sub_dot, n_sub, aux, outs):
    c_ref, s_ref = aux
    (o_ref,) = outs
    c = c_ref[...]
    s = s_ref[...]
    per = PROJ_SUB // LANES
    for t in range(n_sub):
        acc = sub_dot(t)
        for u in range(per):
            o_ref[t * per + u] = _rope128(acc[:, u * LANES:(u + 1) * LANES], c, s).astype(BF16)


def _ep_kw(sub_dot, n_sub, aux, outs):
    c_ref, s_ref = aux
    kab_ref, w_ref = outs
    c = c_ref[...]
    s = s_ref[...]
    acc = sub_dot(0)
    for u in range(2):
        kab_ref[u] = _rope128(acc[:, u * LANES:(u + 1) * LANES], c, s).astype(BF16)
    w_ref[...] = sub_dot(1)[:, :LANES] * (IDX_HEADS ** -0.5 * IDX_DIM ** -0.5)


def _ep_vg(sub_dot, n_sub, aux, outs, *, n_plain, n_silu):
    (o_ref,) = outs
    j = pl.program_id(1)

    def store(fn):
        for t in range(n_sub):
            o_ref[:, t * PROJ_SUB:(t + 1) * PROJ_SUB] = fn(sub_dot(t)).astype(BF16)

    @pl.when(j < n_plain)
    def _():
        store(lambda a: a)

    @pl.when(jnp.logical_and(j >= n_plain, j < n_plain + n_silu))
    def _():
        store(lambda a: a * jax.nn.sigmoid(a))

    @pl.when(j >= n_plain + n_silu)
    def _():
        store(jax.nn.sigmoid)


def _f32_key(v):
    bits = pltpu.bitcast(v, jnp.int32)
    return bits ^ ((bits >> 31) & 0x7FFFFFFF)


def _softmax_rows(s_ref, p_ref, alpha_ref, m_ref, l_ref, adjust):
    tq, tk = s_ref.shape
    for r in range(tq // SOFTMAX_ROWS):
        rs = slice(r * SOFTMAX_ROWS, (r + 1) * SOFTMAX_ROWS)
        lanes = [slice(t * LANES, (t + 1) * LANES) for t in range(tk // LANES)]
        s = [adjust(s_ref[rs, ls], rs, ls) for ls in lanes]
        m_prev = m_ref[rs, :]
        m_new = jnp.maximum(m_prev, jnp.max(functools.reduce(jnp.maximum, s), axis=1, keepdims=True))
        alpha = jnp.exp2(m_prev - m_new)
        p = [jnp.exp2(x - m_new) for x in s]
        l_ref[rs, :] = alpha * l_ref[rs, :] + jnp.sum(functools.reduce(jnp.add, p), axis=1, keepdims=True)
        for ls, x in zip(lanes, p):
            p_ref[rs, ls] = x.astype(BF16)
        alpha_ref[rs, :] = alpha
        m_ref[rs, :] = m_new


def _softmax_rows_bounded(s_ref, p_ref, l_ref, adjust, bound):
    tq, tk = s_ref.shape
    for r in range(tq // SOFTMAX_ROWS):
        rs = slice(r * SOFTMAX_ROWS, (r + 1) * SOFTMAX_ROWS)
        lanes = [slice(t * LANES, (t + 1) * LANES) for t in range(tk // LANES)]
        p = [jnp.exp2(adjust(s_ref[rs, ls], rs, ls) - bound) for ls in lanes]
        l_ref[rs, :] = l_ref[rs, :] + jnp.sum(functools.reduce(jnp.add, p), axis=1, keepdims=True)
        for ls, x in zip(lanes, p):
            p_ref[rs, ls] = x.astype(BF16)


def _a_kernel(qt_ref, kt_ref, qi_ref, wi_ref, kab_ref, qa_ref, ka_ref, va_ref, ga_ref, bound_ref, o_ref,
              sc_ref, gmax_ref, tau_ref, lo_ref, hi_ref, cand_ref, done_ref, ext_ref, cnt_ref, bias_ref,
              s_ref, p_ref, alpha_ref, m_ref, l_ref, acc_ref, *, tq, tk, topk, rows, bounded):
    step = pl.program_id(1)
    iq = qt_ref[step]
    kj = kt_ref[step]
    last = ((iq + 1) * tq - 1) // tk

    @pl.when(kj == 0)
    def _index_and_select():
        w = wi_ref[...]
        row = iq * tq + lax.broadcasted_iota(jnp.int32, (tq, tk), 0)
        col0 = lax.broadcasted_iota(jnp.int32, (tq, tk), 1)
        gmax_ref[...] = jnp.full(gmax_ref.shape, -jnp.inf, F32)

        def score_tile(j, carry):
            start = pl.multiple_of(j * tk, tk)
            k_even = kab_ref[0, pl.ds(start, tk), :]
            k_odd = kab_ref[1, pl.ds(start, tk), :]
            acc = jnp.zeros((tq, tk), F32)
            for p in range(IDX_HEADS // 2):
                q_p = qi_ref[p]
                s_even = jnp.maximum(_nt_dot(q_p, k_even), 0.0)
                s_odd = jnp.maximum(_nt_dot(q_p, k_odd), 0.0)
                acc = acc + w[:, 2 * p:2 * p + 1] * s_even + w[:, 2 * p + 1:2 * p + 2] * s_odd
            causal = col0 + j * tk <= row
            sc_ref[j] = jnp.where(causal, _f32_key(acc), INT_MIN)
            accm = jnp.where(causal, acc, -jnp.inf)
            half = tk // 2
            for g in range(2):
                gm = gmax_ref[g]
                for t in range(half // LANES):
                    lo = g * half + t * LANES
                    gm = jnp.maximum(gm, accm[:, lo:lo + LANES])
                gmax_ref[g] = gm
            return carry

        lax.fori_loop(0, last + 1, score_tile, 0)

        g0 = gmax_ref[0]
        g1 = gmax_ref[1]
        row_max = jnp.max(jnp.maximum(g0, g1), axis=1, keepdims=True)
        row_low = jnp.min(jnp.minimum(g0, g1), axis=1, keepdims=True)
        lo0 = jnp.broadcast_to(_f32_key(row_low), (tq, LANES))
        hi0 = jnp.broadcast_to(_f32_key(row_max), (tq, LANES)) + 1
        lo_ref[...] = lo0
        hi_ref[...] = hi0
        done_ref[...] = jnp.where(hi0 - lo0 == 1, 1.0, 0.0)
        ext_ref[...] = jnp.zeros((tq, LANES), F32)

        def row_pass(ref, body, init, finish):
            for rc in range(tq // rows):
                rs = pl.ds(rc * rows, rows)
                bound = ref[rs, :]

                def tile_step(j, carry):
                    keys = sc_ref[j, rs, :]
                    for t in range(tk // LANES):
                        carry = body(carry, keys[:, t * LANES:(t + 1) * LANES], bound)
                    return carry

                cnt_ref[rs, :] = finish(lax.fori_loop(0, last + 1, tile_step, init))

        def search_cond(c):
            n, stop = c
            return jnp.logical_and(n < 2 * 32 + 2, stop == 0)

        def search_step(c):
            n, _ = c
            stop = (jnp.min(done_ref[...]) > 0.0).astype(jnp.int32)
            lo = lo_ref[...]
            hi = hi_ref[...]
            cand_ref[...] = (lo >> 1) + (hi >> 1) + (lo & hi & 1)
            row_pass(cand_ref, lambda acc, keys, cand: acc + jnp.where(keys >= cand, 1.0, 0.0),
                     jnp.zeros((rows, LANES), F32), lambda acc: acc)
            cnt = jnp.sum(cnt_ref[...], axis=1, keepdims=True)
            cand = cand_ref[...]
            active = done_ref[...] == 0.0
            enough = cnt >= float(topk)
            lo = jnp.where(jnp.logical_and(active, enough), cand, lo)
            hi = jnp.where(jnp.logical_and(active, jnp.logical_not(enough)), cand, hi)
            lo_ref[...] = lo
            hi_ref[...] = hi
            extract = jnp.logical_and(active, cnt == float(topk - 1))
            ext_ref[...] = jnp.where(extract, 1.0, ext_ref[...])
            finished = jnp.logical_or(jnp.logical_or(cnt == float(topk), extract), hi - lo == 1)
            done_ref[...] = jnp.where(jnp.logical_and(active, finished), 1.0, done_ref[...])
            return n + 1, stop

        lax.while_loop(search_cond, search_step, (jnp.int32(0), jnp.int32(0)))

        def key_to_f32(keys):
            return pltpu.bitcast(keys ^ ((keys >> 31) & 0x7FFFFFFF), F32)

        row_pass(hi_ref, lambda acc, keys, hi: jnp.maximum(acc, jnp.where(keys < hi, keys, KEY_NEG_INF)),
                 jnp.full((rows, LANES), KEY_NEG_INF, jnp.int32), key_to_f32)
        below = jnp.broadcast_to(_f32_key(jnp.max(cnt_ref[...], axis=1, keepdims=True)), (tq, LANES))
        thr = jnp.where(ext_ref[...] > 0.0, below, lo_ref[...])
        tau_ref[...] = jnp.maximum(thr, INT_MIN + 1)

        m_ref[...] = jnp.full(m_ref.shape, M_INIT, F32)
        l_ref[...] = jnp.zeros(l_ref.shape, F32)
        acc_ref[...] = jnp.zeros(acc_ref.shape, F32)

    tau = tau_ref[...]
    for t in range(tk // LANES):
        ls = slice(t * LANES, (t + 1) * LANES)
        bias_ref[:, ls] = jnp.where(sc_ref[kj, :, ls] >= tau, 0.0, NEG_BIG)

    def add_bias(x, rs, ls):
        return x + bias_ref[rs, ls]

    heads = [slice(h * HEAD_DIM, (h + 1) * HEAD_DIM) for h in range(A_HEADS)]
    for h, hs in enumerate(heads):
        s_ref[h] = _nt_dot(qa_ref[:, hs], ka_ref[:, hs])
    if bounded:
        bound = bound_ref[...]
        for h in range(A_HEADS):
            _softmax_rows_bounded(s_ref.at[h], p_ref.at[h], l_ref.at[h], add_bias, bound)
        for h, hs in enumerate(heads):
            acc_ref[h] = acc_ref[h] + jnp.dot(p_ref[h], va_ref[:, hs], preferred_element_type=F32)
    else:
        for h in range(A_HEADS):
            _softmax_rows(s_ref.at[h], p_ref.at[h], alpha_ref.at[h], m_ref.at[h], l_ref.at[h], add_bias)
        for h, hs in enumerate(heads):
            acc_ref[h] = alpha_ref[h] * acc_ref[h] + jnp.dot(p_ref[h], va_ref[:, hs],
                                                             preferred_element_type=F32)

    @pl.when(kj == last)
    def _finish():
        for h in range(A_HEADS):
            hs = slice(h * HEAD_DIM, (h + 1) * HEAD_DIM)
            o = acc_ref[h] / l_ref[h]
            o_ref[:, hs] = (o * ga_ref[:, hs].astype(F32)).astype(BF16)


def _causal_steps(nq, tq, tk):
    qt, kt = [], []
    for i in range(nq):
        for j in range(((i + 1) * tq - 1) // tk + 1):
            qt.append(i)
            kt.append(j)
    return jnp.asarray(qt, jnp.int32), jnp.asarray(kt, jnp.int32)


def _attn_a(qi, wi, kab, qkr, vg, bound, batch, seq, tq, tk, topk, bounded):
    m = batch * seq
    nq = seq // tq
    nk = seq // tk
    qt, kt = _causal_steps(nq, tq, tk)

    def qrow(b, s, qt_ref, kt_ref):
        return b * nq + qt_ref[s]

    def krow(b, s, qt_ref, kt_ref):
        return b * nk + kt_ref[s]

    grid_spec = pltpu.PrefetchScalarGridSpec(
        num_scalar_prefetch=2,
        grid=(batch, int(qt.shape[0])),
        in_specs=[
            pl.BlockSpec((IDX_HEADS // 2, tq, LANES), lambda b, s, q, k: (0, qrow(b, s, q, k), 0)),
            pl.BlockSpec((tq, LANES), lambda b, s, q, k: (qrow(b, s, q, k), 0)),
            pl.BlockSpec((2, seq, LANES), lambda b, s, q, k: (0, b, 0)),
            pl.BlockSpec((tq, A_WIDTH), lambda b, s, q, k: (qrow(b, s, q, k), 0)),
            pl.BlockSpec((tk, A_WIDTH), lambda b, s, q, k: (krow(b, s, q, k), 1)),
            pl.BlockSpec((tk, A_WIDTH), lambda b, s, q, k: (krow(b, s, q, k), 0)),
            pl.BlockSpec((tq, A_WIDTH), lambda b, s, q, k: (qrow(b, s, q, k), 2)),
            pl.BlockSpec((1, LANES), lambda b, s, q, k: (0, 0)),
        ],
        out_specs=pl.BlockSpec((tq, A_WIDTH), lambda b, s, q, k: (qrow(b, s, q, k), 0)),
        scratch_shapes=[
            pltpu.VMEM((nk, tq, tk), jnp.int32),
            pltpu.VMEM((2, tq, LANES), F32),
            pltpu.VMEM((tq, LANES), jnp.int32),
            pltpu.VMEM((tq, LANES), jnp.int32),
            pltpu.VMEM((tq, LANES), jnp.int32),
            pltpu.VMEM((tq, LANES), jnp.int32),
            pltpu.VMEM((tq, LANES), F32),
            pltpu.VMEM((tq, LANES), F32),
            pltpu.VMEM((tq, LANES), F32),
            pltpu.VMEM((tq, tk), F32),
            pltpu.VMEM((A_HEADS, tq, tk), F32),
            pltpu.VMEM((A_HEADS, tq, tk), BF16),
            pltpu.VMEM((A_HEADS, tq, LANES), F32),
            pltpu.VMEM((A_HEADS, tq, LANES), F32),
            pltpu.VMEM((A_HEADS, tq, LANES), F32),
            pltpu.VMEM((A_HEADS, tq, HEAD_DIM), F32),
        ],
    )
    return pl.pallas_call(
        functools.partial(_a_kernel, tq=tq, tk=tk, topk=topk, rows=SELECT_ROWS, bounded=bounded),
        grid_spec=grid_spec,
        out_shape=jax.ShapeDtypeStruct((m, A_WIDTH), BF16),
        compiler_params=_params(("parallel", "arbitrary")),
        name="attn_a",
    )(qt, kt, qi, wi, kab, qkr, qkr, vg, vg, bound)


def _b_kernel(qt_ref, kt_ref, q_ref, k_ref, v_ref, sg_ref, gain_ref, lq1_ref, lk1_ref, lq2_ref, lk2_ref,
              bound_ref, o_ref, s_ref, p_ref, alpha_ref, m_ref, l_ref, acc_ref, *, tq, heads, lambda_init,
              bounded):
    step = pl.program_id(2)
    iq = qt_ref[step]
    kj = kt_ref[step]
    hw = 2 * HEAD_DIM
    chains = [(hh, c) for hh in range(heads) for c in range(2)]

    @pl.when(kj == 0)
    def _init():
        m_ref[...] = jnp.full(m_ref.shape, M_INIT, F32)
        l_ref[...] = jnp.zeros(l_ref.shape, F32)
        acc_ref[...] = jnp.zeros(acc_ref.shape, F32)

    def causal_mask(x, rs, ls):
        row = rs.start + lax.broadcasted_iota(jnp.int32, x.shape, 0)
        col = ls.start + lax.broadcasted_iota(jnp.int32, x.shape, 1)
        return jnp.where(col <= row, x, NEG_BIG)

    def attend(diagonal):
        adjust = causal_mask if diagonal else (lambda x, rs, ls: x)
        for i, (hh, c) in enumerate(chains):
            cs = slice(hh * hw + c * HEAD_DIM, hh * hw + (c + 1) * HEAD_DIM)
            s_ref[i] = _nt_dot(q_ref[:, cs], k_ref[:, cs])
        if bounded:
            bound = bound_ref[...]
            for i in range(len(chains)):
                _softmax_rows_bounded(s_ref.at[i], p_ref.at[i], l_ref.at[i], adjust, bound)
            for i, (hh, c) in enumerate(chains):
                acc_ref[i] = acc_ref[i] + jnp.dot(p_ref[i], v_ref[:, hh * hw:(hh + 1) * hw],
                                                  preferred_element_type=F32)
        else:
            for i in range(len(chains)):
                _softmax_rows(s_ref.at[i], p_ref.at[i], alpha_ref.at[i], m_ref.at[i], l_ref.at[i], adjust)
            for i, (hh, c) in enumerate(chains):
                pv = jnp.dot(p_ref[i], v_ref[:, hh * hw:(hh + 1) * hw], preferred_element_type=F32)
                alpha = alpha_ref[i]
                for t in range(hw // LANES):
                    ls = slice(t * LANES, (t + 1) * LANES)
                    acc_ref[i, :, ls] = alpha * acc_ref[i, :, ls] + pv[:, ls]

    @pl.when(kj < iq)
    def _below_diagonal():
        attend(False)

    @pl.when(kj == iq)
    def _diagonal_and_finish():
        attend(True)
        lam = (jnp.exp(jnp.sum(lq1_ref[...] * lk1_ref[...], axis=1, keepdims=True))
               - jnp.exp(jnp.sum(lq2_ref[...] * lk2_ref[...], axis=1, keepdims=True))
               + lambda_init)
        for hh in range(heads):
            o0 = acc_ref[2 * hh] / _tile_lanes(l_ref[2 * hh], hw)
            o1 = acc_ref[2 * hh + 1] / _tile_lanes(l_ref[2 * hh + 1], hw)
            o = o0 - lam * o1
            ms = jnp.mean(o * o, axis=-1, keepdims=True)
            y = o * lax.rsqrt(ms + SUBLN_EPS) * gain_ref[...] * (1.0 - lambda_init)
            hs = slice(hh * hw, (hh + 1) * hw)
            o_ref[:, hs] = (y * sg_ref[:, hs].astype(F32)).astype(BF16)


def _attn_b(qkr, vg, gain, lq1, lk1, lq2, lk2, bound, batch, seq, tq, heads, lambda_init, bounded):
    m = batch * seq
    nq = seq // tq
    bw = heads * 2 * HEAD_DIM
    groups = B_HEADS // heads
    chains = 2 * heads
    qt, kt = _causal_steps(nq, tq, tq)

    def qrow(b, s, qt_ref):
        return b * nq + qt_ref[s]

    vec = pl.BlockSpec((1, HEAD_DIM), lambda b, h, s, q, k: (0, 0))
    grid_spec = pltpu.PrefetchScalarGridSpec(
        num_scalar_prefetch=2,
        grid=(batch, groups, int(qt.shape[0])),
        in_specs=[
            pl.BlockSpec((tq, bw), lambda b, h, s, q, k: (qrow(b, s, q), 2 * groups + h)),
            pl.BlockSpec((tq, bw), lambda b, h, s, q, k: (qrow(b, s, k), 3 * groups + h)),
            pl.BlockSpec((tq, bw), lambda b, h, s, q, k: (qrow(b, s, k), groups + h)),
            pl.BlockSpec((tq, bw), lambda b, h, s, q, k: (qrow(b, s, q), 3 * groups + h)),
            pl.BlockSpec((1, 2 * HEAD_DIM), lambda b, h, s, q, k: (0, 0)),
            vec, vec, vec, vec,
            pl.BlockSpec((1, LANES), lambda b, h, s, q, k: (0, 0)),
        ],
        out_specs=pl.BlockSpec((tq, bw), lambda b, h, s, q, k: (qrow(b, s, q), h)),
        scratch_shapes=[
            pltpu.VMEM((chains, tq, tq), F32),
            pltpu.VMEM((chains, tq, tq), BF16),
            pltpu.VMEM((chains, tq, LANES), F32),
            pltpu.VMEM((chains, tq, LANES), F32),
            pltpu.VMEM((chains, tq, LANES), F32),
            pltpu.VMEM((chains, tq, 2 * HEAD_DIM), F32),
        ],
    )
    return pl.pallas_call(
        functools.partial(_b_kernel, tq=tq, heads=heads, lambda_init=lambda_init, bounded=bounded),
        grid_spec=grid_spec,
        out_shape=jax.ShapeDtypeStruct((m, B_WIDTH), BF16),
        compiler_params=_params(("parallel", "parallel", "arbitrary")),
        name="attn_b",
    )(qt, kt, qkr, qkr, vg, vg, gain, lq1, lk1, lq2, lk2, bound)


def _merge_kernel(oa_ref, ob_ref, wa_ref, wb_ref, ma_ref, mb_ref, o_ref):
    ya = jnp.dot(oa_ref[...], wa_ref[...], preferred_element_type=F32)
    yb = jnp.dot(ob_ref[...], wb_ref[...], preferred_element_type=F32)
    o_ref[...] = (ma_ref[...].astype(F32) * ya + mb_ref[...].astype(F32) * yb).astype(BF16)


def _merge(oa, ob, w_a, w_b, vg, d_model, tm, tn):
    m = oa.shape[0]
    gate_a0 = (A_WIDTH + B_WIDTH) * 2 // tn
    gate_b0 = gate_a0 + d_model // tn
    return pl.pallas_call(
        _merge_kernel,
        grid=(m // tm, d_model // tn),
        in_specs=[
            pl.BlockSpec((tm, A_WIDTH), lambda i, j: (i, 0)),
            pl.BlockSpec((tm, B_WIDTH), lambda i, j: (i, 0)),
            pl.BlockSpec((A_WIDTH, tn), lambda i, j: (0, j)),
            pl.BlockSpec((B_WIDTH, tn), lambda i, j: (0, j)),
            pl.BlockSpec((tm, tn), lambda i, j: (i, gate_a0 + j)),
            pl.BlockSpec((tm, tn), lambda i, j: (i, gate_b0 + j)),
        ],
        out_specs=pl.BlockSpec((tm, tn), lambda i, j: (i, j)),
        out_shape=jax.ShapeDtypeStruct((m, d_model), BF16),
        compiler_params=_params(("parallel", "arbitrary")),
        name="merge",
    )(oa, ob, w_a, w_b, vg, vg)


def _out_kernel(mg_ref, w_ref, x_ref, o_ref):
    o_ref[...] = x_ref[...] + jnp.dot(mg_ref[...], w_ref[...], preferred_element_type=F32)


def _out_proj(merged, w_out, x2d, tm, tn):
    m, d = x2d.shape
    return pl.pallas_call(
        _out_kernel,
        grid=(m // tm, d // tn),
        in_specs=[
            pl.BlockSpec((tm, d), lambda i, j: (i, 0)),
            pl.BlockSpec((d, tn), lambda i, j: (0, j)),
            pl.BlockSpec((tm, tn), lambda i, j: (i, j)),
        ],
        out_specs=pl.BlockSpec((tm, tn), lambda i, j: (i, j)),
        out_shape=jax.ShapeDtypeStruct((m, d), F32),
        compiler_params=_params(("parallel", "arbitrary")),
        name="out_proj",
    )(merged, w_out, x2d)


def _qi_column_order():
    lane = np.arange(LANES)
    quarter, r = lane // 32, lane % 32
    dim = r + 32 * (quarter // 2)
    q_cols = np.concatenate([(2 * p + quarter % 2) * IDX_DIM + dim for p in range(IDX_HEADS // 2)])
    return q_cols, dim, (quarter % 2 == 0)


def _layer(x2d, tabs, norm_gain, w_in, a_q_gain, a_k_gain, b_q_gain, b_k_gain,
           lq1, lk1, lq2, lk2, b_subln_gain, w_o_a, w_o_b, w_out, layer, batch, seq):
    d_model = x2d.shape[1]
    cos_h, sin_h, cos_i, sin_i = tabs
    sizes = (A_WIDTH,) * 4 + (IDX_HEADS * IDX_DIM, IDX_DIM, IDX_HEADS) + (B_WIDTH,) * 4 + (d_model,) * 2
    off = np.concatenate([[0], np.cumsum(sizes)])
    seg = lambda i: w_in[:, off[i]:off[i + 1]]
    (w_qa, w_ka, w_va, w_ga, w_qi, w_ki, w_wi, w_qb, w_kb, w_vb, w_gb, w_ma, w_mb) = [seg(i) for i in range(13)]

    scale = HEAD_DIM ** -0.5 * LOG2E
    w_qk = jnp.concatenate([w_qa, w_ka, w_qb, w_kb], axis=1).astype(BF16)
    gain_qk = jnp.concatenate([jnp.tile(a_q_gain * scale, A_HEADS), jnp.tile(a_k_gain, A_HEADS),
                               jnp.tile(b_q_gain * scale, 2 * B_HEADS), jnp.tile(b_k_gain, 2 * B_HEADS)])[None, :]
    q_cols, k_dim, even = _qi_column_order()
    w_qi_p = w_qi[:, q_cols].astype(BF16)
    w_k_rep = w_ki[:, k_dim]
    zeros = jnp.zeros_like(w_k_rep)
    w_kw = jnp.concatenate([
        jnp.where(even[None, :], w_k_rep, zeros), jnp.where(even[None, :], zeros, w_k_rep),
        jnp.pad(w_wi, ((0, 0), (0, 2 * LANES - IDX_HEADS)))], axis=1).astype(BF16)
    w_vg = jnp.concatenate([w_va, w_vb, w_ga, w_gb, w_ma, w_mb], axis=1).astype(BF16)

    m = x2d.shape[0]
    tm, tn = 1024, 1024
    h = _norm(x2d, norm_gain[None, :], 512)

    tab_spec = pl.BlockSpec((tm, LANES), lambda i, j: (i, 0))
    qkr = _proj_qk(h, w_qk, gain_qk, cos_h, sin_h, tm, tn)
    qi = _proj(h, w_qi_p, (cos_i, sin_i), [tab_spec, tab_spec], _ep_qi,
               jax.ShapeDtypeStruct((IDX_HEADS // 2, m, LANES), BF16),
               pl.BlockSpec((tn // LANES, tm, LANES), lambda i, j: (j, i, 0)), tm, tn, "proj_qi")
    kab, wi = _proj(h, w_kw, (cos_i, sin_i), [tab_spec, tab_spec], _ep_kw,
                    (jax.ShapeDtypeStruct((2, m, LANES), BF16), jax.ShapeDtypeStruct((m, LANES), F32)),
                    (pl.BlockSpec((2, tm, LANES), lambda i, j: (0, i, 0)),
                     pl.BlockSpec((tm, LANES), lambda i, j: (i, 0))), tm, w_kw.shape[1], "proj_kw")
    n_plain = (A_WIDTH + B_WIDTH) // tn
    vg = _proj(h, w_vg, (), [], functools.partial(_ep_vg, n_plain=n_plain, n_silu=n_plain),
               jax.ShapeDtypeStruct((m, w_vg.shape[1]), BF16),
               pl.BlockSpec((tm, tn), lambda i, j: (i, j)), tm, tn, "proj_vg")

    def logit_bound(q_gain, k_gain):
        return HEAD_DIM * 1.02 * jnp.max(jnp.abs(q_gain * scale)) * jnp.max(jnp.abs(k_gain))

    def with_bound(bound, attend):
        row = jnp.full((1, LANES), bound, F32)
        return lax.cond(bound <= MAX_LOGIT_BOUND, lambda r: attend(r, True), lambda r: attend(r, False), row)

    topk = min(TOPK_MAX, seq // 4)
    oa = with_bound(logit_bound(a_q_gain, a_k_gain),
                    lambda r, f: _attn_a(qi, wi, kab, qkr, vg, r, batch, seq, 256, 512, topk, f))
    lambda_init = 0.8 - 0.6 * math.exp(-0.3 * layer)
    ob = with_bound(logit_bound(b_q_gain, b_k_gain),
                    lambda r, f: _attn_b(qkr, vg, b_subln_gain[None, :], lq1[None, :], lk1[None, :], lq2[None, :],
                                         lk2[None, :], r, batch, seq, 512, 4, lambda_init, f))
    merged = _merge(oa, ob, w_o_a.astype(BF16), w_o_b.astype(BF16), vg, d_model, 1024, 512)
    return _out_proj(merged, w_out.astype(BF16), x2d, 1024, 512)


def kernel(x, positions, norm_gain, w_in, a_q_gain, a_k_gain, b_q_gain, b_k_gain, lambda_q1, lambda_k1,
           lambda_q2, lambda_k2, b_subln_gain, w_o_a, w_o_b, w_out):
    batch, seq, d_model = x.shape
    m = batch * seq
    pos = positions.astype(F32).reshape(m, 1)
    ang_h = pos * ROPE_THETA ** (-jnp.arange(0, HEAD_DIM, 2, dtype=F32) / HEAD_DIM)
    ang_i = pos * ROPE_THETA ** (-jnp.arange(0, IDX_DIM, 2, dtype=F32) / IDX_DIM)
    ch, sh, ci, si = jnp.cos(ang_h), jnp.sin(ang_h), jnp.cos(ang_i), jnp.sin(ang_i)
    tabs = (jnp.concatenate([ch, ch], axis=1), jnp.concatenate([-sh, sh], axis=1),
            jnp.concatenate([ci] * 4, axis=1), jnp.concatenate([-si, -si, si, si], axis=1))
    x2d = x.reshape(m, d_model)
    for layer in range(norm_gain.shape[0]):
        x2d = _layer(x2d, tabs, norm_gain[layer], w_in[layer], a_q_gain[layer], a_k_gain[layer],
                     b_q_gain[layer], b_k_gain[layer], lambda_q1[layer], lambda_k1[layer],
                     lambda_q2[layer], lambda_k2[layer], b_subln_gain[layer], w_o_a[layer], w_o_b[layer],
                     w_out[layer], layer, batch, seq)
    return x2d.reshape(batch, seq, d_model)
```

```python
import functools
import math

import jax
import jax.numpy as jnp
import numpy as np
from jax import lax
from jax.experimental import pallas as pl
from jax.experimental.pallas import tpu as pltpu

F32 = jnp.float32
BF16 = jnp.bfloat16

HEAD_DIM = 128
A_HEADS = 8
A_WIDTH = A_HEADS * HEAD_DIM
IDX_HEADS = 16
IDX_DIM = 64
TOPK_MAX = 256
B_HEADS = 4
B_WIDTH = B_HEADS * 2 * HEAD_DIM
ROPE_THETA = 10000.0
NORM_EPS = 1e-6
SUBLN_EPS = 1e-5

LANES = 128
INT_MIN = -(2 ** 31)
KEY_NEG_INF = INT_MIN + 0x7FFFFF
NEG_BIG = -1e30
M_INIT = -5e29
LOG2E = math.log2(math.e)
MAX_LOGIT_BOUND = 60.0
SELECT_ROWS = 64
SOFTMAX_ROWS = 32
PROJ_SUB = 256
VMEM_LIMIT_BYTES = 56 * 1024 * 1024


def _nt_dot(a, b):
    return lax.dot_general(a, b, (((1,), (1,)), ((), ())), preferred_element_type=F32)


def _tile_lanes(v, width):
    reps = width // LANES
    return v if reps == 1 else jnp.concatenate([v] * reps, axis=1)


def _params(semantics):
    return pltpu.CompilerParams(dimension_semantics=semantics, vmem_limit_bytes=VMEM_LIMIT_BYTES)


def _norm_kernel(x_ref, g_ref, h_ref):
    x = x_ref[...]
    ms = jnp.mean(x * x, axis=-1, keepdims=True)
    h_ref[...] = (x * lax.rsqrt(ms + NORM_EPS) * g_ref[...]).astype(BF16)


def _norm(x2d, gain, tm):
    m, d = x2d.shape
    return pl.pallas_call(
        _norm_kernel,
        grid=(m // tm,),
        in_specs=[pl.BlockSpec((tm, d), lambda i: (i, 0)), pl.BlockSpec((1, d), lambda i: (0, 0))],
        out_specs=pl.BlockSpec((tm, d), lambda i: (i, 0)),
        out_shape=jax.ShapeDtypeStruct((m, d), BF16),
        compiler_params=_params(("parallel",)),
        name="norm",
    )(x2d, gain)


def _proj_body(h_ref, w_ref, *rest, n_aux, epilogue):
    tn = w_ref.shape[1]

    def sub_dot(t):
        return jnp.dot(h_ref[...], w_ref[:, t * PROJ_SUB:(t + 1) * PROJ_SUB], preferred_element_type=F32)

    epilogue(sub_dot, tn // PROJ_SUB, rest[:n_aux], rest[n_aux:])


def _proj(h, w, aux, aux_specs, epilogue, out_shape, out_specs, tm, tn, name):
    m, k = h.shape
    n = w.shape[1]
    return pl.pallas_call(
        functools.partial(_proj_body, n_aux=len(aux), epilogue=epilogue),
        grid=(m // tm, n // tn),
        in_specs=[pl.BlockSpec((tm, k), lambda i, j: (i, 0)),
                  pl.BlockSpec((k, tn), lambda i, j: (0, j))] + aux_specs,
        out_specs=out_specs,
        out_shape=out_shape,
        compiler_params=_params(("parallel", "arbitrary")),
        name=name,
    )(h, w, *aux)


def _rope128(y, cos_full, sin_signed):
    return y * cos_full + pltpu.roll(y, 64, 1) * sin_signed


def _proj_qk_kernel(h_ref, w_ref, g_ref, c_ref, s_ref, o_ref, acc_ref):
    acc_ref[...] = jnp.dot(h_ref[...], w_ref[...], preferred_element_type=F32)
    same_head = (lax.broadcasted_iota(jnp.int32, (PROJ_SUB, PROJ_SUB), 0) // HEAD_DIM
                 == lax.broadcasted_iota(jnp.int32, (PROJ_SUB, PROJ_SUB), 1) // HEAD_DIM)
    head_mean = jnp.where(same_head, 1.0 / HEAD_DIM, 0.0).astype(BF16)

    def sub_tile(t, carry):
        cs = pl.ds(pl.multiple_of(t * PROJ_SUB, PROJ_SUB), PROJ_SUB)
        acc = acc_ref[:, cs]
        ms = jnp.dot((acc * acc).astype(BF16), head_mean, preferred_element_type=F32)
        y = acc * lax.rsqrt(ms + NORM_EPS) * g_ref[:, cs]
        c = c_ref[...]
        s = s_ref[...]
        for u in range(PROJ_SUB // LANES):
            ls = pl.ds(pl.multiple_of(t * PROJ_SUB + u * LANES, LANES), LANES)
            o_ref[:, ls] = _rope128(y[:, u * LANES:(u + 1) * LANES], c, s).astype(BF16)
        return carry

    lax.fori_loop(0, acc_ref.shape[1] // PROJ_SUB, sub_tile, 0)


def _proj_qk(h, w, gain, cos_full, sin_signed, tm, tn):
    m, k = h.shape
    n = w.shape[1]
    tab_spec = pl.BlockSpec((tm, LANES), lambda i, j: (i, 0))
    return pl.pallas_call(
        _proj_qk_kernel,
        grid=(m // tm, n // tn),
        in_specs=[pl.BlockSpec((tm, k), lambda i, j: (i, 0)),
                  pl.BlockSpec((k, tn), lambda i, j: (0, j)),
                  pl.BlockSpec((1, tn), lambda i, j: (0, j)), tab_spec, tab_spec],
        out_specs=pl.BlockSpec((tm, tn), lambda i, j: (i, j)),
        out_shape=jax.ShapeDtypeStruct((m, n), BF16),
        scratch_shapes=[pltpu.VMEM((tm, tn), F32)],
        compiler_params=_params(("parallel", "arbitrary")),
        name="proj_qk",
    )(h, w, gain, cos_full, sin_signed)


def _ep_qi(sub_dot, n_sub, aux, outs):
    c_ref, s_ref = aux
    (o_ref,) = outs
    c = c_ref[...]
    s = s_ref[...]
    per = PROJ_SUB // LANES
    for t in range(n_sub):
        acc = sub_dot(t)
        for u in range(per):
            o_ref[t * per + u] = _rope128(acc[:, u * LANES:(u + 1) * LANES], c, s).astype(BF16)


def _ep_kw(sub_dot, n_sub, aux, outs):
    c_ref, s_ref = aux
    kab_ref, w_ref = outs
    c = c_ref[...]
    s = s_ref[...]
    acc = sub_dot(0)
    for u in range(2):
        kab_ref[u] = _rope128(acc[:, u * LANES:(u + 1) * LANES], c, s).astype(BF16)
    w_ref[...] = sub_dot(1)[:, :LANES] * (IDX_HEADS ** -0.5 * IDX_DIM ** -0.5)


def _ep_vg(sub_dot, n_sub, aux, outs, *, n_plain, n_silu):
    (o_ref,) = outs
    j = pl.program_id(1)

    def store(fn):
        for t in range(n_sub):
            o_ref[:, t * PROJ_SUB:(t + 1) * PROJ_SUB] = fn(sub_dot(t)).astype(BF16)

    @pl.when(j < n_plain)
    def _():
        store(lambda a: a)

    @pl.when(jnp.logical_and(j >= n_plain, j < n_plain + n_silu))
    def _():
        store(lambda a: a * jax.nn.sigmoid(a))

    @pl.when(j >= n_plain + n_silu)
    def _():
        store(jax.nn.sigmoid)


def _f32_key(v):
    bits = pltpu.bitcast(v, jnp.int32)
    return bits ^ ((bits >> 31) & 0x7FFFFFFF)


def _softmax_rows(s_ref, p_ref, alpha_ref, m_ref, l_ref, adjust):
    tq, tk = s_ref.shape
    for r in range(tq // SOFTMAX_ROWS):
        rs = slice(r * SOFTMAX_ROWS, (r + 1) * SOFTMAX_ROWS)
        lanes = [slice(t * LANES, (t + 1) * LANES) for t in range(tk // LANES)]
        s = [adjust(s_ref[rs, ls], rs, ls) for ls in lanes]
        m_prev = m_ref[rs, :]
        m_new = jnp.maximum(m_prev, jnp.max(functools.reduce(jnp.maximum, s), axis=1, keepdims=True))
        alpha = jnp.exp2(m_prev - m_new)
        p = [jnp.exp2(x - m_new) for x in s]
        l_ref[rs, :] = alpha * l_ref[rs, :] + jnp.sum(functools.reduce(jnp.add, p), axis=1, keepdims=True)
        for ls, x in zip(lanes, p):
            p_ref[rs, ls] = x.astype(BF16)
        alpha_ref[rs, :] = alpha
        m_ref[rs, :] = m_new


def _softmax_rows_bounded(s_ref, p_ref, l_ref, adjust):
    tq, tk = s_ref.shape
    for r in range(tq // SOFTMAX_ROWS):
        rs = slice(r * SOFTMAX_ROWS, (r + 1) * SOFTMAX_ROWS)
        lanes = [slice(t * LANES, (t + 1) * LANES) for t in range(tk // LANES)]
        p = [jnp.exp2(adjust(s_ref[rs, ls], rs, ls)) for ls in lanes]
        l_ref[rs, :] = l_ref[rs, :] + jnp.sum(functools.reduce(jnp.add, p), axis=1, keepdims=True)
        for ls, x in zip(lanes, p):
            p_ref[rs, ls] = x.astype(BF16)


def _a_kernel(qt_ref, kt_ref, qi_ref, wi_ref, kab_ref, qa_ref, ka_ref, va_ref, ga_ref, o_ref,
              sc_ref, gmax_ref, tau_ref, lo_ref, hi_ref, cand_ref, done_ref, ext_ref, cnt_ref, bias_ref,
              s_ref, p_ref, alpha_ref, m_ref, l_ref, acc_ref, *, tq, tk, topk, rows, bounded):
    step = pl.program_id(1)
    iq = qt_ref[step]
    kj = kt_ref[step]
    last = ((iq + 1) * tq - 1) // tk

    @pl.when(kj == 0)
    def _index_and_select():
        w = wi_ref[...]
        row = iq * tq + lax.broadcasted_iota(jnp.int32, (tq, tk), 0)
        col0 = lax.broadcasted_iota(jnp.int32, (tq, tk), 1)
        gmax_ref[...] = jnp.full(gmax_ref.shape, -jnp.inf, F32)

        def score_tile(j, carry):
            start = pl.multiple_of(j * tk, tk)
            k_even = kab_ref[0, pl.ds(start, tk), :]
            k_odd = kab_ref[1, pl.ds(start, tk), :]
            acc = jnp.zeros((tq, tk), F32)
            for p in range(IDX_HEADS // 2):
                q_p = qi_ref[p]
                s_even = jnp.maximum(_nt_dot(q_p, k_even), 0.0)
                s_odd = jnp.maximum(_nt_dot(q_p, k_odd), 0.0)
                acc = acc + w[:, 2 * p:2 * p + 1] * s_even + w[:, 2 * p + 1:2 * p + 2] * s_odd
            causal = col0 + j * tk <= row
            sc_ref[j] = jnp.where(causal, _f32_key(acc), INT_MIN)
            accm = jnp.where(causal, acc, -jnp.inf)
            half = tk // 2
            for g in range(2):
                gm = gmax_ref[g]
                for t in range(half // LANES):
                    lo = g * half + t * LANES
                    gm = jnp.maximum(gm, accm[:, lo:lo + LANES])
                gmax_ref[g] = gm
            return carry

        lax.fori_loop(0, last + 1, score_tile, 0)

        g0 = gmax_ref[0]
        g1 = gmax_ref[1]
        row_max = jnp.max(jnp.maximum(g0, g1), axis=1, keepdims=True)
        row_low = jnp.min(jnp.minimum(g0, g1), axis=1, keepdims=True)
        lo0 = jnp.broadcast_to(_f32_key(row_low), (tq, LANES))
        hi0 = jnp.broadcast_to(_f32_key(row_max), (tq, LANES)) + 1
        lo_ref[...] = lo0
        hi_ref[...] = hi0
        done_ref[...] = jnp.where(hi0 - lo0 == 1, 1.0, 0.0)
        ext_ref[...] = jnp.zeros((tq, LANES), F32)

        def row_pass(ref, body, init, finish):
            for rc in range(tq // rows):
                rs = pl.ds(rc * rows, rows)
                bound = ref[rs, :]

                def tile_step(j, carry):
                    keys = sc_ref[j, rs, :]
                    for t in range(tk // LANES):
                        carry = body(carry, keys[:, t * LANES:(t + 1) * LANES], bound)
                    return carry

                cnt_ref[rs, :] = finish(lax.fori_loop(0, last + 1, tile_step, init))

        def search_cond(c):
            n, stop = c
            return jnp.logical_and(n < 2 * 32 + 2, stop == 0)

        def search_step(c):
            n, _ = c
            stop = (jnp.min(done_ref[...]) > 0.0).astype(jnp.int32)
            lo = lo_ref[...]
            hi = hi_ref[...]
            cand_ref[...] = (lo >> 1) + (hi >> 1) + (lo & hi & 1)
            row_pass(cand_ref, lambda acc, keys, cand: acc + jnp.where(keys >= cand, 1.0, 0.0),
                     jnp.zeros((rows, LANES), F32), lambda acc: acc)
            cnt = jnp.sum(cnt_ref[...], axis=1, keepdims=True)
            cand = cand_ref[...]
            active = done_ref[...] == 0.0
            enough = cnt >= float(topk)
            lo = jnp.where(jnp.logical_and(active, enough), cand, lo)
            hi = jnp.where(jnp.logical_and(active, jnp.logical_not(enough)), cand, hi)
            lo_ref[...] = lo
            hi_ref[...] = hi
            extract = jnp.logical_and(active, cnt == float(topk - 1))
            ext_ref[...] = jnp.where(extract, 1.0, ext_ref[...])
            finished = jnp.logical_or(jnp.logical_or(cnt == float(topk), extract), hi - lo == 1)
            done_ref[...] = jnp.where(jnp.logical_and(active, finished), 1.0, done_ref[...])
            return n + 1, stop

        lax.while_loop(search_cond, search_step, (jnp.int32(0), jnp.int32(0)))

        def key_to_f32(keys):
            return pltpu.bitcast(keys ^ ((keys >> 31) & 0x7FFFFFFF), F32)

        row_pass(hi_ref, lambda acc, keys, hi: jnp.maximum(acc, jnp.where(keys < hi, keys, KEY_NEG_INF)),
                 jnp.full((rows, LANES), KEY_NEG_INF, jnp.int32), key_to_f32)
        below = jnp.broadcast_to(_f32_key(jnp.max(cnt_ref[...], axis=1, keepdims=True)), (tq, LANES))
        thr = jnp.where(ext_ref[...] > 0.0, below, lo_ref[...])
        tau_ref[...] = jnp.maximum(thr, INT_MIN + 1)

        m_ref[...] = jnp.full(m_ref.shape, M_INIT, F32)
        l_ref[...] = jnp.zeros(l_ref.shape, F32)
        acc_ref[...] = jnp.zeros(acc_ref.shape, F32)

    tau = tau_ref[...]
    for t in range(tk // LANES):
        ls = slice(t * LANES, (t + 1) * LANES)
        bias_ref[:, ls] = jnp.where(sc_ref[kj, :, ls] >= tau, 0.0, NEG_BIG)

    def add_bias(x, rs, ls):
        return x + bias_ref[rs, ls]

    heads = [slice(h * HEAD_DIM, (h + 1) * HEAD_DIM) for h in range(A_HEADS)]
    for h, hs in enumerate(heads):
        s_ref[h] = _nt_dot(qa_ref[:, hs], ka_ref[:, hs])
    if bounded:
        for h in range(A_HEADS):
            _softmax_rows_bounded(s_ref.at[h], p_ref.at[h], l_ref.at[h], add_bias)
        for h, hs in enumerate(heads):
            acc_ref[h] = acc_ref[h] + jnp.dot(p_ref[h], va_ref[:, hs], preferred_element_type=F32)
    else:
        for h in range(A_HEADS):
            _softmax_rows(s_ref.at[h], p_ref.at[h], alpha_ref.at[h], m_ref.at[h], l_ref.at[h], add_bias)
        for h, hs in enumerate(heads):
            acc_ref[h] = alpha_ref[h] * acc_ref[h] + jnp.dot(p_ref[h], va_ref[:, hs],
                                                             preferred_element_type=F32)

    @pl.when(kj == last)
    def _finish():
        for h in range(A_HEADS):
            hs = slice(h * HEAD_DIM, (h + 1) * HEAD_DIM)
            o = acc_ref[h] / l_ref[h]
            o_ref[:, hs] = (o * ga_ref[:, hs].astype(F32)).astype(BF16)


def _causal_steps(nq, tq, tk):
    qt, kt = [], []
    for i in range(nq):
        for j in range(((i + 1) * tq - 1) // tk + 1):
            qt.append(i)
            kt.append(j)
    return jnp.asarray(qt, jnp.int32), jnp.asarray(kt, jnp.int32)


def _attn_a(qi, wi, kab, qkr, vg, batch, seq, tq, tk, topk, bounded):
    m = batch * seq
    nq = seq // tq
    nk = seq // tk
    qt, kt = _causal_steps(nq, tq, tk)

    def qrow(b, s, qt_ref, kt_ref):
        return b * nq + qt_ref[s]

    def krow(b, s, qt_ref, kt_ref):
        return b * nk + kt_ref[s]

    grid_spec = pltpu.PrefetchScalarGridSpec(
        num_scalar_prefetch=2,
        grid=(batch, int(qt.shape[0])),
        in_specs=[
            pl.BlockSpec((IDX_HEADS // 2, tq, LANES), lambda b, s, q, k: (0, qrow(b, s, q, k), 0)),
            pl.BlockSpec((tq, LANES), lambda b, s, q, k: (qrow(b, s, q, k), 0)),
            pl.BlockSpec((2, seq, LANES), lambda b, s, q, k: (0, b, 0),
                         pipeline_mode=pl.Buffered(1)),
            pl.BlockSpec((tq, A_WIDTH), lambda b, s, q, k: (qrow(b, s, q, k), 0)),
            pl.BlockSpec((tk, A_WIDTH), lambda b, s, q, k: (krow(b, s, q, k), 1)),
            pl.BlockSpec((tk, A_WIDTH), lambda b, s, q, k: (krow(b, s, q, k), 0)),
            pl.BlockSpec((tq, A_WIDTH), lambda b, s, q, k: (qrow(b, s, q, k), 2)),
        ],
        out_specs=pl.BlockSpec((tq, A_WIDTH), lambda b, s, q, k: (qrow(b, s, q, k), 0)),
        scratch_shapes=[
            pltpu.VMEM((nk, tq, tk), jnp.int32),
            pltpu.VMEM((2, tq, LANES), F32),
            pltpu.VMEM((tq, LANES), jnp.int32),
            pltpu.VMEM((tq, LANES), jnp.int32),
            pltpu.VMEM((tq, LANES), jnp.int32),
            pltpu.VMEM((tq, LANES), jnp.int32),
            pltpu.VMEM((tq, LANES), F32),
            pltpu.VMEM((tq, LANES), F32),
            pltpu.VMEM((tq, LANES), F32),
            pltpu.VMEM((tq, tk), F32),
            pltpu.VMEM((A_HEADS, tq, tk), F32),
            pltpu.VMEM((A_HEADS, tq, tk), BF16),
            pltpu.VMEM((A_HEADS, tq, LANES), F32),
            pltpu.VMEM((A_HEADS, tq, LANES), F32),
            pltpu.VMEM((A_HEADS, tq, LANES), F32),
            pltpu.VMEM((A_HEADS, tq, HEAD_DIM), F32),
        ],
    )
    return pl.pallas_call(
        functools.partial(_a_kernel, tq=tq, tk=tk, topk=topk, rows=SELECT_ROWS, bounded=bounded),
        grid_spec=grid_spec,
        out_shape=jax.ShapeDtypeStruct((m, A_WIDTH), BF16),
        compiler_params=_params(("parallel", "arbitrary")),
        name="attn_a",
    )(qt, kt, qi, wi, kab, qkr, qkr, vg, vg)


def _b_kernel(qt_ref, kt_ref, q_ref, k_ref, v_ref, sg_ref, gain_ref, lq1_ref, lk1_ref, lq2_ref, lk2_ref,
              o_ref, s_ref, p_ref, alpha_ref, m_ref, l_ref, acc_ref, *, tq, heads, lambda_init, bounded):
    step = pl.program_id(2)
    iq = qt_ref[step]
    kj = kt_ref[step]
    hw = 2 * HEAD_DIM
    chains = [(hh, c) for hh in range(heads) for c in range(2)]

    @pl.when(kj == 0)
    def _init():
        m_ref[...] = jnp.full(m_ref.shape, M_INIT, F32)
        l_ref[...] = jnp.zeros(l_ref.shape, F32)
        acc_ref[...] = jnp.zeros(acc_ref.shape, F32)

    def causal_mask(x, rs, ls):
        row = rs.start + lax.broadcasted_iota(jnp.int32, x.shape, 0)
        col = ls.start + lax.broadcasted_iota(jnp.int32, x.shape, 1)
        return jnp.where(col <= row, x, NEG_BIG)

    def attend(diagonal):
        adjust = causal_mask if diagonal else (lambda x, rs, ls: x)
        for i, (hh, c) in enumerate(chains):
            cs = slice(hh * hw + c * HEAD_DIM, hh * hw + (c + 1) * HEAD_DIM)
            s_ref[i] = _nt_dot(q_ref[:, cs], k_ref[:, cs])
        if bounded:
            for i in range(len(chains)):
                _softmax_rows_bounded(s_ref.at[i], p_ref.at[i], l_ref.at[i], adjust)
            for i, (hh, c) in enumerate(chains):
                acc_ref[i] = acc_ref[i] + jnp.dot(p_ref[i], v_ref[:, hh * hw:(hh + 1) * hw],
                                                  preferred_element_type=F32)
        else:
            for i in range(len(chains)):
                _softmax_rows(s_ref.at[i], p_ref.at[i], alpha_ref.at[i], m_ref.at[i], l_ref.at[i], adjust)
            for i, (hh, c) in enumerate(chains):
                pv = jnp.dot(p_ref[i], v_ref[:, hh * hw:(hh + 1) * hw], preferred_element_type=F32)
                alpha = alpha_ref[i]
                for t in range(hw // LANES):
                    ls = slice(t * LANES, (t + 1) * LANES)
                    acc_ref[i, :, ls] = alpha * acc_ref[i, :, ls] + pv[:, ls]

    @pl.when(kj < iq)
    def _below_diagonal():
        attend(False)

    @pl.when(kj == iq)
    def _diagonal_and_finish():
        attend(True)
        lam = (jnp.exp(jnp.sum(lq1_ref[...] * lk1_ref[...], axis=1, keepdims=True))
               - jnp.exp(jnp.sum(lq2_ref[...] * lk2_ref[...], axis=1, keepdims=True))
               + lambda_init)
        for hh in range(heads):
            o0 = acc_ref[2 * hh] / _tile_lanes(l_ref[2 * hh], hw)
            o1 = acc_ref[2 * hh + 1] / _tile_lanes(l_ref[2 * hh + 1], hw)
            o = o0 - lam * o1
            ms = jnp.mean(o * o, axis=-1, keepdims=True)
            y = o * lax.rsqrt(ms + SUBLN_EPS) * gain_ref[...] * (1.0 - lambda_init)
            hs = slice(hh * hw, (hh + 1) * hw)
            o_ref[:, hs] = (y * sg_ref[:, hs].astype(F32)).astype(BF16)


def _attn_b(qkr, vg, gain, lq1, lk1, lq2, lk2, batch, seq, tq, heads, lambda_init, bounded):
    m = batch * seq
    nq = seq // tq
    bw = heads * 2 * HEAD_DIM
    groups = B_HEADS // heads
    chains = 2 * heads
    qt, kt = _causal_steps(nq, tq, tq)

    def qrow(b, s, qt_ref):
        return b * nq + qt_ref[s]

    vec = pl.BlockSpec((1, HEAD_DIM), lambda b, h, s, q, k: (0, 0))
    grid_spec = pltpu.PrefetchScalarGridSpec(
        num_scalar_prefetch=2,
        grid=(batch, groups, int(qt.shape[0])),
        in_specs=[
            pl.BlockSpec((tq, bw), lambda b, h, s, q, k: (qrow(b, s, q), 2 * groups + h)),
            pl.BlockSpec((tq, bw), lambda b, h, s, q, k: (qrow(b, s, k), 3 * groups + h)),
            pl.BlockSpec((tq, bw), lambda b, h, s, q, k: (qrow(b, s, k), groups + h)),
            pl.BlockSpec((tq, bw), lambda b, h, s, q, k: (qrow(b, s, q), 3 * groups + h)),
            pl.BlockSpec((1, 2 * HEAD_DIM), lambda b, h, s, q, k: (0, 0)),
            vec, vec, vec, vec,
        ],
        out_specs=pl.BlockSpec((tq, bw), lambda b, h, s, q, k: (qrow(b, s, q), h)),
        scratch_shapes=[
            pltpu.VMEM((chains, tq, tq), F32),
            pltpu.VMEM((chains, tq, tq), BF16),
            pltpu.VMEM((chains, tq, LANES), F32),
            pltpu.VMEM((chains, tq, LANES), F32),
            pltpu.VMEM((chains, tq, LANES), F32),
            pltpu.VMEM((chains, tq, 2 * HEAD_DIM), F32),
        ],
    )
    return pl.pallas_call(
        functools.partial(_b_kernel, tq=tq, heads=heads, lambda_init=lambda_init, bounded=bounded),
        grid_spec=grid_spec,
        out_shape=jax.ShapeDtypeStruct((m, B_WIDTH), BF16),
        compiler_params=_params(("parallel", "parallel", "arbitrary")),
        name="attn_b",
    )(qt, kt, qkr, qkr, vg, vg, gain, lq1, lk1, lq2, lk2)


def _merge_kernel(oa_ref, ob_ref, wa_ref, wb_ref, ma_ref, mb_ref, o_ref):
    ya = jnp.dot(oa_ref[...], wa_ref[...], preferred_element_type=F32)
    yb = jnp.dot(ob_ref[...], wb_ref[...], preferred_element_type=F32)
    o_ref[...] = (ma_ref[...].astype(F32) * ya + mb_ref[...].astype(F32) * yb).astype(BF16)


def _merge(oa, ob, w_a, w_b, vg, d_model, tm, tn):
    m = oa.shape[0]
    gate_a0 = (A_WIDTH + B_WIDTH) * 2 // tn
    gate_b0 = gate_a0 + d_model // tn
    return pl.pallas_call(
        _merge_kernel,
        grid=(m // tm, d_model // tn),
        in_specs=[
            pl.BlockSpec((tm, A_WIDTH), lambda i, j: (i, 0)),
            pl.BlockSpec((tm, B_WIDTH), lambda i, j: (i, 0)),
            pl.BlockSpec((A_WIDTH, tn), lambda i, j: (0, j)),
            pl.BlockSpec((B_WIDTH, tn), lambda i, j: (0, j)),
            pl.BlockSpec((tm, tn), lambda i, j: (i, gate_a0 + j)),
            pl.BlockSpec((tm, tn), lambda i, j: (i, gate_b0 + j)),
        ],
        out_specs=pl.BlockSpec((tm, tn), lambda i, j: (i, j)),
        out_shape=jax.ShapeDtypeStruct((m, d_model), BF16),
        compiler_params=_params(("parallel", "arbitrary")),
        name="merge",
    )(oa, ob, w_a, w_b, vg, vg)


def _out_kernel(mg_ref, w_ref, x_ref, o_ref):
    o_ref[...] = x_ref[...] + jnp.dot(mg_ref[...], w_ref[...], preferred_element_type=F32)


def _out_proj(merged, w_out, x2d, tm, tn):
    m, d = x2d.shape
    return pl.pallas_call(
        _out_kernel,
        grid=(m // tm, d // tn),
        in_specs=[
            pl.BlockSpec((tm, d), lambda i, j: (i, 0)),
            pl.BlockSpec((d, tn), lambda i, j: (0, j)),
            pl.BlockSpec((tm, tn), lambda i, j: (i, j)),
        ],
        out_specs=pl.BlockSpec((tm, tn), lambda i, j: (i, j)),
        out_shape=jax.ShapeDtypeStruct((m, d), F32),
        compiler_params=_params(("parallel", "arbitrary")),
        name="out_proj",
    )(merged, w_out, x2d)


def _qi_column_order():
    lane = np.arange(LANES)
    quarter, r = lane // 32, lane % 32
    dim = r + 32 * (quarter // 2)
    q_cols = np.concatenate([(2 * p + quarter % 2) * IDX_DIM + dim for p in range(IDX_HEADS // 2)])
    return q_cols, dim, (quarter % 2 == 0)


def _layer(x2d, tabs, norm_gain, w_in, a_q_gain, a_k_gain, b_q_gain, b_k_gain,
           lq1, lk1, lq2, lk2, b_subln_gain, w_o_a, w_o_b, w_out, layer, batch, seq):
    d_model = x2d.shape[1]
    cos_h, sin_h, cos_i, sin_i = tabs
    sizes = (A_WIDTH,) * 4 + (IDX_HEADS * IDX_DIM, IDX_DIM, IDX_HEADS) + (B_WIDTH,) * 4 + (d_model,) * 2
    off = np.concatenate([[0], np.cumsum(sizes)])
    seg = lambda i: w_in[:, off[i]:off[i + 1]]
    (w_qa, w_ka, w_va, w_ga, w_qi, w_ki, w_wi, w_qb, w_kb, w_vb, w_gb, w_ma, w_mb) = [seg(i) for i in range(13)]

    scale = HEAD_DIM ** -0.5 * LOG2E
    w_qk = jnp.concatenate([w_qa, w_ka, w_qb, w_kb], axis=1).astype(BF16)
    gain_qk = jnp.concatenate([jnp.tile(a_q_gain * scale, A_HEADS), jnp.tile(a_k_gain, A_HEADS),
                               jnp.tile(b_q_gain * scale, 2 * B_HEADS), jnp.tile(b_k_gain, 2 * B_HEADS)])[None, :]
    q_cols, k_dim, even = _qi_column_order()
    w_qi_p = w_qi[:, q_cols].astype(BF16)
    w_k_rep = w_ki[:, k_dim]
    zeros = jnp.zeros_like(w_k_rep)
    w_kw = jnp.concatenate([
        jnp.where(even[None, :], w_k_rep, zeros), jnp.where(even[None, :], zeros, w_k_rep),
        jnp.pad(w_wi, ((0, 0), (0, 2 * LANES - IDX_HEADS)))], axis=1).astype(BF16)
    w_vg = jnp.concatenate([w_va, w_vb, w_ga, w_gb, w_ma, w_mb], axis=1).astype(BF16)

    m = x2d.shape[0]
    tm, tn = 1024, 1024
    h = _norm(x2d, norm_gain[None, :], 512)

    tab_spec = pl.BlockSpec((tm, LANES), lambda i, j: (i, 0))
    qkr = _proj_qk(h, w_qk, gain_qk, cos_h, sin_h, tm, tn)
    qi = _proj(h, w_qi_p, (cos_i, sin_i), [tab_spec, tab_spec], _ep_qi,
               jax.ShapeDtypeStruct((IDX_HEADS // 2, m, LANES), BF16),
               pl.BlockSpec((tn // LANES, tm, LANES), lambda i, j: (j, i, 0)), tm, tn, "proj_qi")
    kab, wi = _proj(h, w_kw, (cos_i, sin_i), [tab_spec, tab_spec], _ep_kw,
                    (jax.ShapeDtypeStruct((2, m, LANES), BF16), jax.ShapeDtypeStruct((m, LANES), F32)),
                    (pl.BlockSpec((2, tm, LANES), lambda i, j: (0, i, 0)),
                     pl.BlockSpec((tm, LANES), lambda i, j: (i, 0))), tm, w_kw.shape[1], "proj_kw")
    n_plain = (A_WIDTH + B_WIDTH) // tn
    vg = _proj(h, w_vg, (), [], functools.partial(_ep_vg, n_plain=n_plain, n_silu=n_plain),
               jax.ShapeDtypeStruct((m, w_vg.shape[1]), BF16),
               pl.BlockSpec((tm, tn), lambda i, j: (i, j)), tm, tn, "proj_vg")

    def logit_bound(q_gain, k_gain):
        return HEAD_DIM * 1.02 * jnp.max(jnp.abs(q_gain * scale)) * jnp.max(jnp.abs(k_gain))

    def with_bound(bound, attend):
        return lax.cond(bound <= MAX_LOGIT_BOUND, lambda: attend(True), lambda: attend(False))

    topk = min(TOPK_MAX, seq // 4)
    oa = with_bound(logit_bound(a_q_gain, a_k_gain),
                    lambda f: _attn_a(qi, wi, kab, qkr, vg, batch, seq, 256, 1024, topk, f))
    lambda_init = 0.8 - 0.6 * math.exp(-0.3 * layer)
    ob = with_bound(logit_bound(b_q_gain, b_k_gain),
                    lambda f: _attn_b(qkr, vg, b_subln_gain[None, :], lq1[None, :], lk1[None, :], lq2[None, :],
                                      lk2[None, :], batch, seq, 512, 4, lambda_init, f))
    merged = _merge(oa, ob, w_o_a.astype(BF16), w_o_b.astype(BF16), vg, d_model, 1024, 512)
    return _out_proj(merged, w_out.astype(BF16), x2d, 1024, 512)


def kernel(x, positions, norm_gain, w_in, a_q_gain, a_k_gain, b_q_gain, b_k_gain, lambda_q1, lambda_k1,
           lambda_q2, lambda_k2, b_subln_gain, w_o_a, w_o_b, w_out):
    batch, seq, d_model = x.shape
    m = batch * seq
    pos = positions.astype(F32).reshape(m, 1)
    ang_h = pos * ROPE_THETA ** (-jnp.arange(0, HEAD_DIM, 2, dtype=F32) / HEAD_DIM)
    ang_i = pos * ROPE_THETA ** (-jnp.arange(0, IDX_DIM, 2, dtype=F32) / IDX_DIM)
    ch, sh, ci, si = jnp.cos(ang_h), jnp.sin(ang_h), jnp.cos(ang_i), jnp.sin(ang_i)
    tabs = (jnp.concatenate([ch, ch], axis=1), jnp.concatenate([-sh, sh], axis=1),
            jnp.concatenate([ci] * 4, axis=1), jnp.concatenate([-si, -si, si, si], axis=1))
    x2d = x.reshape(m, d_model)
    for layer in range(norm_gain.shape[0]):
        x2d = _layer(x2d, tabs, norm_gain[layer], w_in[layer], a_q_gain[layer], a_k_gain[layer],
                     b_q_gain[layer], b_k_gain[layer], lambda_q1[layer], lambda_k1[layer],
                     lambda_q2[layer], lambda_k2[layer], b_subln_gain[layer], w_o_a[layer], w_o_b[layer],
                     w_out[layer], layer, batch, seq)
    return x2d.reshape(batch, seq, d_model)
```

```python
import functools
import math

import jax
import jax.numpy as jnp
import numpy as np
from jax import lax
from jax.experimental import pallas as pl
from jax.experimental.pallas import tpu as pltpu

F32 = jnp.float32
BF16 = jnp.bfloat16

HEAD_DIM = 128
A_HEADS = 8
A_WIDTH = A_HEADS * HEAD_DIM
IDX_HEADS = 16
IDX_DIM = 64
TOPK_MAX = 256
B_HEADS = 4
B_WIDTH = B_HEADS * 2 * HEAD_DIM
ROPE_THETA = 10000.0
NORM_EPS = 1e-6
SUBLN_EPS = 1e-5

LANES = 128
INT_MIN = -(2 ** 31)
KEY_NEG_INF = INT_MIN + 0x7FFFFF
NEG_BIG = -1e30
M_INIT = -5e29
LOG2E = math.log2(math.e)
MAX_LOGIT_BOUND = 60.0
SELECT_ROWS = 64
SOFTMAX_ROWS = 32
PROJ_SUB = 256
VMEM_LIMIT_BYTES = 56 * 1024 * 1024


def _nt_dot(a, b):
    return lax.dot_general(a, b, (((1,), (1,)), ((), ())), preferred_element_type=F32)


def _tile_lanes(v, width):
    reps = width // LANES
    return v if reps == 1 else jnp.concatenate([v] * reps, axis=1)


def _params(semantics):
    return pltpu.CompilerParams(dimension_semantics=semantics, vmem_limit_bytes=VMEM_LIMIT_BYTES)


def _norm_kernel(x_ref, g_ref, h_ref):
    x = x_ref[...]
    ms = jnp.mean(x * x, axis=-1, keepdims=True)
    h_ref[...] = (x * lax.rsqrt(ms + NORM_EPS) * g_ref[...]).astype(BF16)


def _norm(x2d, gain, tm):
    m, d = x2d.shape
    return pl.pallas_call(
        _norm_kernel,
        grid=(m // tm,),
        in_specs=[pl.BlockSpec((tm, d), lambda i: (i, 0)), pl.BlockSpec((1, d), lambda i: (0, 0))],
        out_specs=pl.BlockSpec((tm, d), lambda i: (i, 0)),
        out_shape=jax.ShapeDtypeStruct((m, d), BF16),
        compiler_params=_params(("parallel",)),
        name="norm",
    )(x2d, gain)


def _proj_body(h_ref, w_ref, *rest, n_aux, epilogue):
    tn = w_ref.shape[1]

    def sub_dot(t):
        return jnp.dot(h_ref[...], w_ref[:, t * PROJ_SUB:(t + 1) * PROJ_SUB], preferred_element_type=F32)

    epilogue(sub_dot, tn // PROJ_SUB, rest[:n_aux], rest[n_aux:])


def _proj(h, w, aux, aux_specs, epilogue, out_shape, out_specs, tm, tn, name):
    m, k = h.shape
    n = w.shape[1]
    return pl.pallas_call(
        functools.partial(_proj_body, n_aux=len(aux), epilogue=epilogue),
        grid=(m // tm, n // tn),
        in_specs=[pl.BlockSpec((tm, k), lambda i, j: (i, 0)),
                  pl.BlockSpec((k, tn), lambda i, j: (0, j))] + aux_specs,
        out_specs=out_specs,
        out_shape=out_shape,
        compiler_params=_params(("parallel", "arbitrary")),
        name=name,
    )(h, w, *aux)


def _rope128(y, cos_full, sin_signed):
    return y * cos_full + pltpu.roll(y, 64, 1) * sin_signed


def _proj_qk_kernel(h_ref, w_ref, g_ref, c_ref, s_ref, o_ref, acc_ref):
    acc_ref[...] = jnp.dot(h_ref[...], w_ref[...], preferred_element_type=F32)
    same_head = (lax.broadcasted_iota(jnp.int32, (PROJ_SUB, PROJ_SUB), 0) // HEAD_DIM
                 == lax.broadcasted_iota(jnp.int32, (PROJ_SUB, PROJ_SUB), 1) // HEAD_DIM)
    head_mean = jnp.where(same_head, 1.0 / HEAD_DIM, 0.0).astype(BF16)

    def sub_tile(t, carry):
        cs = pl.ds(pl.multiple_of(t * PROJ_SUB, PROJ_SUB), PROJ_SUB)
        acc = acc_ref[:, cs]
        ms = jnp.dot((acc * acc).astype(BF16), head_mean, preferred_element_type=F32)
        y = acc * lax.rsqrt(ms + NORM_EPS) * g_ref[:, cs]
        c = c_ref[...]
        s = s_ref[...]
        for u in range(PROJ_SUB // LANES):
            ls = pl.ds(pl.multiple_of(t * PROJ_SUB + u * LANES, LANES), LANES)
            o_ref[:, ls] = _rope128(y[:, u * LANES:(u + 1) * LANES], c, s).astype(BF16)
        return carry

    lax.fori_loop(0, acc_ref.shape[1] // PROJ_SUB, sub_tile, 0)


def _proj_qk(h, w, gain, cos_full, sin_signed, tm, tn):
    m, k = h.shape
    n = w.shape[1]
    tab_spec = pl.BlockSpec((tm, LANES), lambda i, j: (i, 0))
    return pl.pallas_call(
        _proj_qk_kernel,
        grid=(m // tm, n // tn),
        in_specs=[pl.BlockSpec((tm, k), lambda i, j: (i, 0)),
                  pl.BlockSpec((k, tn), lambda i, j: (0, j)),
                  pl.BlockSpec((1, tn), lambda i, j: (0, j)), tab_spec, tab_spec],
        out_specs=pl.BlockSpec((tm, tn), lambda i, j: (i, j)),
        out_shape=jax.ShapeDtypeStruct((m, n), BF16),
        scratch_shapes=[pltpu.VMEM((tm, tn), F32)],
        compiler_params=_params(("parallel", "arbitrary")),
        name="proj_qk",
    )(h, w, gain, cos_full, sin_signed)


def _ep_qi(sub_dot, n_sub, aux, outs):
    c_ref, s_ref = aux
    (o_ref,) = outs
    c = c_ref[...]
    s = s_ref[...]
    per = PROJ_SUB // LANES
    for t in range(n_sub):
        acc = sub_dot(t)
        for u in range(per):
            o_ref[t * per + u] = _rope128(acc[:, u * LANES:(u + 1) * LANES], c, s).astype(BF16)


def _ep_kw(sub_dot, n_sub, aux, outs):
    c_ref, s_ref = aux
    kab_ref, w_ref = outs
    c = c_ref[...]
    s = s_ref[...]
    acc = sub_dot(0)
    for u in range(2):
        kab_ref[u] = _rope128(acc[:, u * LANES:(u + 1) * LANES], c, s).astype(BF16)
    w_ref[...] = sub_dot(1)[:, :LANES] * (IDX_HEADS ** -0.5 * IDX_DIM ** -0.5)


def _ep_vg(sub_dot, n_sub, aux, outs, *, n_plain, n_silu):
    (o_ref,) = outs
    j = pl.program_id(1)

    def store(fn):
        for t in range(n_sub):
            o_ref[:, t * PROJ_SUB:(t + 1) * PROJ_SUB] = fn(sub_dot(t)).astype(BF16)

    @pl.when(j < n_plain)
    def _():
        store(lambda a: a)

    @pl.when(jnp.logical_and(j >= n_plain, j < n_plain + n_silu))
    def _():
        store(lambda a: a * jax.nn.sigmoid(a))

    @pl.when(j >= n_plain + n_silu)
    def _():
        store(jax.nn.sigmoid)


def _f32_key(v):
    bits = pltpu.bitcast(v, jnp.int32)
    return bits ^ ((bits >> 31) & 0x7FFFFFFF)


def _softmax_rows(s_ref, p_ref, alpha_ref, m_ref, l_ref, adjust):
    tq, tk = s_ref.shape
    for r in range(tq // SOFTMAX_ROWS):
        rs = slice(r * SOFTMAX_ROWS, (r + 1) * SOFTMAX_ROWS)
        lanes = [slice(t * LANES, (t + 1) * LANES) for t in range(tk // LANES)]
        s = [adjust(s_ref[rs, ls], rs, ls) for ls in lanes]
        m_prev = m_ref[rs, :]
        m_new = jnp.maximum(m_prev, jnp.max(functools.reduce(jnp.maximum, s), axis=1, keepdims=True))
        alpha = jnp.exp2(m_prev - m_new)
        p = [jnp.exp2(x - m_new) for x in s]
        l_ref[rs, :] = alpha * l_ref[rs, :] + jnp.sum(functools.reduce(jnp.add, p), axis=1, keepdims=True)
        for ls, x in zip(lanes, p):
            p_ref[rs, ls] = x.astype(BF16)
        alpha_ref[rs, :] = alpha
        m_ref[rs, :] = m_new


def _softmax_rows_bounded(s_ref, p_ref, l_ref, adjust):
    tq, tk = s_ref.shape
    for r in range(tq // SOFTMAX_ROWS):
        rs = slice(r * SOFTMAX_ROWS, (r + 1) * SOFTMAX_ROWS)
        lanes = [slice(t * LANES, (t + 1) * LANES) for t in range(tk // LANES)]
        p = [jnp.exp2(adjust(s_ref[rs, ls], rs, ls)) for ls in lanes]
        l_ref[rs, :] = l_ref[rs, :] + jnp.sum(functools.reduce(jnp.add, p), axis=1, keepdims=True)
        for ls, x in zip(lanes, p):
            p_ref[rs, ls] = x.astype(BF16)


def _a_kernel(qt_ref, kt_ref, qi_ref, wi_ref, kab_ref, qa_ref, ka_ref, va_ref, ga_ref, o_ref,
              sc_ref, gmax_ref, tau_ref, lo_ref, hi_ref, cand_ref, done_ref, ext_ref, cnt_ref, bias_ref,
              s_ref, p_ref, alpha_ref, m_ref, l_ref, acc_ref, *, tq, tk, topk, rows, bounded):
    step = pl.program_id(1)
    iq = qt_ref[step]
    kj = kt_ref[step]
    last = ((iq + 1) * tq - 1) // tk

    @pl.when(kj == 0)
    def _index_and_select():
        w = wi_ref[...]
        row = iq * tq + lax.broadcasted_iota(jnp.int32, (tq, tk), 0)
        col0 = lax.broadcasted_iota(jnp.int32, (tq, tk), 1)
        gmax_ref[...] = jnp.full(gmax_ref.shape, -jnp.inf, F32)

        def score_tile(j, carry):
            start = pl.multiple_of(j * tk, tk)
            k_even = kab_ref[0, pl.ds(start, tk), :]
            k_odd = kab_ref[1, pl.ds(start, tk), :]
            acc = jnp.zeros((tq, tk), F32)
            for p in range(IDX_HEADS // 2):
                q_p = qi_ref[p]
                s_even = jnp.maximum(_nt_dot(q_p, k_even), 0.0)
                s_odd = jnp.maximum(_nt_dot(q_p, k_odd), 0.0)
                acc = acc + w[:, 2 * p:2 * p + 1] * s_even + w[:, 2 * p + 1:2 * p + 2] * s_odd
            causal = col0 + j * tk <= row
            sc_ref[j] = jnp.where(causal, _f32_key(acc), INT_MIN)
            accm = jnp.where(causal, acc, -jnp.inf)
            half = tk // 2
            for g in range(2):
                gm = gmax_ref[g]
                for t in range(half // LANES):
                    lo = g * half + t * LANES
                    gm = jnp.maximum(gm, accm[:, lo:lo + LANES])
                gmax_ref[g] = gm
            return carry

        lax.fori_loop(0, last + 1, score_tile, 0)

        g0 = gmax_ref[0]
        g1 = gmax_ref[1]
        row_max = jnp.max(jnp.maximum(g0, g1), axis=1, keepdims=True)
        row_low = jnp.min(jnp.minimum(g0, g1), axis=1, keepdims=True)
        lo0 = jnp.broadcast_to(_f32_key(row_low), (tq, LANES))
        hi0 = jnp.broadcast_to(_f32_key(row_max), (tq, LANES)) + 1
        lo_ref[...] = lo0
        hi_ref[...] = hi0
        done_ref[...] = jnp.where(hi0 - lo0 == 1, 1.0, 0.0)
        ext_ref[...] = jnp.zeros((tq, LANES), F32)

        def row_pass(ref, body, init, finish, needed):
            chunks = [pl.ds(rc * rows, rows) for rc in range(tq // rows)]
            flags = [needed(rs) for rs in chunks]
            for rs, flag in zip(chunks, flags):

                @pl.when(flag)
                def _(rs=rs):
                    bound = ref[rs, :]

                    def tile_step(j, carry):
                        keys = sc_ref[j, rs, :]
                        for t in range(tk // LANES):
                            carry = body(carry, keys[:, t * LANES:(t + 1) * LANES], bound)
                        return carry

                    cnt_ref[rs, :] = finish(lax.fori_loop(0, last + 1, tile_step, init))

        def search_cond(c):
            n, stop = c
            return jnp.logical_and(n < 2 * 32 + 2, stop == 0)

        def search_step(c):
            n, _ = c
            stop = (jnp.min(done_ref[...]) > 0.0).astype(jnp.int32)
            lo = lo_ref[...]
            hi = hi_ref[...]
            cand_ref[...] = (lo >> 1) + (hi >> 1) + (lo & hi & 1)
            row_pass(cand_ref, lambda acc, keys, cand: acc + jnp.where(keys >= cand, 1.0, 0.0),
                     jnp.zeros((rows, LANES), F32), lambda acc: acc,
                     lambda rs: jnp.min(done_ref[rs, :]) == 0.0)
            cnt = jnp.sum(cnt_ref[...], axis=1, keepdims=True)
            cand = cand_ref[...]
            active = done_ref[...] == 0.0
            enough = cnt >= float(topk)
            lo = jnp.where(jnp.logical_and(active, enough), cand, lo)
            hi = jnp.where(jnp.logical_and(active, jnp.logical_not(enough)), cand, hi)
            lo_ref[...] = lo
            hi_ref[...] = hi
            extract = jnp.logical_and(active, cnt == float(topk - 1))
            ext_ref[...] = jnp.where(extract, 1.0, ext_ref[...])
            finished = jnp.logical_or(jnp.logical_or(cnt == float(topk), extract), hi - lo == 1)
            done_ref[...] = jnp.where(jnp.logical_and(active, finished), 1.0, done_ref[...])
            return n + 1, stop

        lax.while_loop(search_cond, search_step, (jnp.int32(0), jnp.int32(0)))

        def key_to_f32(keys):
            return pltpu.bitcast(keys ^ ((keys >> 31) & 0x7FFFFFFF), F32)

        row_pass(hi_ref, lambda acc, keys, hi: jnp.maximum(acc, jnp.where(keys < hi, keys, KEY_NEG_INF)),
                 jnp.full((rows, LANES), KEY_NEG_INF, jnp.int32), key_to_f32,
                 lambda rs: jnp.max(ext_ref[rs, :]) > 0.0)
        below = jnp.broadcast_to(_f32_key(jnp.max(cnt_ref[...], axis=1, keepdims=True)), (tq, LANES))
        thr = jnp.where(ext_ref[...] > 0.0, below, lo_ref[...])
        tau_ref[...] = jnp.maximum(thr, INT_MIN + 1)

        m_ref[...] = jnp.full(m_ref.shape, M_INIT, F32)
        l_ref[...] = jnp.zeros(l_ref.shape, F32)
        acc_ref[...] = jnp.zeros(acc_ref.shape, F32)

    tau = tau_ref[...]
    for t in range(tk // LANES):
        ls = slice(t * LANES, (t + 1) * LANES)
        bias_ref[:, ls] = jnp.where(sc_ref[kj, :, ls] >= tau, 0.0, NEG_BIG)

    def add_bias(x, rs, ls):
        return x + bias_ref[rs, ls]

    heads = [slice(h * HEAD_DIM, (h + 1) * HEAD_DIM) for h in range(A_HEADS)]
    for h, hs in enumerate(heads):
        s_ref[h] = _nt_dot(qa_ref[:, hs], ka_ref[:, hs])
    if bounded:
        for h in range(A_HEADS):
            _softmax_rows_bounded(s_ref.at[h], p_ref.at[h], l_ref.at[h], add_bias)
        for h, hs in enumerate(heads):
            acc_ref[h] = acc_ref[h] + jnp.dot(p_ref[h], va_ref[:, hs], preferred_element_type=F32)
    else:
        for h in range(A_HEADS):
            _softmax_rows(s_ref.at[h], p_ref.at[h], alpha_ref.at[h], m_ref.at[h], l_ref.at[h], add_bias)
        for h, hs in enumerate(heads):
            acc_ref[h] = alpha_ref[h] * acc_ref[h] + jnp.dot(p_ref[h], va_ref[:, hs],
                                                             preferred_element_type=F32)

    @pl.when(kj == last)
    def _finish():
        for h in range(A_HEADS):
            hs = slice(h * HEAD_DIM, (h + 1) * HEAD_DIM)
            o = acc_ref[h] / l_ref[h]
            o_ref[:, hs] = (o * ga_ref[:, hs].astype(F32)).astype(BF16)


def _causal_steps(nq, tq, tk):
    qt, kt = [], []
    for i in range(nq):
        for j in range(((i + 1) * tq - 1) // tk + 1):
            qt.append(i)
            kt.append(j)
    return jnp.asarray(qt, jnp.int32), jnp.asarray(kt, jnp.int32)


def _attn_a(qi, wi, kab, qkr, vg, batch, seq, tq, tk, topk, bounded):
    m = batch * seq
    nq = seq // tq
    nk = seq // tk
    qt, kt = _causal_steps(nq, tq, tk)

    def qrow(b, s, qt_ref, kt_ref):
        return b * nq + qt_ref[s]

    def krow(b, s, qt_ref, kt_ref):
        return b * nk + kt_ref[s]

    grid_spec = pltpu.PrefetchScalarGridSpec(
        num_scalar_prefetch=2,
        grid=(batch, int(qt.shape[0])),
        in_specs=[
            pl.BlockSpec((IDX_HEADS // 2, tq, LANES), lambda b, s, q, k: (0, qrow(b, s, q, k), 0)),
            pl.BlockSpec((tq, LANES), lambda b, s, q, k: (qrow(b, s, q, k), 0)),
            pl.BlockSpec((2, seq, LANES), lambda b, s, q, k: (0, b, 0),
                         pipeline_mode=pl.Buffered(1)),
            pl.BlockSpec((tq, A_WIDTH), lambda b, s, q, k: (qrow(b, s, q, k), 0)),
            pl.BlockSpec((tk, A_WIDTH), lambda b, s, q, k: (krow(b, s, q, k), 1)),
            pl.BlockSpec((tk, A_WIDTH), lambda b, s, q, k: (krow(b, s, q, k), 0)),
            pl.BlockSpec((tq, A_WIDTH), lambda b, s, q, k: (qrow(b, s, q, k), 2)),
        ],
        out_specs=pl.BlockSpec((tq, A_WIDTH), lambda b, s, q, k: (qrow(b, s, q, k), 0)),
        scratch_shapes=[
            pltpu.VMEM((nk, tq, tk), jnp.int32),
            pltpu.VMEM((2, tq, LANES), F32),
            pltpu.VMEM((tq, LANES), jnp.int32),
            pltpu.VMEM((tq, LANES), jnp.int32),
            pltpu.VMEM((tq, LANES), jnp.int32),
            pltpu.VMEM((tq, LANES), jnp.int32),
            pltpu.VMEM((tq, LANES), F32),
            pltpu.VMEM((tq, LANES), F32),
            pltpu.VMEM((tq, LANES), F32),
            pltpu.VMEM((tq, tk), F32),
            pltpu.VMEM((A_HEADS, tq, tk), F32),
            pltpu.VMEM((A_HEADS, tq, tk), BF16),
            pltpu.VMEM((A_HEADS, tq, LANES), F32),
            pltpu.VMEM((A_HEADS, tq, LANES), F32),
            pltpu.VMEM((A_HEADS, tq, LANES), F32),
            pltpu.VMEM((A_HEADS, tq, HEAD_DIM), F32),
        ],
    )
    return pl.pallas_call(
        functools.partial(_a_kernel, tq=tq, tk=tk, topk=topk, rows=SELECT_ROWS, bounded=bounded),
        grid_spec=grid_spec,
        out_shape=jax.ShapeDtypeStruct((m, A_WIDTH), BF16),
        compiler_params=_params(("parallel", "arbitrary")),
        name="attn_a",
    )(qt, kt, qi, wi, kab, qkr, qkr, vg, vg)


def _b_kernel(qt_ref, kt_ref, q_ref, k_ref, v_ref, sg_ref, gain_ref, lq1_ref, lk1_ref, lq2_ref, lk2_ref,
              o_ref, s_ref, p_ref, alpha_ref, m_ref, l_ref, acc_ref, *, tq, heads, lambda_init, bounded):
    step = pl.program_id(2)
    iq = qt_ref[step]
    kj = kt_ref[step]
    hw = 2 * HEAD_DIM
    chains = [(hh, c) for hh in range(heads) for c in range(2)]

    @pl.when(kj == 0)
    def _init():
        m_ref[...] = jnp.full(m_ref.shape, M_INIT, F32)
        l_ref[...] = jnp.zeros(l_ref.shape, F32)
        acc_ref[...] = jnp.zeros(acc_ref.shape, F32)

    def causal_mask(x, rs, ls):
        row = rs.start + lax.broadcasted_iota(jnp.int32, x.shape, 0)
        col = ls.start + lax.broadcasted_iota(jnp.int32, x.shape, 1)
        return jnp.where(col <= row, x, NEG_BIG)

    def attend(diagonal):
        adjust = causal_mask if diagonal else (lambda x, rs, ls: x)
        for i, (hh, c) in enumerate(chains):
            cs = slice(hh * hw + c * HEAD_DIM, hh * hw + (c + 1) * HEAD_DIM)
            s_ref[i] = _nt_dot(q_ref[:, cs], k_ref[:, cs])
        if bounded:
            for i in range(len(chains)):
                _softmax_rows_bounded(s_ref.at[i], p_ref.at[i], l_ref.at[i], adjust)
            for i, (hh, c) in enumerate(chains):
                acc_ref[i] = acc_ref[i] + jnp.dot(p_ref[i], v_ref[:, hh * hw:(hh + 1) * hw],
                                                  preferred_element_type=F32)
        else:
            for i in range(len(chains)):
                _softmax_rows(s_ref.at[i], p_ref.at[i], alpha_ref.at[i], m_ref.at[i], l_ref.at[i], adjust)
            for i, (hh, c) in enumerate(chains):
                pv = jnp.dot(p_ref[i], v_ref[:, hh * hw:(hh + 1) * hw], preferred_element_type=F32)
                alpha = alpha_ref[i]
                for t in range(hw // LANES):
                    ls = slice(t * LANES, (t + 1) * LANES)
                    acc_ref[i, :, ls] = alpha * acc_ref[i, :, ls] + pv[:, ls]

    @pl.when(kj < iq)
    def _below_diagonal():
        attend(False)

    @pl.when(kj == iq)
    def _diagonal_and_finish():
        attend(True)
        lam = (jnp.exp(jnp.sum(lq1_ref[...] * lk1_ref[...], axis=1, keepdims=True))
               - jnp.exp(jnp.sum(lq2_ref[...] * lk2_ref[...], axis=1, keepdims=True))
               + lambda_init)
        for hh in range(heads):
            o0 = acc_ref[2 * hh] / _tile_lanes(l_ref[2 * hh], hw)
            o1 = acc_ref[2 * hh + 1] / _tile_lanes(l_ref[2 * hh + 1], hw)
            o = o0 - lam * o1
            ms = jnp.mean(o * o, axis=-1, keepdims=True)
            y = o * lax.rsqrt(ms + SUBLN_EPS) * gain_ref[...] * (1.0 - lambda_init)
            hs = slice(hh * hw, (hh + 1) * hw)
            o_ref[:, hs] = (y * sg_ref[:, hs].astype(F32)).astype(BF16)


def _attn_b(qkr, vg, gain, lq1, lk1, lq2, lk2, batch, seq, tq, heads, lambda_init, bounded):
    m = batch * seq
    nq = seq // tq
    bw = heads * 2 * HEAD_DIM
    groups = B_HEADS // heads
    chains = 2 * heads
    qt, kt = _causal_steps(nq, tq, tq)

    def qrow(b, s, qt_ref):
        return b * nq + qt_ref[s]

    vec = pl.BlockSpec((1, HEAD_DIM), lambda b, h, s, q, k: (0, 0))
    grid_spec = pltpu.PrefetchScalarGridSpec(
        num_scalar_prefetch=2,
        grid=(batch, groups, int(qt.shape[0])),
        in_specs=[
            pl.BlockSpec((tq, bw), lambda b, h, s, q, k: (qrow(b, s, q), 2 * groups + h)),
            pl.BlockSpec((tq, bw), lambda b, h, s, q, k: (qrow(b, s, k), 3 * groups + h)),
            pl.BlockSpec((tq, bw), lambda b, h, s, q, k: (qrow(b, s, k), groups + h)),
            pl.BlockSpec((tq, bw), lambda b, h, s, q, k: (qrow(b, s, q), 3 * groups + h)),
            pl.BlockSpec((1, 2 * HEAD_DIM), lambda b, h, s, q, k: (0, 0)),
            vec, vec, vec, vec,
        ],
        out_specs=pl.BlockSpec((tq, bw), lambda b, h, s, q, k: (qrow(b, s, q), h)),
        scratch_shapes=[
            pltpu.VMEM((chains, tq, tq), F32),
            pltpu.VMEM((chains, tq, tq), BF16),
            pltpu.VMEM((chains, tq, LANES), F32),
            pltpu.VMEM((chains, tq, LANES), F32),
            pltpu.VMEM((chains, tq, LANES), F32),
            pltpu.VMEM((chains, tq, 2 * HEAD_DIM), F32),
        ],
    )
    return pl.pallas_call(
        functools.partial(_b_kernel, tq=tq, heads=heads, lambda_init=lambda_init, bounded=bounded),
        grid_spec=grid_spec,
        out_shape=jax.ShapeDtypeStruct((m, B_WIDTH), BF16),
        compiler_params=_params(("parallel", "parallel", "arbitrary")),
        name="attn_b",
    )(qt, kt, qkr, qkr, vg, vg, gain, lq1, lk1, lq2, lk2)


def _merge_kernel(oa_ref, ob_ref, wa_ref, wb_ref, ma_ref, mb_ref, o_ref):
    ya = jnp.dot(oa_ref[...], wa_ref[...], preferred_element_type=F32)
    yb = jnp.dot(ob_ref[...], wb_ref[...], preferred_element_type=F32)
    o_ref[...] = (ma_ref[...].astype(F32) * ya + mb_ref[...].astype(F32) * yb).astype(BF16)


def _merge(oa, ob, w_a, w_b, vg, d_model, tm, tn):
    m = oa.shape[0]
    gate_a0 = (A_WIDTH + B_WIDTH) * 2 // tn
    gate_b0 = gate_a0 + d_model // tn
    return pl.pallas_call(
        _merge_kernel,
        grid=(m // tm, d_model // tn),
        in_specs=[
            pl.BlockSpec((tm, A_WIDTH), lambda i, j: (i, 0)),
            pl.BlockSpec((tm, B_WIDTH), lambda i, j: (i, 0)),
            pl.BlockSpec((A_WIDTH, tn), lambda i, j: (0, j)),
            pl.BlockSpec((B_WIDTH, tn), lambda i, j: (0, j)),
            pl.BlockSpec((tm, tn), lambda i, j: (i, gate_a0 + j)),
            pl.BlockSpec((tm, tn), lambda i, j: (i, gate_b0 + j)),
        ],
        out_specs=pl.BlockSpec((tm, tn), lambda i, j: (i, j)),
        out_shape=jax.ShapeDtypeStruct((m, d_model), BF16),
        compiler_params=_params(("parallel", "arbitrary")),
        name="merge",
    )(oa, ob, w_a, w_b, vg, vg)


def _out_kernel(mg_ref, w_ref, x_ref, o_ref):
    o_ref[...] = x_ref[...] + jnp.dot(mg_ref[...], w_ref[...], preferred_element_type=F32)


def _out_proj(merged, w_out, x2d, tm, tn):
    m, d = x2d.shape
    return pl.pallas_call(
        _out_kernel,
        grid=(m // tm, d // tn),
        in_specs=[
            pl.BlockSpec((tm, d), lambda i, j: (i, 0)),
            pl.BlockSpec((d, tn), lambda i, j: (0, j)),
            pl.BlockSpec((tm, tn), lambda i, j: (i, j)),
        ],
        out_specs=pl.BlockSpec((tm, tn), lambda i, j: (i, j)),
        out_shape=jax.ShapeDtypeStruct((m, d), F32),
        compiler_params=_params(("parallel", "arbitrary")),
        name="out_proj",
    )(merged, w_out, x2d)


def _qi_column_order():
    lane = np.arange(LANES)
    quarter, r = lane // 32, lane % 32
    dim = r + 32 * (quarter // 2)
    q_cols = np.concatenate([(2 * p + quarter % 2) * IDX_DIM + dim for p in range(IDX_HEADS // 2)])
    return q_cols, dim, (quarter % 2 == 0)


def _layer(x2d, tabs, norm_gain, w_in, a_q_gain, a_k_gain, b_q_gain, b_k_gain,
           lq1, lk1, lq2, lk2, b_subln_gain, w_o_a, w_o_b, w_out, layer, batch, seq):
    d_model = x2d.shape[1]
    cos_h, sin_h, cos_i, sin_i = tabs
    sizes = (A_WIDTH,) * 4 + (IDX_HEADS * IDX_DIM, IDX_DIM, IDX_HEADS) + (B_WIDTH,) * 4 + (d_model,) * 2
    off = np.concatenate([[0], np.cumsum(sizes)])
    w_bf = w_in.astype(BF16)
    seg = lambda i: w_bf[:, off[i]:off[i + 1]]
    (w_qa, w_ka, w_va, w_ga, w_qi, w_ki, w_wi, w_qb, w_kb, w_vb, w_gb, w_ma, w_mb) = [seg(i) for i in range(13)]

    scale = HEAD_DIM ** -0.5 * LOG2E
    w_qk = jnp.concatenate([w_qa, w_ka, w_qb, w_kb], axis=1)
    gain_qk = jnp.concatenate([jnp.tile(a_q_gain * scale, A_HEADS), jnp.tile(a_k_gain, A_HEADS),
                               jnp.tile(b_q_gain * scale, 2 * B_HEADS), jnp.tile(b_k_gain, 2 * B_HEADS)])[None, :]
    q_cols, k_dim, even = _qi_column_order()
    w_qi_p = w_qi[:, q_cols]
    w_k_rep = w_ki[:, k_dim]
    zeros = jnp.zeros_like(w_k_rep)
    w_kw = jnp.concatenate([
        jnp.where(even[None, :], w_k_rep, zeros), jnp.where(even[None, :], zeros, w_k_rep),
        jnp.pad(w_wi, ((0, 0), (0, 2 * LANES - IDX_HEADS)))], axis=1)
    w_vg = jnp.concatenate([w_va, w_vb, w_ga, w_gb, w_ma, w_mb], axis=1)

    m = x2d.shape[0]
    tm, tn = 1024, 1024
    h = _norm(x2d, norm_gain[None, :], 512)

    tab_spec = pl.BlockSpec((tm, LANES), lambda i, j: (i, 0))
    qkr = _proj_qk(h, w_qk, gain_qk, cos_h, sin_h, tm, tn)
    qi = _proj(h, w_qi_p, (cos_i, sin_i), [tab_spec, tab_spec], _ep_qi,
               jax.ShapeDtypeStruct((IDX_HEADS // 2, m, LANES), BF16),
               pl.BlockSpec((tn // LANES, tm, LANES), lambda i, j: (j, i, 0)), tm, tn, "proj_qi")
    kab, wi = _proj(h, w_kw, (cos_i, sin_i), [tab_spec, tab_spec], _ep_kw,
                    (jax.ShapeDtypeStruct((2, m, LANES), BF16), jax.ShapeDtypeStruct((m, LANES), F32)),
                    (pl.BlockSpec((2, tm, LANES), lambda i, j: (0, i, 0)),
                     pl.BlockSpec((tm, LANES), lambda i, j: (i, 0))), tm, w_kw.shape[1], "proj_kw")
    n_plain = (A_WIDTH + B_WIDTH) // tn
    vg = _proj(h, w_vg, (), [], functools.partial(_ep_vg, n_plain=n_plain, n_silu=n_plain),
               jax.ShapeDtypeStruct((m, w_vg.shape[1]), BF16),
               pl.BlockSpec((tm, tn), lambda i, j: (i, j)), tm, tn, "proj_vg")

    def logit_bound(q_gain, k_gain):
        return HEAD_DIM * 1.02 * jnp.max(jnp.abs(q_gain * scale)) * jnp.max(jnp.abs(k_gain))

    def with_bound(bound, attend):
        return lax.cond(bound <= MAX_LOGIT_BOUND, lambda: attend(True), lambda: attend(False))

    topk = min(TOPK_MAX, seq // 4)
    oa = with_bound(logit_bound(a_q_gain, a_k_gain),
                    lambda f: _attn_a(qi, wi, kab, qkr, vg, batch, seq, 256, 1024, topk, f))
    lambda_init = 0.8 - 0.6 * math.exp(-0.3 * layer)
    ob = with_bound(logit_bound(b_q_gain, b_k_gain),
                    lambda f: _attn_b(qkr, vg, b_subln_gain[None, :], lq1[None, :], lk1[None, :], lq2[None, :],
                                      lk2[None, :], batch, seq, 512, 4, lambda_init, f))
    merged = _merge(oa, ob, w_o_a.astype(BF16), w_o_b.astype(BF16), vg, d_model, 1024, 512)
    return _out_proj(merged, w_out.astype(BF16), x2d, 1024, 512)


def kernel(x, positions, norm_gain, w_in, a_q_gain, a_k_gain, b_q_gain, b_k_gain, lambda_q1, lambda_k1,
           lambda_q2, lambda_k2, b_subln_gain, w_o_a, w_o_b, w_out):
    batch, seq, d_model = x.shape
    m = batch * seq
    pos = positions.astype(F32).reshape(m, 1)
    ang_h = pos * ROPE_THETA ** (-jnp.arange(0, HEAD_DIM, 2, dtype=F32) / HEAD_DIM)
    ang_i = pos * ROPE_THETA ** (-jnp.arange(0, IDX_DIM, 2, dtype=F32) / IDX_DIM)
    ch, sh, ci, si = jnp.cos(ang_h), jnp.sin(ang_h), jnp.cos(ang_i), jnp.sin(ang_i)
    tabs = (jnp.concatenate([ch, ch], axis=1), jnp.concatenate([-sh, sh], axis=1),
            jnp.concatenate([ci] * 4, axis=1), jnp.concatenate([-si, -si, si, si], axis=1))
    x2d = x.reshape(m, d_model)
    for layer in range(norm_gain.shape[0]):
        x2d = _layer(x2d, tabs, norm_gain[layer], w_in[layer], a_q_gain[layer], a_k_gain[layer],
                     b_q_gain[layer], b_k_gain[layer], lambda_q1[layer], lambda_k1[layer],
                     lambda_q2[layer], lambda_k2[layer], b_subln_gain[layer], w_o_a[layer], w_o_b[layer],
                     w_out[layer], layer, batch, seq)
    return x2d.reshape(batch, seq, d_model)
```

```python
import functools
import math

import jax
import jax.numpy as jnp
import numpy as np
from jax import lax
from jax.experimental import pallas as pl
from jax.experimental.pallas import tpu as pltpu

F32 = jnp.float32
BF16 = jnp.bfloat16

HEAD_DIM = 128
A_HEADS = 8
A_WIDTH = A_HEADS * HEAD_DIM
IDX_HEADS = 16
IDX_DIM = 64
TOPK_MAX = 256
B_HEADS = 4
B_WIDTH = B_HEADS * 2 * HEAD_DIM
ROPE_THETA = 10000.0
NORM_EPS = 1e-6
SUBLN_EPS = 1e-5

LANES = 128
INT_MIN = -(2 ** 31)
KEY_NEG_INF = INT_MIN + 0x7FFFFF
NEG_BIG = -1e30
M_INIT = -5e29
LOG2E = math.log2(math.e)
MAX_LOGIT_BOUND = 60.0
SELECT_ROWS = 64
SOFTMAX_ROWS = 32
PROJ_SUB = 256
VMEM_LIMIT_BYTES = 56 * 1024 * 1024

NORM_ROWS = 512
PROJ_TILE = (1024, 1024)
ATTN_A_TILE = (256, 1024)
ATTN_B_TILE = 512
OUT_TILE = (1024, 512)


def _nt_dot(a, b):
    return lax.dot_general(a, b, (((1,), (1,)), ((), ())), preferred_element_type=F32)


def _tile_lanes(v, width):
    reps = width // LANES
    return v if reps == 1 else jnp.concatenate([v] * reps, axis=1)


def _params(semantics):
    return pltpu.CompilerParams(dimension_semantics=semantics, vmem_limit_bytes=VMEM_LIMIT_BYTES)


def _norm_kernel(x_ref, g_ref, h_ref):
    x = x_ref[...]
    ms = jnp.mean(x * x, axis=-1, keepdims=True)
    h_ref[...] = (x * lax.rsqrt(ms + NORM_EPS) * g_ref[...]).astype(BF16)


def _norm(x2d, gain, tm):
    m, d = x2d.shape
    return pl.pallas_call(
        _norm_kernel,
        grid=(m // tm,),
        in_specs=[pl.BlockSpec((tm, d), lambda i: (i, 0)), pl.BlockSpec((1, d), lambda i: (0, 0))],
        out_specs=pl.BlockSpec((tm, d), lambda i: (i, 0)),
        out_shape=jax.ShapeDtypeStruct((m, d), BF16),
        compiler_params=_params(("parallel",)),
        name="norm",
    )(x2d, gain)


def _proj_body(h_ref, w_ref, *rest, n_aux, epilogue):
    tn = w_ref.shape[1]

    def sub_dot(t):
        return jnp.dot(h_ref[...], w_ref[:, t * PROJ_SUB:(t + 1) * PROJ_SUB], preferred_element_type=F32)

    epilogue(sub_dot, tn // PROJ_SUB, rest[:n_aux], rest[n_aux:])


def _proj(h, w, aux, aux_specs, epilogue, out_shape, out_specs, tm, tn, name):
    m, k = h.shape
    n = w.shape[1]
    return pl.pallas_call(
        functools.partial(_proj_body, n_aux=len(aux), epilogue=epilogue),
        grid=(m // tm, n // tn),
        in_specs=[pl.BlockSpec((tm, k), lambda i, j: (i, 0)),
                  pl.BlockSpec((k, tn), lambda i, j: (0, j))] + aux_specs,
        out_specs=out_specs,
        out_shape=out_shape,
        compiler_params=_params(("parallel", "arbitrary")),
        name=name,
    )(h, w, *aux)


def _rope128(y, cos_full, sin_signed):
    return y * cos_full + pltpu.roll(y, 64, 1) * sin_signed


def _proj_qk_kernel(h_ref, w_ref, g_ref, c_ref, s_ref, o_ref, acc_ref):
    acc_ref[...] = jnp.dot(h_ref[...], w_ref[...], preferred_element_type=F32)
    same_head = (lax.broadcasted_iota(jnp.int32, (PROJ_SUB, PROJ_SUB), 0) // HEAD_DIM
                 == lax.broadcasted_iota(jnp.int32, (PROJ_SUB, PROJ_SUB), 1) // HEAD_DIM)
    head_mean = jnp.where(same_head, 1.0 / HEAD_DIM, 0.0).astype(BF16)

    def sub_tile(t, carry):
        cs = pl.ds(pl.multiple_of(t * PROJ_SUB, PROJ_SUB), PROJ_SUB)
        acc = acc_ref[:, cs]
        ms = jnp.dot((acc * acc).astype(BF16), head_mean, preferred_element_type=F32)
        y = acc * lax.rsqrt(ms + NORM_EPS) * g_ref[:, cs]
        c = c_ref[...]
        s = s_ref[...]
        for u in range(PROJ_SUB // LANES):
            ls = pl.ds(pl.multiple_of(t * PROJ_SUB + u * LANES, LANES), LANES)
            o_ref[:, ls] = _rope128(y[:, u * LANES:(u + 1) * LANES], c, s).astype(BF16)
        return carry

    lax.fori_loop(0, acc_ref.shape[1] // PROJ_SUB, sub_tile, 0)


def _proj_qk(h, w, gain, cos_full, sin_signed, tm, tn):
    m, k = h.shape
    n = w.shape[1]
    tab_spec = pl.BlockSpec((tm, LANES), lambda i, j: (i, 0))
    return pl.pallas_call(
        _proj_qk_kernel,
        grid=(m // tm, n // tn),
        in_specs=[pl.BlockSpec((tm, k), lambda i, j: (i, 0)),
                  pl.BlockSpec((k, tn), lambda i, j: (0, j)),
                  pl.BlockSpec((1, tn), lambda i, j: (0, j)), tab_spec, tab_spec],
        out_specs=pl.BlockSpec((tm, tn), lambda i, j: (i, j)),
        out_shape=jax.ShapeDtypeStruct((m, n), BF16),
        scratch_shapes=[pltpu.VMEM((tm, tn), F32)],
        compiler_params=_params(("parallel", "arbitrary")),
        name="proj_qk",
    )(h, w, gain, cos_full, sin_signed)


def _ep_qi(sub_dot, n_sub, aux, outs):
    c_ref, s_ref = aux
    (o_ref,) = outs
    c = c_ref[...]
    s = s_ref[...]
    per = PROJ_SUB // LANES
    for t in range(n_sub):
        acc = sub_dot(t)
        for u in range(per):
            o_ref[t * per + u] = _rope128(acc[:, u * LANES:(u + 1) * LANES], c, s).astype(BF16)


def _ep_kw(sub_dot, n_sub, aux, outs):
    c_ref, s_ref = aux
    kab_ref, w_ref = outs
    c = c_ref[...]
    s = s_ref[...]
    acc = sub_dot(0)
    for u in range(2):
        kab_ref[u] = _rope128(acc[:, u * LANES:(u + 1) * LANES], c, s).astype(BF16)
    w_ref[...] = sub_dot(1)[:, :LANES] * (IDX_HEADS ** -0.5 * IDX_DIM ** -0.5)


def _ep_vg(sub_dot, n_sub, aux, outs, *, n_plain, n_silu):
    (o_ref,) = outs
    j = pl.program_id(1)

    def store(fn):
        for t in range(n_sub):
            o_ref[:, t * PROJ_SUB:(t + 1) * PROJ_SUB] = fn(sub_dot(t)).astype(BF16)

    @pl.when(j < n_plain)
    def _():
        store(lambda a: a)

    @pl.when(jnp.logical_and(j >= n_plain, j < n_plain + n_silu))
    def _():
        store(lambda a: a * jax.nn.sigmoid(a))

    @pl.when(j >= n_plain + n_silu)
    def _():
        store(jax.nn.sigmoid)


def _f32_key(v):
    bits = pltpu.bitcast(v, jnp.int32)
    return bits ^ ((bits >> 31) & 0x7FFFFFFF)


def _softmax_rows(s_ref, p_ref, alpha_ref, m_ref, l_ref, adjust):
    tq, tk = s_ref.shape
    for r in range(tq // SOFTMAX_ROWS):
        rs = slice(r * SOFTMAX_ROWS, (r + 1) * SOFTMAX_ROWS)
        lanes = [slice(t * LANES, (t + 1) * LANES) for t in range(tk // LANES)]
        s = [adjust(s_ref[rs, ls], rs, ls) for ls in lanes]
        m_prev = m_ref[rs, :]
        m_new = jnp.maximum(m_prev, jnp.max(functools.reduce(jnp.maximum, s), axis=1, keepdims=True))
        alpha = jnp.exp2(m_prev - m_new)
        p = [jnp.exp2(x - m_new) for x in s]
        l_ref[rs, :] = alpha * l_ref[rs, :] + jnp.sum(functools.reduce(jnp.add, p), axis=1, keepdims=True)
        for ls, x in zip(lanes, p):
            p_ref[rs, ls] = x.astype(BF16)
        alpha_ref[rs, :] = alpha
        m_ref[rs, :] = m_new


def _softmax_rows_bounded(s_ref, p_ref, l_ref, adjust):
    tq, tk = s_ref.shape
    for r in range(tq // SOFTMAX_ROWS):
        rs = slice(r * SOFTMAX_ROWS, (r + 1) * SOFTMAX_ROWS)
        lanes = [slice(t * LANES, (t + 1) * LANES) for t in range(tk // LANES)]
        p = [jnp.exp2(adjust(s_ref[rs, ls], rs, ls)) for ls in lanes]
        l_ref[rs, :] = l_ref[rs, :] + jnp.sum(functools.reduce(jnp.add, p), axis=1, keepdims=True)
        for ls, x in zip(lanes, p):
            p_ref[rs, ls] = x.astype(BF16)


def _a_kernel(qt_ref, kt_ref, qi_ref, wi_ref, kab_ref, qa_ref, ka_ref, va_ref, ga_ref, o_ref,
              sc_ref, gmax_ref, tau_ref, lo_ref, hi_ref, cand_ref, done_ref, ext_ref, cnt_ref, bias_ref,
              s_ref, p_ref, alpha_ref, m_ref, l_ref, acc_ref, *, tq, tk, topk, rows, bounded):
    step = pl.program_id(1)
    iq = qt_ref[step]
    kj = kt_ref[step]
    last = ((iq + 1) * tq - 1) // tk

    @pl.when(kj == 0)
    def _index_and_select():
        w = wi_ref[...]
        row = iq * tq + lax.broadcasted_iota(jnp.int32, (tq, tk), 0)
        col0 = lax.broadcasted_iota(jnp.int32, (tq, tk), 1)
        gmax_ref[...] = jnp.full(gmax_ref.shape, -jnp.inf, F32)

        def score_tile(j, carry):
            start = pl.multiple_of(j * tk, tk)
            k_even = kab_ref[0, pl.ds(start, tk), :]
            k_odd = kab_ref[1, pl.ds(start, tk), :]
            acc = jnp.zeros((tq, tk), F32)
            for p in range(IDX_HEADS // 2):
                q_p = qi_ref[p]
                s_even = jnp.maximum(_nt_dot(q_p, k_even), 0.0)
                s_odd = jnp.maximum(_nt_dot(q_p, k_odd), 0.0)
                acc = acc + w[:, 2 * p:2 * p + 1] * s_even + w[:, 2 * p + 1:2 * p + 2] * s_odd
            causal = col0 + j * tk <= row
            sc_ref[j] = jnp.where(causal, _f32_key(acc), INT_MIN)
            accm = jnp.where(causal, acc, -jnp.inf)
            half = tk // 2
            for g in range(2):
                gm = gmax_ref[g]
                for t in range(half // LANES):
                    lo = g * half + t * LANES
                    gm = jnp.maximum(gm, accm[:, lo:lo + LANES])
                gmax_ref[g] = gm
            return carry

        lax.fori_loop(0, last + 1, score_tile, 0)

        g0 = gmax_ref[0]
        g1 = gmax_ref[1]
        row_max = jnp.max(jnp.maximum(g0, g1), axis=1, keepdims=True)
        row_low = jnp.min(jnp.minimum(g0, g1), axis=1, keepdims=True)
        lo0 = jnp.broadcast_to(_f32_key(row_low), (tq, LANES))
        hi0 = jnp.broadcast_to(_f32_key(row_max), (tq, LANES)) + 1
        lo_ref[...] = lo0
        hi_ref[...] = hi0
        done_ref[...] = jnp.where(hi0 - lo0 == 1, 1.0, 0.0)
        ext_ref[...] = jnp.zeros((tq, LANES), F32)
        cnt_ref[...] = jnp.zeros((tq, LANES), F32)

        def row_pass(ref, body, init, finish, needed):
            chunks = [pl.ds(rc * rows, rows) for rc in range(tq // rows)]
            flags = [needed(rs) for rs in chunks]
            for rs, flag in zip(chunks, flags):

                @pl.when(flag)
                def _(rs=rs):
                    bound = ref[rs, :]

                    def tile_step(j, carry):
                        keys = sc_ref[j, rs, :]
                        for t in range(tk // LANES):
                            carry = body(carry, keys[:, t * LANES:(t + 1) * LANES], bound)
                        return carry

                    cnt_ref[rs, :] = finish(lax.fori_loop(0, last + 1, tile_step, init))

        def search_cond(c):
            n, stop = c
            return jnp.logical_and(n < 2 * 32 + 2, stop == 0)

        def search_step(c):
            n, _ = c
            stop = (jnp.min(done_ref[...]) > 0.0).astype(jnp.int32)
            lo = lo_ref[...]
            hi = hi_ref[...]
            cand_ref[...] = (lo >> 1) + (hi >> 1) + (lo & hi & 1)
            row_pass(cand_ref, lambda acc, keys, cand: acc + jnp.where(keys >= cand, 1.0, 0.0),
                     jnp.zeros((rows, LANES), F32), lambda acc: acc,
                     lambda rs: jnp.min(done_ref[rs, :]) == 0.0)
            cnt = jnp.sum(cnt_ref[...], axis=1, keepdims=True)
            cand = cand_ref[...]
            active = done_ref[...] == 0.0
            enough = cnt >= float(topk)
            lo = jnp.where(jnp.logical_and(active, enough), cand, lo)
            hi = jnp.where(jnp.logical_and(active, jnp.logical_not(enough)), cand, hi)
            lo_ref[...] = lo
            hi_ref[...] = hi
            extract = jnp.logical_and(active, cnt == float(topk - 1))
            ext_ref[...] = jnp.where(extract, 1.0, ext_ref[...])
            finished = jnp.logical_or(jnp.logical_or(cnt == float(topk), extract), hi - lo == 1)
            done_ref[...] = jnp.where(jnp.logical_and(active, finished), 1.0, done_ref[...])
            return n + 1, stop

        lax.while_loop(search_cond, search_step, (jnp.int32(0), jnp.int32(0)))

        def key_to_f32(keys):
            return pltpu.bitcast(keys ^ ((keys >> 31) & 0x7FFFFFFF), F32)

        row_pass(hi_ref, lambda acc, keys, hi: jnp.maximum(acc, jnp.where(keys < hi, keys, KEY_NEG_INF)),
                 jnp.full((rows, LANES), KEY_NEG_INF, jnp.int32), key_to_f32,
                 lambda rs: jnp.max(ext_ref[rs, :]) > 0.0)
        below = jnp.broadcast_to(_f32_key(jnp.max(cnt_ref[...], axis=1, keepdims=True)), (tq, LANES))
        thr = jnp.where(ext_ref[...] > 0.0, below, lo_ref[...])
        tau_ref[...] = jnp.maximum(thr, INT_MIN + 1)

        m_ref[...] = jnp.full(m_ref.shape, M_INIT, F32)
        l_ref[...] = jnp.zeros(l_ref.shape, F32)
        acc_ref[...] = jnp.zeros(acc_ref.shape, F32)

    tau = tau_ref[...]
    for t in range(tk // LANES):
        ls = slice(t * LANES, (t + 1) * LANES)
        bias_ref[:, ls] = jnp.where(sc_ref[kj, :, ls] >= tau, 0.0, NEG_BIG)

    def add_bias(x, rs, ls):
        return x + bias_ref[rs, ls]

    heads = [slice(h * HEAD_DIM, (h + 1) * HEAD_DIM) for h in range(A_HEADS)]
    for h, hs in enumerate(heads):
        s_ref[h] = _nt_dot(qa_ref[:, hs], ka_ref[:, hs])
    if bounded:
        for h in range(A_HEADS):
            _softmax_rows_bounded(s_ref.at[h], p_ref.at[h], l_ref.at[h], add_bias)
        for h, hs in enumerate(heads):
            acc_ref[h] = acc_ref[h] + jnp.dot(p_ref[h], va_ref[:, hs], preferred_element_type=F32)
    else:
        for h in range(A_HEADS):
            _softmax_rows(s_ref.at[h], p_ref.at[h], alpha_ref.at[h], m_ref.at[h], l_ref.at[h], add_bias)
        for h, hs in enumerate(heads):
            acc_ref[h] = alpha_ref[h] * acc_ref[h] + jnp.dot(p_ref[h], va_ref[:, hs],
                                                             preferred_element_type=F32)

    @pl.when(kj == last)
    def _finish():
        for h in range(A_HEADS):
            hs = slice(h * HEAD_DIM, (h + 1) * HEAD_DIM)
            o = acc_ref[h] / l_ref[h]
            o_ref[:, hs] = (o * ga_ref[:, hs].astype(F32)).astype(BF16)


def _causal_steps(nq, tq, tk):
    qt, kt = [], []
    for i in range(nq):
        for j in range(((i + 1) * tq - 1) // tk + 1):
            qt.append(i)
            kt.append(j)
    return jnp.asarray(qt, jnp.int32), jnp.asarray(kt, jnp.int32)


def _attn_a(qi, wi, kab, qkr, vg, batch, seq, tq, tk, topk, bounded):
    m = batch * seq
    nq = seq // tq
    nk = seq // tk
    qt, kt = _causal_steps(nq, tq, tk)

    def qrow(b, s, qt_ref, kt_ref):
        return b * nq + qt_ref[s]

    def krow(b, s, qt_ref, kt_ref):
        return b * nk + kt_ref[s]

    grid_spec = pltpu.PrefetchScalarGridSpec(
        num_scalar_prefetch=2,
        grid=(batch, int(qt.shape[0])),
        in_specs=[
            pl.BlockSpec((IDX_HEADS // 2, tq, LANES), lambda b, s, q, k: (0, qrow(b, s, q, k), 0)),
            pl.BlockSpec((tq, LANES), lambda b, s, q, k: (qrow(b, s, q, k), 0)),
            pl.BlockSpec((2, seq, LANES), lambda b, s, q, k: (0, b, 0),
                         pipeline_mode=pl.Buffered(1)),
            pl.BlockSpec((tq, A_WIDTH), lambda b, s, q, k: (qrow(b, s, q, k), 0)),
            pl.BlockSpec((tk, A_WIDTH), lambda b, s, q, k: (krow(b, s, q, k), 1)),
            pl.BlockSpec((tk, A_WIDTH), lambda b, s, q, k: (krow(b, s, q, k), 0)),
            pl.BlockSpec((tq, A_WIDTH), lambda b, s, q, k: (qrow(b, s, q, k), 2)),
        ],
        out_specs=pl.BlockSpec((tq, A_WIDTH), lambda b, s, q, k: (qrow(b, s, q, k), 0)),
        scratch_shapes=[
            pltpu.VMEM((nk, tq, tk), jnp.int32),
            pltpu.VMEM((2, tq, LANES), F32),
            pltpu.VMEM((tq, LANES), jnp.int32),
            pltpu.VMEM((tq, LANES), jnp.int32),
            pltpu.VMEM((tq, LANES), jnp.int32),
            pltpu.VMEM((tq, LANES), jnp.int32),
            pltpu.VMEM((tq, LANES), F32),
            pltpu.VMEM((tq, LANES), F32),
            pltpu.VMEM((tq, LANES), F32),
            pltpu.VMEM((tq, tk), F32),
            pltpu.VMEM((A_HEADS, tq, tk), F32),
            pltpu.VMEM((A_HEADS, tq, tk), BF16),
            pltpu.VMEM((A_HEADS, tq, LANES), F32),
            pltpu.VMEM((A_HEADS, tq, LANES), F32),
            pltpu.VMEM((A_HEADS, tq, LANES), F32),
            pltpu.VMEM((A_HEADS, tq, HEAD_DIM), F32),
        ],
    )
    return pl.pallas_call(
        functools.partial(_a_kernel, tq=tq, tk=tk, topk=topk, rows=SELECT_ROWS, bounded=bounded),
        grid_spec=grid_spec,
        out_shape=jax.ShapeDtypeStruct((m, A_WIDTH), BF16),
        compiler_params=_params(("parallel", "arbitrary")),
        name="attn_a",
    )(qt, kt, qi, wi, kab, qkr, qkr, vg, vg)


def _b_kernel(qt_ref, kt_ref, q_ref, k_ref, v_ref, sg_ref, gain_ref, lq1_ref, lk1_ref, lq2_ref, lk2_ref,
              o_ref, s_ref, p_ref, alpha_ref, m_ref, l_ref, acc_ref, *, tq, heads, lambda_init, bounded):
    step = pl.program_id(2)
    iq = qt_ref[step]
    kj = kt_ref[step]
    hw = 2 * HEAD_DIM
    chains = [(hh, c) for hh in range(heads) for c in range(2)]

    @pl.when(kj == 0)
    def _init():
        m_ref[...] = jnp.full(m_ref.shape, M_INIT, F32)
        l_ref[...] = jnp.zeros(l_ref.shape, F32)
        acc_ref[...] = jnp.zeros(acc_ref.shape, F32)

    def causal_mask(x, rs, ls):
        row = rs.start + lax.broadcasted_iota(jnp.int32, x.shape, 0)
        col = ls.start + lax.broadcasted_iota(jnp.int32, x.shape, 1)
        return jnp.where(col <= row, x, NEG_BIG)

    def attend(diagonal):
        adjust = causal_mask if diagonal else (lambda x, rs, ls: x)
        for i, (hh, c) in enumerate(chains):
            cs = slice(hh * hw + c * HEAD_DIM, hh * hw + (c + 1) * HEAD_DIM)
            s_ref[i] = _nt_dot(q_ref[:, cs], k_ref[:, cs])
        if bounded:
            for i in range(len(chains)):
                _softmax_rows_bounded(s_ref.at[i], p_ref.at[i], l_ref.at[i], adjust)
            for i, (hh, c) in enumerate(chains):
                acc_ref[i] = acc_ref[i] + jnp.dot(p_ref[i], v_ref[:, hh * hw:(hh + 1) * hw],
                                                  preferred_element_type=F32)
        else:
            for i in range(len(chains)):
                _softmax_rows(s_ref.at[i], p_ref.at[i], alpha_ref.at[i], m_ref.at[i], l_ref.at[i], adjust)
            for i, (hh, c) in enumerate(chains):
                pv = jnp.dot(p_ref[i], v_ref[:, hh * hw:(hh + 1) * hw], preferred_element_type=F32)
                alpha = alpha_ref[i]
                for t in range(hw // LANES):
                    ls = slice(t * LANES, (t + 1) * LANES)
                    acc_ref[i, :, ls] = alpha * acc_ref[i, :, ls] + pv[:, ls]

    @pl.when(kj < iq)
    def _below_diagonal():
        attend(False)

    @pl.when(kj == iq)
    def _diagonal_and_finish():
        attend(True)
        lam = (jnp.exp(jnp.sum(lq1_ref[...] * lk1_ref[...], axis=1, keepdims=True))
               - jnp.exp(jnp.sum(lq2_ref[...] * lk2_ref[...], axis=1, keepdims=True))
               + lambda_init)
        for hh in range(heads):
            o0 = acc_ref[2 * hh] / _tile_lanes(l_ref[2 * hh], hw)
            o1 = acc_ref[2 * hh + 1] / _tile_lanes(l_ref[2 * hh + 1], hw)
            o = o0 - lam * o1
            ms = jnp.mean(o * o, axis=-1, keepdims=True)
            y = o * lax.rsqrt(ms + SUBLN_EPS) * gain_ref[...] * (1.0 - lambda_init)
            hs = slice(hh * hw, (hh + 1) * hw)
            o_ref[:, hs] = (y * sg_ref[:, hs].astype(F32)).astype(BF16)


def _attn_b(qkr, vg, gain, lq1, lk1, lq2, lk2, batch, seq, tq, heads, lambda_init, bounded):
    m = batch * seq
    nq = seq // tq
    bw = heads * 2 * HEAD_DIM
    groups = B_HEADS // heads
    chains = 2 * heads
    qt, kt = _causal_steps(nq, tq, tq)

    def qrow(b, s, qt_ref):
        return b * nq + qt_ref[s]

    vec = pl.BlockSpec((1, HEAD_DIM), lambda b, h, s, q, k: (0, 0))
    grid_spec = pltpu.PrefetchScalarGridSpec(
        num_scalar_prefetch=2,
        grid=(batch, groups, int(qt.shape[0])),
        in_specs=[
            pl.BlockSpec((tq, bw), lambda b, h, s, q, k: (qrow(b, s, q), 2 * groups + h)),
            pl.BlockSpec((tq, bw), lambda b, h, s, q, k: (qrow(b, s, k), 3 * groups + h)),
            pl.BlockSpec((tq, bw), lambda b, h, s, q, k: (qrow(b, s, k), groups + h)),
            pl.BlockSpec((tq, bw), lambda b, h, s, q, k: (qrow(b, s, q), 3 * groups + h)),
            pl.BlockSpec((1, 2 * HEAD_DIM), lambda b, h, s, q, k: (0, 0)),
            vec, vec, vec, vec,
        ],
        out_specs=pl.BlockSpec((tq, bw), lambda b, h, s, q, k: (qrow(b, s, q), h)),
        scratch_shapes=[
            pltpu.VMEM((chains, tq, tq), F32),
            pltpu.VMEM((chains, tq, tq), BF16),
            pltpu.VMEM((chains, tq, LANES), F32),
            pltpu.VMEM((chains, tq, LANES), F32),
            pltpu.VMEM((chains, tq, LANES), F32),
            pltpu.VMEM((chains, tq, 2 * HEAD_DIM), F32),
        ],
    )
    return pl.pallas_call(
        functools.partial(_b_kernel, tq=tq, heads=heads, lambda_init=lambda_init, bounded=bounded),
        grid_spec=grid_spec,
        out_shape=jax.ShapeDtypeStruct((m, B_WIDTH), BF16),
        compiler_params=_params(("parallel", "parallel", "arbitrary")),
        name="attn_b",
    )(qt, kt, qkr, qkr, vg, vg, gain, lq1, lk1, lq2, lk2)


def _merge_kernel(oa_ref, ob_ref, wa_ref, wb_ref, ma_ref, mb_ref, o_ref):
    ya = jnp.dot(oa_ref[...], wa_ref[...], preferred_element_type=F32)
    yb = jnp.dot(ob_ref[...], wb_ref[...], preferred_element_type=F32)
    o_ref[...] = (ma_ref[...].astype(F32) * ya + mb_ref[...].astype(F32) * yb).astype(BF16)


def _merge(oa, ob, w_a, w_b, vg, d_model, tm, tn):
    m = oa.shape[0]
    gate_a0 = (A_WIDTH + B_WIDTH) * 2 // tn
    gate_b0 = gate_a0 + d_model // tn
    return pl.pallas_call(
        _merge_kernel,
        grid=(m // tm, d_model // tn),
        in_specs=[
            pl.BlockSpec((tm, A_WIDTH), lambda i, j: (i, 0)),
            pl.BlockSpec((tm, B_WIDTH), lambda i, j: (i, 0)),
            pl.BlockSpec((A_WIDTH, tn), lambda i, j: (0, j)),
            pl.BlockSpec((B_WIDTH, tn), lambda i, j: (0, j)),
            pl.BlockSpec((tm, tn), lambda i, j: (i, gate_a0 + j)),
            pl.BlockSpec((tm, tn), lambda i, j: (i, gate_b0 + j)),
        ],
        out_specs=pl.BlockSpec((tm, tn), lambda i, j: (i, j)),
        out_shape=jax.ShapeDtypeStruct((m, d_model), BF16),
        compiler_params=_params(("parallel", "arbitrary")),
        name="merge",
    )(oa, ob, w_a, w_b, vg, vg)


def _out_kernel(mg_ref, w_ref, x_ref, o_ref):
    o_ref[...] = x_ref[...] + jnp.dot(mg_ref[...], w_ref[...], preferred_element_type=F32)


def _out_proj(merged, w_out, x2d, tm, tn):
    m, d = x2d.shape
    return pl.pallas_call(
        _out_kernel,
        grid=(m // tm, d // tn),
        in_specs=[
            pl.BlockSpec((tm, d), lambda i, j: (i, 0)),
            pl.BlockSpec((d, tn), lambda i, j: (0, j)),
            pl.BlockSpec((tm, tn), lambda i, j: (i, j)),
        ],
        out_specs=pl.BlockSpec((tm, tn), lambda i, j: (i, j)),
        out_shape=jax.ShapeDtypeStruct((m, d), F32),
        compiler_params=_params(("parallel", "arbitrary")),
        name="out_proj",
    )(merged, w_out, x2d)


def _qi_column_order():
    lane = np.arange(LANES)
    quarter, r = lane // 32, lane % 32
    dim = r + 32 * (quarter // 2)
    q_cols = np.concatenate([(2 * p + quarter % 2) * IDX_DIM + dim for p in range(IDX_HEADS // 2)])
    return q_cols, dim, (quarter % 2 == 0)


def _layer(x2d, tabs, norm_gain, w_in, a_q_gain, a_k_gain, b_q_gain, b_k_gain,
           lq1, lk1, lq2, lk2, b_subln_gain, w_o_a, w_o_b, w_out, layer, batch, seq):
    d_model = x2d.shape[1]
    cos_h, sin_h, cos_i, sin_i = tabs
    sizes = (A_WIDTH,) * 4 + (IDX_HEADS * IDX_DIM, IDX_DIM, IDX_HEADS) + (B_WIDTH,) * 4 + (d_model,) * 2
    off = np.concatenate([[0], np.cumsum(sizes)])
    w_bf = w_in.astype(BF16)
    seg = lambda i: w_bf[:, off[i]:off[i + 1]]
    (w_qa, w_ka, w_va, w_ga, w_qi, w_ki, w_wi, w_qb, w_kb, w_vb, w_gb, w_ma, w_mb) = [seg(i) for i in range(13)]

    scale = HEAD_DIM ** -0.5 * LOG2E
    w_qk = jnp.concatenate([w_qa, w_ka, w_qb, w_kb], axis=1)
    gain_qk = jnp.concatenate([jnp.tile(a_q_gain * scale, A_HEADS), jnp.tile(a_k_gain, A_HEADS),
                               jnp.tile(b_q_gain * scale, 2 * B_HEADS), jnp.tile(b_k_gain, 2 * B_HEADS)])[None, :]
    q_cols, k_dim, even = _qi_column_order()
    w_qi_p = w_qi[:, q_cols]
    w_k_rep = w_ki[:, k_dim]
    zeros = jnp.zeros_like(w_k_rep)
    w_kw = jnp.concatenate([
        jnp.where(even[None, :], w_k_rep, zeros), jnp.where(even[None, :], zeros, w_k_rep),
        jnp.pad(w_wi, ((0, 0), (0, 2 * LANES - IDX_HEADS)))], axis=1)
    w_vg = jnp.concatenate([w_va, w_vb, w_ga, w_gb, w_ma, w_mb], axis=1)

    m = x2d.shape[0]
    tm, tn = PROJ_TILE
    h = _norm(x2d, norm_gain[None, :], NORM_ROWS)

    tab_spec = pl.BlockSpec((tm, LANES), lambda i, j: (i, 0))
    qkr = _proj_qk(h, w_qk, gain_qk, cos_h, sin_h, tm, tn)
    qi = _proj(h, w_qi_p, (cos_i, sin_i), [tab_spec, tab_spec], _ep_qi,
               jax.ShapeDtypeStruct((IDX_HEADS // 2, m, LANES), BF16),
               pl.BlockSpec((tn // LANES, tm, LANES), lambda i, j: (j, i, 0)), tm, tn, "proj_qi")
    kab, wi = _proj(h, w_kw, (cos_i, sin_i), [tab_spec, tab_spec], _ep_kw,
                    (jax.ShapeDtypeStruct((2, m, LANES), BF16), jax.ShapeDtypeStruct((m, LANES), F32)),
                    (pl.BlockSpec((2, tm, LANES), lambda i, j: (0, i, 0)),
                     pl.BlockSpec((tm, LANES), lambda i, j: (i, 0))), tm, w_kw.shape[1], "proj_kw")
    n_plain = (A_WIDTH + B_WIDTH) // tn
    vg = _proj(h, w_vg, (), [], functools.partial(_ep_vg, n_plain=n_plain, n_silu=n_plain),
               jax.ShapeDtypeStruct((m, w_vg.shape[1]), BF16),
               pl.BlockSpec((tm, tn), lambda i, j: (i, j)), tm, tn, "proj_vg")

    def logit_bound(q_gain, k_gain):
        return HEAD_DIM * 1.02 * jnp.max(jnp.abs(q_gain * scale)) * jnp.max(jnp.abs(k_gain))

    def with_bound(bound, attend):
        return lax.cond(bound <= MAX_LOGIT_BOUND, lambda: attend(True), lambda: attend(False))

    topk = min(TOPK_MAX, seq // 4)
    oa = with_bound(logit_bound(a_q_gain, a_k_gain),
                    lambda f: _attn_a(qi, wi, kab, qkr, vg, batch, seq, *ATTN_A_TILE, topk, f))
    lambda_init = 0.8 - 0.6 * math.exp(-0.3 * layer)
    ob = with_bound(logit_bound(b_q_gain, b_k_gain),
                    lambda f: _attn_b(qkr, vg, b_subln_gain[None, :], lq1[None, :], lk1[None, :], lq2[None, :],
                                      lk2[None, :], batch, seq, ATTN_B_TILE, B_HEADS, lambda_init, f))
    merged = _merge(oa, ob, w_o_a.astype(BF16), w_o_b.astype(BF16), vg, d_model, *OUT_TILE)
    return _out_proj(merged, w_out.astype(BF16), x2d, *OUT_TILE)


def kernel(x, positions, norm_gain, w_in, a_q_gain, a_k_gain, b_q_gain, b_k_gain, lambda_q1, lambda_k1,
           lambda_q2, lambda_k2, b_subln_gain, w_o_a, w_o_b, w_out):
    batch, seq, d_model = x.shape
    m = batch * seq
    pos = positions.astype(F32).reshape(m, 1)
    ang_h = pos * ROPE_THETA ** (-jnp.arange(0, HEAD_DIM, 2, dtype=F32) / HEAD_DIM)
    ang_i = pos * ROPE_THETA ** (-jnp.arange(0, IDX_DIM, 2, dtype=F32) / IDX_DIM)
    ch, sh, ci, si = jnp.cos(ang_h), jnp.sin(ang_h), jnp.cos(ang_i), jnp.sin(ang_i)
    tabs = (jnp.concatenate([ch, ch], axis=1), jnp.concatenate([-sh, sh], axis=1),
            jnp.concatenate([ci] * 4, axis=1), jnp.concatenate([-si, -si, si, si], axis=1))
    x2d = x.reshape(m, d_model)
    for layer in range(norm_gain.shape[0]):
        x2d = _layer(x2d, tabs, norm_gain[layer], w_in[layer], a_q_gain[layer], a_k_gain[layer],
                     b_q_gain[layer], b_k_gain[layer], lambda_q1[layer], lambda_k1[layer],
                     lambda_q2[layer], lambda_k2[layer], b_subln_gain[layer], w_o_a[layer], w_o_b[layer],
                     w_out[layer], layer, batch, seq)
    return x2d.reshape(batch, seq, d_model)
```

```python
import functools
import math

import jax
import jax.numpy as jnp
import numpy as np
from jax import lax
from jax.experimental import pallas as pl
from jax.experimental.pallas import tpu as pltpu

F32 = jnp.float32
BF16 = jnp.bfloat16

HEAD_DIM = 128
A_HEADS = 8
A_WIDTH = A_HEADS * HEAD_DIM
IDX_HEADS = 16
IDX_DIM = 64
TOPK_MAX = 256
B_HEADS = 4
B_WIDTH = B_HEADS * 2 * HEAD_DIM
ROPE_THETA = 10000.0
NORM_EPS = 1e-6
SUBLN_EPS = 1e-5

LANES = 128
INT_MIN = -(2 ** 31)
KEY_NEG_INF = INT_MIN + 0x7FFFFF
NEG_BIG = -1e30
M_INIT = -5e29
LOG2E = math.log2(math.e)
MAX_LOGIT_BOUND = 60.0
SELECT_ROWS = 64
SOFTMAX_ROWS = 32
PROJ_SUB = 256
VMEM_LIMIT_BYTES = 56 * 1024 * 1024

NORM_ROWS = 512
PROJ_TILE = (1024, 1024)
ATTN_A_TILE = (256, 1024)
ATTN_B_TILE = 512
OUT_TILE = (1024, 512)


def _nt_dot(a, b):
    return lax.dot_general(a, b, (((1,), (1,)), ((), ())), preferred_element_type=F32)


def _tile_lanes(v, width):
    reps = width // LANES
    return v if reps == 1 else jnp.concatenate([v] * reps, axis=1)


def _params(semantics):
    return pltpu.CompilerParams(dimension_semantics=semantics, vmem_limit_bytes=VMEM_LIMIT_BYTES)


def _norm_kernel(x_ref, g_ref, h_ref):
    x = x_ref[...]
    ms = jnp.mean(x * x, axis=-1, keepdims=True)
    h_ref[...] = (x * lax.rsqrt(ms + NORM_EPS) * g_ref[...]).astype(BF16)


def _norm(x2d, gain, tm):
    m, d = x2d.shape
    return pl.pallas_call(
        _norm_kernel,
        grid=(m // tm,),
        in_specs=[pl.BlockSpec((tm, d), lambda i: (i, 0)), pl.BlockSpec((1, d), lambda i: (0, 0))],
        out_specs=pl.BlockSpec((tm, d), lambda i: (i, 0)),
        out_shape=jax.ShapeDtypeStruct((m, d), BF16),
        compiler_params=_params(("parallel",)),
        name="norm",
    )(x2d, gain)


def _proj_body(h_ref, w_ref, *rest, n_aux, epilogue):
    tn = w_ref.shape[1]

    def sub_dot(t):
        return jnp.dot(h_ref[...], w_ref[:, t * PROJ_SUB:(t + 1) * PROJ_SUB], preferred_element_type=F32)

    epilogue(sub_dot, tn // PROJ_SUB, rest[:n_aux], rest[n_aux:])


def _proj(h, w, aux, aux_specs, epilogue, out_shape, out_specs, tm, tn, name):
    m, k = h.shape
    n = w.shape[1]
    return pl.pallas_call(
        functools.partial(_proj_body, n_aux=len(aux), epilogue=epilogue),
        grid=(m // tm, n // tn),
        in_specs=[pl.BlockSpec((tm, k), lambda i, j: (i, 0)),
                  pl.BlockSpec((k, tn), lambda i, j: (0, j))] + aux_specs,
        out_specs=out_specs,
        out_shape=out_shape,
        compiler_params=_params(("parallel", "arbitrary")),
        name=name,
    )(h, w, *aux)


def _rope128(y, cos_full, sin_signed):
    return y * cos_full + pltpu.roll(y, 64, 1) * sin_signed


def _proj_qk_kernel(h_ref, w_ref, g_ref, c_ref, s_ref, o_ref, acc_ref):
    acc_ref[...] = jnp.dot(h_ref[...], w_ref[...], preferred_element_type=F32)
    same_head = (lax.broadcasted_iota(jnp.int32, (PROJ_SUB, PROJ_SUB), 0) // HEAD_DIM
                 == lax.broadcasted_iota(jnp.int32, (PROJ_SUB, PROJ_SUB), 1) // HEAD_DIM)
    head_mean = jnp.where(same_head, 1.0 / HEAD_DIM, 0.0).astype(BF16)

    def sub_tile(t, carry):
        cs = pl.ds(pl.multiple_of(t * PROJ_SUB, PROJ_SUB), PROJ_SUB)
        acc = acc_ref[:, cs]
        ms = jnp.dot((acc * acc).astype(BF16), head_mean, preferred_element_type=F32)
        y = acc * lax.rsqrt(ms + NORM_EPS) * g_ref[:, cs]
        c = c_ref[...]
        s = s_ref[...]
        for u in range(PROJ_SUB // LANES):
            ls = pl.ds(pl.multiple_of(t * PROJ_SUB + u * LANES, LANES), LANES)
            o_ref[:, ls] = _rope128(y[:, u * LANES:(u + 1) * LANES], c, s).astype(BF16)
        return carry

    lax.fori_loop(0, acc_ref.shape[1] // PROJ_SUB, sub_tile, 0)


def _proj_qk(h, w, gain, cos_full, sin_signed, tm, tn):
    m, k = h.shape
    n = w.shape[1]
    tab_spec = pl.BlockSpec((tm, LANES), lambda i, j: (i, 0))
    return pl.pallas_call(
        _proj_qk_kernel,
        grid=(m // tm, n // tn),
        in_specs=[pl.BlockSpec((tm, k), lambda i, j: (i, 0)),
                  pl.BlockSpec((k, tn), lambda i, j: (0, j)),
                  pl.BlockSpec((1, tn), lambda i, j: (0, j)), tab_spec, tab_spec],
        out_specs=pl.BlockSpec((tm, tn), lambda i, j: (i, j)),
        out_shape=jax.ShapeDtypeStruct((m, n), BF16),
        scratch_shapes=[pltpu.VMEM((tm, tn), F32)],
        compiler_params=_params(("parallel", "arbitrary")),
        name="proj_qk",
    )(h, w, gain, cos_full, sin_signed)


def _ep_qi(sub_dot, n_sub, aux, outs):
    c_ref, s_ref = aux
    (o_ref,) = outs
    c = c_ref[...]
    s = s_ref[...]
    per = PROJ_SUB // LANES
    for t in range(n_sub):
        acc = sub_dot(t)
        for u in range(per):
            o_ref[t * per + u] = _rope128(acc[:, u * LANES:(u + 1) * LANES], c, s).astype(BF16)


def _ep_kw(sub_dot, n_sub, aux, outs):
    c_ref, s_ref = aux
    kab_ref, w_ref = outs
    c = c_ref[...]
    s = s_ref[...]
    acc = sub_dot(0)
    for u in range(2):
        kab_ref[u] = _rope128(acc[:, u * LANES:(u + 1) * LANES], c, s).astype(BF16)
    w_ref[...] = sub_dot(1)[:, :LANES] * (IDX_HEADS ** -0.5 * IDX_DIM ** -0.5)


def _ep_vg(sub_dot, n_sub, aux, outs, *, n_plain, n_silu):
    (o_ref,) = outs
    j = pl.program_id(1)

    def store(fn):
        for t in range(n_sub):
            o_ref[:, t * PROJ_SUB:(t + 1) * PROJ_SUB] = fn(sub_dot(t)).astype(BF16)

    @pl.when(j < n_plain)
    def _():
        store(lambda a: a)

    @pl.when(jnp.logical_and(j >= n_plain, j < n_plain + n_silu))
    def _():
        store(lambda a: a * jax.nn.sigmoid(a))

    @pl.when(j >= n_plain + n_silu)
    def _():
        store(jax.nn.sigmoid)


def _f32_key(v):
    bits = pltpu.bitcast(v, jnp.int32)
    return bits ^ ((bits >> 31) & 0x7FFFFFFF)


def _softmax_rows(s_ref, p_ref, alpha_ref, m_ref, l_ref, adjust):
    tq, tk = s_ref.shape
    for r in range(tq // SOFTMAX_ROWS):
        rs = slice(r * SOFTMAX_ROWS, (r + 1) * SOFTMAX_ROWS)
        lanes = [slice(t * LANES, (t + 1) * LANES) for t in range(tk // LANES)]
        s = [adjust(s_ref[rs, ls], rs, ls) for ls in lanes]
        m_prev = m_ref[rs, :]
        m_new = jnp.maximum(m_prev, jnp.max(functools.reduce(jnp.maximum, s), axis=1, keepdims=True))
        alpha = jnp.exp2(m_prev - m_new)
        p = [jnp.exp2(x - m_new) for x in s]
        l_ref[rs, :] = alpha * l_ref[rs, :] + jnp.sum(functools.reduce(jnp.add, p), axis=1, keepdims=True)
        for ls, x in zip(lanes, p):
            p_ref[rs, ls] = x.astype(BF16)
        alpha_ref[rs, :] = alpha
        m_ref[rs, :] = m_new


def _a_kernel(qt_ref, kt_ref, qi_ref, wi_ref, kab_ref, qa_ref, ka_ref, va_ref, ga_ref, o_ref,
              sc_ref, gmax_ref, tau_ref, lo_ref, hi_ref, cand_ref, done_ref, ext_ref, cnt_ref, bias_ref,
              s_ref, p_ref, alpha_ref, m_ref, l_ref, acc_ref, *, tq, tk, topk, rows, bounded):
    step = pl.program_id(1)
    iq = qt_ref[step]
    kj = kt_ref[step]
    last = ((iq + 1) * tq - 1) // tk

    @pl.when(kj == 0)
    def _index_and_select():
        w = wi_ref[...]
        row = iq * tq + lax.broadcasted_iota(jnp.int32, (tq, tk), 0)
        col0 = lax.broadcasted_iota(jnp.int32, (tq, tk), 1)
        gmax_ref[...] = jnp.full(gmax_ref.shape, -jnp.inf, F32)

        def score_tile(j, carry):
            start = pl.multiple_of(j * tk, tk)
            k_even = kab_ref[0, pl.ds(start, tk), :]
            k_odd = kab_ref[1, pl.ds(start, tk), :]
            acc = jnp.zeros((tq, tk), F32)
            for p in range(IDX_HEADS // 2):
                q_p = qi_ref[p]
                s_even = jnp.maximum(_nt_dot(q_p, k_even), 0.0)
                s_odd = jnp.maximum(_nt_dot(q_p, k_odd), 0.0)
                acc = acc + w[:, 2 * p:2 * p + 1] * s_even + w[:, 2 * p + 1:2 * p + 2] * s_odd
            causal = col0 + j * tk <= row
            sc_ref[j] = jnp.where(causal, _f32_key(acc), INT_MIN)
            accm = jnp.where(causal, acc, -jnp.inf)
            half = tk // 2
            for g in range(2):
                gm = gmax_ref[g]
                for t in range(half // LANES):
                    lo = g * half + t * LANES
                    gm = jnp.maximum(gm, accm[:, lo:lo + LANES])
                gmax_ref[g] = gm
            return carry

        lax.fori_loop(0, last + 1, score_tile, 0)

        g0 = gmax_ref[0]
        g1 = gmax_ref[1]
        row_max = jnp.max(jnp.maximum(g0, g1), axis=1, keepdims=True)
        row_low = jnp.min(jnp.minimum(g0, g1), axis=1, keepdims=True)
        lo0 = jnp.broadcast_to(_f32_key(row_low), (tq, LANES))
        hi0 = jnp.broadcast_to(_f32_key(row_max), (tq, LANES)) + 1
        lo_ref[...] = lo0
        hi_ref[...] = hi0
        done_ref[...] = jnp.where(hi0 - lo0 == 1, 1.0, 0.0)
        ext_ref[...] = jnp.zeros((tq, LANES), F32)
        cnt_ref[...] = jnp.zeros((tq, LANES), F32)

        def row_pass(ref, body, init, finish, needed):
            chunks = [pl.ds(rc * rows, rows) for rc in range(tq // rows)]
            flags = [needed(rs) for rs in chunks]
            for rs, flag in zip(chunks, flags):

                @pl.when(flag)
                def _(rs=rs):
                    bound = ref[rs, :]

                    def tile_step(j, carry):
                        keys = sc_ref[j, rs, :]
                        for t in range(tk // LANES):
                            carry = body(carry, keys[:, t * LANES:(t + 1) * LANES], bound)
                        return carry

                    cnt_ref[rs, :] = finish(lax.fori_loop(0, last + 1, tile_step, init))

        def search_cond(c):
            n, stop = c
            return jnp.logical_and(n < 2 * 32 + 2, stop == 0)

        def search_step(c):
            n, _ = c
            stop = (jnp.min(done_ref[...]) > 0.0).astype(jnp.int32)
            lo = lo_ref[...]
            hi = hi_ref[...]
            cand_ref[...] = (lo >> 1) + (hi >> 1) + (lo & hi & 1)
            row_pass(cand_ref, lambda acc, keys, cand: acc + jnp.where(keys >= cand, 1.0, 0.0),
                     jnp.zeros((rows, LANES), F32), lambda acc: acc,
                     lambda rs: jnp.min(done_ref[rs, :]) == 0.0)
            cnt = jnp.sum(cnt_ref[...], axis=1, keepdims=True)
            cand = cand_ref[...]
            active = done_ref[...] == 0.0
            enough = cnt >= float(topk)
            lo = jnp.where(jnp.logical_and(active, enough), cand, lo)
            hi = jnp.where(jnp.logical_and(active, jnp.logical_not(enough)), cand, hi)
            lo_ref[...] = lo
            hi_ref[...] = hi
            extract = jnp.logical_and(active, cnt == float(topk - 1))
            ext_ref[...] = jnp.where(extract, 1.0, ext_ref[...])
            finished = jnp.logical_or(jnp.logical_or(cnt == float(topk), extract), hi - lo == 1)
            done_ref[...] = jnp.where(jnp.logical_and(active, finished), 1.0, done_ref[...])
            return n + 1, stop

        lax.while_loop(search_cond, search_step, (jnp.int32(0), jnp.int32(0)))

        def key_to_f32(keys):
            return pltpu.bitcast(keys ^ ((keys >> 31) & 0x7FFFFFFF), F32)

        row_pass(hi_ref, lambda acc, keys, hi: jnp.maximum(acc, jnp.where(keys < hi, keys, KEY_NEG_INF)),
                 jnp.full((rows, LANES), KEY_NEG_INF, jnp.int32), key_to_f32,
                 lambda rs: jnp.max(ext_ref[rs, :]) > 0.0)
        below = jnp.broadcast_to(_f32_key(jnp.max(cnt_ref[...], axis=1, keepdims=True)), (tq, LANES))
        thr = jnp.where(ext_ref[...] > 0.0, below, lo_ref[...])
        tau_ref[...] = jnp.maximum(thr, INT_MIN + 1)

        m_ref[...] = jnp.full(m_ref.shape, M_INIT, F32)
        l_ref[...] = jnp.zeros(l_ref.shape, F32)
        acc_ref[...] = jnp.zeros(acc_ref.shape, F32)

    tau = tau_ref[...]
    for t in range(tk // LANES):
        ls = slice(t * LANES, (t + 1) * LANES)
        bias_ref[:, ls] = jnp.where(sc_ref[kj, :, ls] >= tau, 0.0, NEG_BIG)

    def add_bias(x, rs, ls):
        return x + bias_ref[rs, ls]

    heads = [slice(h * HEAD_DIM, (h + 1) * HEAD_DIM) for h in range(A_HEADS)]
    if bounded:
        bias = bias_ref[...]
        for h, hs in enumerate(heads):
            p = jnp.exp2(_nt_dot(qa_ref[:, hs], ka_ref[:, hs]) + bias)
            l_ref[h] = l_ref[h] + jnp.sum(p, axis=1, keepdims=True)
            acc_ref[h] = acc_ref[h] + jnp.dot(p.astype(BF16), va_ref[:, hs], preferred_element_type=F32)
    else:
        for h, hs in enumerate(heads):
            s_ref[h] = _nt_dot(qa_ref[:, hs], ka_ref[:, hs])
        for h in range(A_HEADS):
            _softmax_rows(s_ref.at[h], p_ref.at[h], alpha_ref.at[h], m_ref.at[h], l_ref.at[h], add_bias)
        for h, hs in enumerate(heads):
            acc_ref[h] = alpha_ref[h] * acc_ref[h] + jnp.dot(p_ref[h], va_ref[:, hs],
                                                             preferred_element_type=F32)

    @pl.when(kj == last)
    def _finish():
        for h in range(A_HEADS):
            hs = slice(h * HEAD_DIM, (h + 1) * HEAD_DIM)
            o = acc_ref[h] / l_ref[h]
            o_ref[:, hs] = (o * ga_ref[:, hs].astype(F32)).astype(BF16)


def _causal_steps(nq, tq, tk):
    qt, kt = [], []
    for i in range(nq):
        for j in range(((i + 1) * tq - 1) // tk + 1):
            qt.append(i)
            kt.append(j)
    return jnp.asarray(qt, jnp.int32), jnp.asarray(kt, jnp.int32)


def _attn_a(qi, wi, kab, qkr, vg, batch, seq, tq, tk, topk, bounded):
    m = batch * seq
    nq = seq // tq
    nk = seq // tk
    qt, kt = _causal_steps(nq, tq, tk)

    def qrow(b, s, qt_ref, kt_ref):
        return b * nq + qt_ref[s]

    def krow(b, s, qt_ref, kt_ref):
        return b * nk + kt_ref[s]

    grid_spec = pltpu.PrefetchScalarGridSpec(
        num_scalar_prefetch=2,
        grid=(batch, int(qt.shape[0])),
        in_specs=[
            pl.BlockSpec((IDX_HEADS // 2, tq, LANES), lambda b, s, q, k: (0, qrow(b, s, q, k), 0)),
            pl.BlockSpec((tq, LANES), lambda b, s, q, k: (qrow(b, s, q, k), 0)),
            pl.BlockSpec((2, seq, LANES), lambda b, s, q, k: (0, b, 0),
                         pipeline_mode=pl.Buffered(1)),
            pl.BlockSpec((tq, A_WIDTH), lambda b, s, q, k: (qrow(b, s, q, k), 0)),
            pl.BlockSpec((tk, A_WIDTH), lambda b, s, q, k: (krow(b, s, q, k), 1)),
            pl.BlockSpec((tk, A_WIDTH), lambda b, s, q, k: (krow(b, s, q, k), 0)),
            pl.BlockSpec((tq, A_WIDTH), lambda b, s, q, k: (qrow(b, s, q, k), 2)),
        ],
        out_specs=pl.BlockSpec((tq, A_WIDTH), lambda b, s, q, k: (qrow(b, s, q, k), 0)),
        scratch_shapes=[
            pltpu.VMEM((nk, tq, tk), jnp.int32),
            pltpu.VMEM((2, tq, LANES), F32),
            pltpu.VMEM((tq, LANES), jnp.int32),
            pltpu.VMEM((tq, LANES), jnp.int32),
            pltpu.VMEM((tq, LANES), jnp.int32),
            pltpu.VMEM((tq, LANES), jnp.int32),
            pltpu.VMEM((tq, LANES), F32),
            pltpu.VMEM((tq, LANES), F32),
            pltpu.VMEM((tq, LANES), F32),
            pltpu.VMEM((tq, tk), F32),
            pltpu.VMEM((A_HEADS, tq, tk), F32),
            pltpu.VMEM((A_HEADS, tq, tk), BF16),
            pltpu.VMEM((A_HEADS, tq, LANES), F32),
            pltpu.VMEM((A_HEADS, tq, LANES), F32),
            pltpu.VMEM((A_HEADS, tq, LANES), F32),
            pltpu.VMEM((A_HEADS, tq, HEAD_DIM), F32),
        ],
    )
    return pl.pallas_call(
        functools.partial(_a_kernel, tq=tq, tk=tk, topk=topk, rows=SELECT_ROWS, bounded=bounded),
        grid_spec=grid_spec,
        out_shape=jax.ShapeDtypeStruct((m, A_WIDTH), BF16),
        compiler_params=_params(("parallel", "arbitrary")),
        name="attn_a",
    )(qt, kt, qi, wi, kab, qkr, qkr, vg, vg)


def _b_kernel(qt_ref, kt_ref, q_ref, k_ref, v_ref, sg_ref, gain_ref, lq1_ref, lk1_ref, lq2_ref, lk2_ref,
              o_ref, s_ref, p_ref, alpha_ref, m_ref, l_ref, acc_ref, *, tq, heads, lambda_init, bounded):
    step = pl.program_id(2)
    iq = qt_ref[step]
    kj = kt_ref[step]
    hw = 2 * HEAD_DIM
    chains = [(hh, c) for hh in range(heads) for c in range(2)]

    @pl.when(kj == 0)
    def _init():
        m_ref[...] = jnp.full(m_ref.shape, M_INIT, F32)
        l_ref[...] = jnp.zeros(l_ref.shape, F32)
        acc_ref[...] = jnp.zeros(acc_ref.shape, F32)

    def causal_mask(x, rs, ls):
        row = rs.start + lax.broadcasted_iota(jnp.int32, x.shape, 0)
        col = ls.start + lax.broadcasted_iota(jnp.int32, x.shape, 1)
        return jnp.where(col <= row, x, NEG_BIG)

    def attend(diagonal):
        adjust = causal_mask if diagonal else (lambda x, rs, ls: x)
        qk = lambda hh, c: (slice(hh * hw + c * HEAD_DIM, hh * hw + (c + 1) * HEAD_DIM),) * 2
        if bounded:
            if diagonal:
                keep = (lax.broadcasted_iota(jnp.int32, (tq, tq), 1)
                        <= lax.broadcasted_iota(jnp.int32, (tq, tq), 0))
            for i, (hh, c) in enumerate(chains):
                qs, ks = qk(hh, c)
                p = jnp.exp2(_nt_dot(q_ref[:, qs], k_ref[:, ks]))
                if diagonal:
                    p = jnp.where(keep, p, 0.0)
                l_ref[i] = l_ref[i] + jnp.sum(p, axis=1, keepdims=True)
                acc_ref[i] = acc_ref[i] + jnp.dot(p.astype(BF16), v_ref[:, hh * hw:(hh + 1) * hw],
                                                  preferred_element_type=F32)
        else:
            for i, (hh, c) in enumerate(chains):
                qs, ks = qk(hh, c)
                s_ref[i] = _nt_dot(q_ref[:, qs], k_ref[:, ks])
            for i in range(len(chains)):
                _softmax_rows(s_ref.at[i], p_ref.at[i], alpha_ref.at[i], m_ref.at[i], l_ref.at[i], adjust)
            for i, (hh, c) in enumerate(chains):
                pv = jnp.dot(p_ref[i], v_ref[:, hh * hw:(hh + 1) * hw], preferred_element_type=F32)
                alpha = alpha_ref[i]
                for t in range(hw // LANES):
                    ls = slice(t * LANES, (t + 1) * LANES)
                    acc_ref[i, :, ls] = alpha * acc_ref[i, :, ls] + pv[:, ls]

    @pl.when(kj < iq)
    def _below_diagonal():
        attend(False)

    @pl.when(kj == iq)
    def _diagonal_and_finish():
        attend(True)
        lam = (jnp.exp(jnp.sum(lq1_ref[...] * lk1_ref[...], axis=1, keepdims=True))
               - jnp.exp(jnp.sum(lq2_ref[...] * lk2_ref[...], axis=1, keepdims=True))
               + lambda_init)
        for hh in range(heads):
            o0 = acc_ref[2 * hh] / _tile_lanes(l_ref[2 * hh], hw)
            o1 = acc_ref[2 * hh + 1] / _tile_lanes(l_ref[2 * hh + 1], hw)
            o = o0 - lam * o1
            ms = jnp.mean(o * o, axis=-1, keepdims=True)
            y = o * lax.rsqrt(ms + SUBLN_EPS) * gain_ref[...] * (1.0 - lambda_init)
            hs = slice(hh * hw, (hh + 1) * hw)
            o_ref[:, hs] = (y * sg_ref[:, hs].astype(F32)).astype(BF16)


def _attn_b(qkr, vg, gain, lq1, lk1, lq2, lk2, batch, seq, tq, heads, lambda_init, bounded):
    m = batch * seq
    nq = seq // tq
    bw = heads * 2 * HEAD_DIM
    groups = B_HEADS // heads
    chains = 2 * heads
    qt, kt = _causal_steps(nq, tq, tq)

    def qrow(b, s, qt_ref):
        return b * nq + qt_ref[s]

    vec = pl.BlockSpec((1, HEAD_DIM), lambda b, h, s, q, k: (0, 0))
    grid_spec = pltpu.PrefetchScalarGridSpec(
        num_scalar_prefetch=2,
        grid=(batch, groups, int(qt.shape[0])),
        in_specs=[
            pl.BlockSpec((tq, bw), lambda b, h, s, q, k: (qrow(b, s, q), 2 * groups + h)),
            pl.BlockSpec((tq, bw), lambda b, h, s, q, k: (qrow(b, s, k), 3 * groups + h)),
            pl.BlockSpec((tq, bw), lambda b, h, s, q, k: (qrow(b, s, k), groups + h)),
            pl.BlockSpec((tq, bw), lambda b, h, s, q, k: (qrow(b, s, q), 3 * groups + h)),
            pl.BlockSpec((1, 2 * HEAD_DIM), lambda b, h, s, q, k: (0, 0)),
            vec, vec, vec, vec,
        ],
        out_specs=pl.BlockSpec((tq, bw), lambda b, h, s, q, k: (qrow(b, s, q), h)),
        scratch_shapes=[
            pltpu.VMEM((chains, tq, tq), F32),
            pltpu.VMEM((chains, tq, tq), BF16),
            pltpu.VMEM((chains, tq, LANES), F32),
            pltpu.VMEM((chains, tq, LANES), F32),
            pltpu.VMEM((chains, tq, LANES), F32),
            pltpu.VMEM((chains, tq, 2 * HEAD_DIM), F32),
        ],
    )
    return pl.pallas_call(
        functools.partial(_b_kernel, tq=tq, heads=heads, lambda_init=lambda_init, bounded=bounded),
        grid_spec=grid_spec,
        out_shape=jax.ShapeDtypeStruct((m, B_WIDTH), BF16),
        compiler_params=_params(("parallel", "parallel", "arbitrary")),
        name="attn_b",
    )(qt, kt, qkr, qkr, vg, vg, gain, lq1, lk1, lq2, lk2)


def _merge_kernel(oa_ref, ob_ref, wa_ref, wb_ref, ma_ref, mb_ref, o_ref):
    ya = jnp.dot(oa_ref[...], wa_ref[...], preferred_element_type=F32)
    yb = jnp.dot(ob_ref[...], wb_ref[...], preferred_element_type=F32)
    o_ref[...] = (ma_ref[...].astype(F32) * ya + mb_ref[...].astype(F32) * yb).astype(BF16)


def _merge(oa, ob, w_a, w_b, vg, d_model, tm, tn):
    m = oa.shape[0]
    gate_a0 = (A_WIDTH + B_WIDTH) * 2 // tn
    gate_b0 = gate_a0 + d_model // tn
    return pl.pallas_call(
        _merge_kernel,
        grid=(m // tm, d_model // tn),
        in_specs=[
            pl.BlockSpec((tm, A_WIDTH), lambda i, j: (i, 0)),
            pl.BlockSpec((tm, B_WIDTH), lambda i, j: (i, 0)),
            pl.BlockSpec((A_WIDTH, tn), lambda i, j: (0, j)),
            pl.BlockSpec((B_WIDTH, tn), lambda i, j: (0, j)),
            pl.BlockSpec((tm, tn), lambda i, j: (i, gate_a0 + j)),
            pl.BlockSpec((tm, tn), lambda i, j: (i, gate_b0 + j)),
        ],
        out_specs=pl.BlockSpec((tm, tn), lambda i, j: (i, j)),
        out_shape=jax.ShapeDtypeStruct((m, d_model), BF16),
        compiler_params=_params(("parallel", "arbitrary")),
        name="merge",
    )(oa, ob, w_a, w_b, vg, vg)


def _out_kernel(mg_ref, w_ref, x_ref, o_ref):
    o_ref[...] = x_ref[...] + jnp.dot(mg_ref[...], w_ref[...], preferred_element_type=F32)


def _out_proj(merged, w_out, x2d, tm, tn):
    m, d = x2d.shape
    return pl.pallas_call(
        _out_kernel,
        grid=(m // tm, d // tn),
        in_specs=[
            pl.BlockSpec((tm, d), lambda i, j: (i, 0)),
            pl.BlockSpec((d, tn), lambda i, j: (0, j)),
            pl.BlockSpec((tm, tn), lambda i, j: (i, j)),
        ],
        out_specs=pl.BlockSpec((tm, tn), lambda i, j: (i, j)),
        out_shape=jax.ShapeDtypeStruct((m, d), F32),
        compiler_params=_params(("parallel", "arbitrary")),
        name="out_proj",
    )(merged, w_out, x2d)


def _qi_column_order():
    lane = np.arange(LANES)
    quarter, r = lane // 32, lane % 32
    dim = r + 32 * (quarter // 2)
    q_cols = np.concatenate([(2 * p + quarter % 2) * IDX_DIM + dim for p in range(IDX_HEADS // 2)])
    return q_cols, dim, (quarter % 2 == 0)


def _layer(x2d, tabs, norm_gain, w_in, a_q_gain, a_k_gain, b_q_gain, b_k_gain,
           lq1, lk1, lq2, lk2, b_subln_gain, w_o_a, w_o_b, w_out, layer, batch, seq):
    d_model = x2d.shape[1]
    cos_h, sin_h, cos_i, sin_i = tabs
    sizes = (A_WIDTH,) * 4 + (IDX_HEADS * IDX_DIM, IDX_DIM, IDX_HEADS) + (B_WIDTH,) * 4 + (d_model,) * 2
    off = np.concatenate([[0], np.cumsum(sizes)])
    w_bf = w_in.astype(BF16)
    seg = lambda i: w_bf[:, off[i]:off[i + 1]]
    (w_qa, w_ka, w_va, w_ga, w_qi, w_ki, w_wi, w_qb, w_kb, w_vb, w_gb, w_ma, w_mb) = [seg(i) for i in range(13)]

    scale = HEAD_DIM ** -0.5 * LOG2E
    w_qk = jnp.concatenate([w_qa, w_ka, w_qb, w_kb], axis=1)
    gain_qk = jnp.concatenate([jnp.tile(a_q_gain * scale, A_HEADS), jnp.tile(a_k_gain, A_HEADS),
                               jnp.tile(b_q_gain * scale, 2 * B_HEADS), jnp.tile(b_k_gain, 2 * B_HEADS)])[None, :]
    q_cols, k_dim, even = _qi_column_order()
    w_qi_p = w_qi[:, q_cols]
    w_k_rep = w_ki[:, k_dim]
    zeros = jnp.zeros_like(w_k_rep)
    w_kw = jnp.concatenate([
        jnp.where(even[None, :], w_k_rep, zeros), jnp.where(even[None, :], zeros, w_k_rep),
        jnp.pad(w_wi, ((0, 0), (0, 2 * LANES - IDX_HEADS)))], axis=1)
    w_vg = jnp.concatenate([w_va, w_vb, w_ga, w_gb, w_ma, w_mb], axis=1)

    m = x2d.shape[0]
    tm, tn = PROJ_TILE
    h = _norm(x2d, norm_gain[None, :], NORM_ROWS)

    tab_spec = pl.BlockSpec((tm, LANES), lambda i, j: (i, 0))
    qkr = _proj_qk(h, w_qk, gain_qk, cos_h, sin_h, tm, tn)
    qi = _proj(h, w_qi_p, (cos_i, sin_i), [tab_spec, tab_spec], _ep_qi,
               jax.ShapeDtypeStruct((IDX_HEADS // 2, m, LANES), BF16),
               pl.BlockSpec((tn // LANES, tm, LANES), lambda i, j: (j, i, 0)), tm, tn, "proj_qi")
    kab, wi = _proj(h, w_kw, (cos_i, sin_i), [tab_spec, tab_spec], _ep_kw,
                    (jax.ShapeDtypeStruct((2, m, LANES), BF16), jax.ShapeDtypeStruct((m, LANES), F32)),
                    (pl.BlockSpec((2, tm, LANES), lambda i, j: (0, i, 0)),
                     pl.BlockSpec((tm, LANES), lambda i, j: (i, 0))), tm, w_kw.shape[1], "proj_kw")
    n_plain = (A_WIDTH + B_WIDTH) // tn
    vg = _proj(h, w_vg, (), [], functools.partial(_ep_vg, n_plain=n_plain, n_silu=n_plain),
               jax.ShapeDtypeStruct((m, w_vg.shape[1]), BF16),
               pl.BlockSpec((tm, tn), lambda i, j: (i, j)), tm, tn, "proj_vg")

    def logit_bound(q_gain, k_gain):
        return HEAD_DIM * 1.02 * jnp.max(jnp.abs(q_gain * scale)) * jnp.max(jnp.abs(k_gain))

    def with_bound(bound, attend):
        return lax.cond(bound <= MAX_LOGIT_BOUND, lambda: attend(True), lambda: attend(False))

    topk = min(TOPK_MAX, seq // 4)
    oa = with_bound(logit_bound(a_q_gain, a_k_gain),
                    lambda f: _attn_a(qi, wi, kab, qkr, vg, batch, seq, *ATTN_A_TILE, topk, f))
    lambda_init = 0.8 - 0.6 * math.exp(-0.3 * layer)
    ob = with_bound(logit_bound(b_q_gain, b_k_gain),
                    lambda f: _attn_b(qkr, vg, b_subln_gain[None, :], lq1[None, :], lk1[None, :], lq2[None, :],
                                      lk2[None, :], batch, seq, ATTN_B_TILE, B_HEADS, lambda_init, f))
    merged = _merge(oa, ob, w_o_a.astype(BF16), w_o_b.astype(BF16), vg, d_model, *OUT_TILE)
    return _out_proj(merged, w_out.astype(BF16), x2d, *OUT_TILE)


def kernel(x, positions, norm_gain, w_in, a_q_gain, a_k_gain, b_q_gain, b_k_gain, lambda_q1, lambda_k1,
           lambda_q2, lambda_k2, b_subln_gain, w_o_a, w_o_b, w_out):
    batch, seq, d_model = x.shape
    m = batch * seq
    pos = positions.astype(F32).reshape(m, 1)
    ang_h = pos * ROPE_THETA ** (-jnp.arange(0, HEAD_DIM, 2, dtype=F32) / HEAD_DIM)
    ang_i = pos * ROPE_THETA ** (-jnp.arange(0, IDX_DIM, 2, dtype=F32) / IDX_DIM)
    ch, sh, ci, si = jnp.cos(ang_h), jnp.sin(ang_h), jnp.cos(ang_i), jnp.sin(ang_i)
    tabs = (jnp.concatenate([ch, ch], axis=1), jnp.concatenate([-sh, sh], axis=1),
            jnp.concatenate([ci] * 4, axis=1), jnp.concatenate([-si, -si, si, si], axis=1))
    x2d = x.reshape(m, d_model)
    for layer in range(norm_gain.shape[0]):
        x2d = _layer(x2d, tabs, norm_gain[layer], w_in[layer], a_q_gain[layer], a_k_gain[layer],
                     b_q_gain[layer], b_k_gain[layer], lambda_q1[layer], lambda_k1[layer],
                     lambda_q2[layer], lambda_k2[layer], b_subln_gain[layer], w_o_a[layer], w_o_b[layer],
                     w_out[layer], layer, batch, seq)
    return x2d.reshape(batch, seq, d_model)
```

```python
import functools
import math

import jax
import jax.numpy as jnp
import numpy as np
from jax import lax
from jax.experimental import pallas as pl
from jax.experimental.pallas import tpu as pltpu

F32 = jnp.float32
BF16 = jnp.bfloat16

HEAD_DIM = 128
A_HEADS = 8
A_WIDTH = A_HEADS * HEAD_DIM
IDX_HEADS = 16
IDX_DIM = 64
TOPK_MAX = 256
B_HEADS = 4
B_WIDTH = B_HEADS * 2 * HEAD_DIM
ROPE_THETA = 10000.0
NORM_EPS = 1e-6
SUBLN_EPS = 1e-5

LANES = 128
INT_MIN = -(2 ** 31)
KEY_NEG_INF = INT_MIN + 0x7FFFFF
NEG_BIG = -1e30
M_INIT = -5e29
LOG2E = math.log2(math.e)
MAX_LOGIT_BOUND = 60.0
SELECT_ROWS = 64
SOFTMAX_ROWS = 32
PROJ_SUB = 256
VMEM_LIMIT_BYTES = 56 * 1024 * 1024

NORM_ROWS = 512
PROJ_TILE = (1024, 1024)
ATTN_A_TILE = (256, 1024)
ATTN_B_TILE = 512
OUT_TILE = (1024, 1024)


def _nt_dot(a, b):
    return lax.dot_general(a, b, (((1,), (1,)), ((), ())), preferred_element_type=F32)


def _tile_lanes(v, width):
    reps = width // LANES
    return v if reps == 1 else jnp.concatenate([v] * reps, axis=1)


def _params(semantics):
    return pltpu.CompilerParams(dimension_semantics=semantics, vmem_limit_bytes=VMEM_LIMIT_BYTES)


def _norm_kernel(x_ref, g_ref, h_ref):
    x = x_ref[...]
    ms = jnp.mean(x * x, axis=-1, keepdims=True)
    h_ref[...] = (x * lax.rsqrt(ms + NORM_EPS) * g_ref[...]).astype(BF16)


def _norm(x2d, gain, tm):
    m, d = x2d.shape
    return pl.pallas_call(
        _norm_kernel,
        grid=(m // tm,),
        in_specs=[pl.BlockSpec((tm, d), lambda i: (i, 0)), pl.BlockSpec((1, d), lambda i: (0, 0))],
        out_specs=pl.BlockSpec((tm, d), lambda i: (i, 0)),
        out_shape=jax.ShapeDtypeStruct((m, d), BF16),
        compiler_params=_params(("parallel",)),
        name="norm",
    )(x2d, gain)


def _proj_body(h_ref, w_ref, *rest, n_aux, epilogue):
    tn = w_ref.shape[1]

    def sub_dot(t):
        return jnp.dot(h_ref[...], w_ref[:, t * PROJ_SUB:(t + 1) * PROJ_SUB], preferred_element_type=F32)

    epilogue(sub_dot, tn // PROJ_SUB, rest[:n_aux], rest[n_aux:])


def _proj(h, w, aux, aux_specs, epilogue, out_shape, out_specs, tm, tn, name):
    m, k = h.shape
    n = w.shape[1]
    return pl.pallas_call(
        functools.partial(_proj_body, n_aux=len(aux), epilogue=epilogue),
        grid=(m // tm, n // tn),
        in_specs=[pl.BlockSpec((tm, k), lambda i, j: (i, 0)),
                  pl.BlockSpec((k, tn), lambda i, j: (0, j))] + aux_specs,
        out_specs=out_specs,
        out_shape=out_shape,
        compiler_params=_params(("parallel", "arbitrary")),
        name=name,
    )(h, w, *aux)


def _rope128(y, cos_full, sin_signed):
    return y * cos_full + pltpu.roll(y, 64, 1) * sin_signed


def _proj_qk_kernel(h_ref, w_ref, g_ref, c_ref, s_ref, o_ref, acc_ref):
    acc_ref[...] = jnp.dot(h_ref[...], w_ref[...], preferred_element_type=F32)
    same_head = (lax.broadcasted_iota(jnp.int32, (PROJ_SUB, PROJ_SUB), 0) // HEAD_DIM
                 == lax.broadcasted_iota(jnp.int32, (PROJ_SUB, PROJ_SUB), 1) // HEAD_DIM)
    head_mean = jnp.where(same_head, 1.0 / HEAD_DIM, 0.0).astype(BF16)

    def sub_tile(t, carry):
        cs = pl.ds(pl.multiple_of(t * PROJ_SUB, PROJ_SUB), PROJ_SUB)
        acc = acc_ref[:, cs]
        ms = jnp.dot((acc * acc).astype(BF16), head_mean, preferred_element_type=F32)
        y = acc * lax.rsqrt(ms + NORM_EPS) * g_ref[:, cs]
        c = c_ref[...]
        s = s_ref[...]
        for u in range(PROJ_SUB // LANES):
            ls = pl.ds(pl.multiple_of(t * PROJ_SUB + u * LANES, LANES), LANES)
            o_ref[:, ls] = _rope128(y[:, u * LANES:(u + 1) * LANES], c, s).astype(BF16)
        return carry

    lax.fori_loop(0, acc_ref.shape[1] // PROJ_SUB, sub_tile, 0)


def _proj_qk(h, w, gain, cos_full, sin_signed, tm, tn):
    m, k = h.shape
    n = w.shape[1]
    tab_spec = pl.BlockSpec((tm, LANES), lambda i, j: (i, 0))
    return pl.pallas_call(
        _proj_qk_kernel,
        grid=(m // tm, n // tn),
        in_specs=[pl.BlockSpec((tm, k), lambda i, j: (i, 0)),
                  pl.BlockSpec((k, tn), lambda i, j: (0, j)),
                  pl.BlockSpec((1, tn), lambda i, j: (0, j)), tab_spec, tab_spec],
        out_specs=pl.BlockSpec((tm, tn), lambda i, j: (i, j)),
        out_shape=jax.ShapeDtypeStruct((m, n), BF16),
        scratch_shapes=[pltpu.VMEM((tm, tn), F32)],
        compiler_params=_params(("parallel", "arbitrary")),
        name="proj_qk",
    )(h, w, gain, cos_full, sin_signed)


def _ep_qi(sub_dot, n_sub, aux, outs):
    c_ref, s_ref = aux
    (o_ref,) = outs
    c = c_ref[...]
    s = s_ref[...]
    per = PROJ_SUB // LANES
    for t in range(n_sub):
        acc = sub_dot(t)
        for u in range(per):
            o_ref[t * per + u] = _rope128(acc[:, u * LANES:(u + 1) * LANES], c, s).astype(BF16)


def _ep_kw(sub_dot, n_sub, aux, outs):
    c_ref, s_ref = aux
    kab_ref, w_ref = outs
    c = c_ref[...]
    s = s_ref[...]
    acc = sub_dot(0)
    for u in range(2):
        kab_ref[u] = _rope128(acc[:, u * LANES:(u + 1) * LANES], c, s).astype(BF16)
    w_ref[...] = sub_dot(1)[:, :LANES] * (IDX_HEADS ** -0.5 * IDX_DIM ** -0.5)


def _ep_vg(sub_dot, n_sub, aux, outs, *, n_plain, n_silu):
    (o_ref,) = outs
    j = pl.program_id(1)

    def store(fn):
        for t in range(n_sub):
            o_ref[:, t * PROJ_SUB:(t + 1) * PROJ_SUB] = fn(sub_dot(t)).astype(BF16)

    @pl.when(j < n_plain)
    def _():
        store(lambda a: a)

    @pl.when(jnp.logical_and(j >= n_plain, j < n_plain + n_silu))
    def _():
        store(lambda a: a * jax.nn.sigmoid(a))

    @pl.when(j >= n_plain + n_silu)
    def _():
        store(jax.nn.sigmoid)


def _f32_key(v):
    bits = pltpu.bitcast(v, jnp.int32)
    return bits ^ ((bits >> 31) & 0x7FFFFFFF)


def _softmax_rows(s_ref, p_ref, alpha_ref, m_ref, l_ref, adjust):
    tq, tk = s_ref.shape
    for r in range(tq // SOFTMAX_ROWS):
        rs = slice(r * SOFTMAX_ROWS, (r + 1) * SOFTMAX_ROWS)
        lanes = [slice(t * LANES, (t + 1) * LANES) for t in range(tk // LANES)]
        s = [adjust(s_ref[rs, ls], rs, ls) for ls in lanes]
        m_prev = m_ref[rs, :]
        m_new = jnp.maximum(m_prev, jnp.max(functools.reduce(jnp.maximum, s), axis=1, keepdims=True))
        alpha = jnp.exp2(m_prev - m_new)
        p = [jnp.exp2(x - m_new) for x in s]
        l_ref[rs, :] = alpha * l_ref[rs, :] + jnp.sum(functools.reduce(jnp.add, p), axis=1, keepdims=True)
        for ls, x in zip(lanes, p):
            p_ref[rs, ls] = x.astype(BF16)
        alpha_ref[rs, :] = alpha
        m_ref[rs, :] = m_new


def _a_kernel(qt_ref, kt_ref, qi_ref, wi_ref, kab_ref, qa_ref, ka_ref, va_ref, ga_ref, o_ref,
              sc_ref, gmax_ref, tau_ref, lo_ref, hi_ref, cand_ref, done_ref, ext_ref, cnt_ref, bias_ref,
              s_ref, p_ref, alpha_ref, m_ref, l_ref, acc_ref, *, tq, tk, topk, rows, bounded):
    step = pl.program_id(1)
    iq = qt_ref[step]
    kj = kt_ref[step]
    last = ((iq + 1) * tq - 1) // tk

    @pl.when(kj == 0)
    def _index_and_select():
        w = wi_ref[...]
        row = iq * tq + lax.broadcasted_iota(jnp.int32, (tq, tk), 0)
        col0 = lax.broadcasted_iota(jnp.int32, (tq, tk), 1)
        gmax_ref[...] = jnp.full(gmax_ref.shape, -jnp.inf, F32)

        def score_tile(j, carry):
            start = pl.multiple_of(j * tk, tk)
            k_even = kab_ref[0, pl.ds(start, tk), :]
            k_odd = kab_ref[1, pl.ds(start, tk), :]
            acc = jnp.zeros((tq, tk), F32)
            for p in range(IDX_HEADS // 2):
                q_p = qi_ref[p]
                s_even = jnp.maximum(_nt_dot(q_p, k_even), 0.0)
                s_odd = jnp.maximum(_nt_dot(q_p, k_odd), 0.0)
                acc = acc + w[:, 2 * p:2 * p + 1] * s_even + w[:, 2 * p + 1:2 * p + 2] * s_odd
            causal = col0 + j * tk <= row
            sc_ref[j] = jnp.where(causal, _f32_key(acc), INT_MIN)
            accm = jnp.where(causal, acc, -jnp.inf)
            half = tk // 2
            for g in range(2):
                gm = gmax_ref[g]
                for t in range(half // LANES):
                    lo = g * half + t * LANES
                    gm = jnp.maximum(gm, accm[:, lo:lo + LANES])
                gmax_ref[g] = gm
            return carry

        lax.fori_loop(0, last + 1, score_tile, 0)

        g0 = gmax_ref[0]
        g1 = gmax_ref[1]
        row_max = jnp.max(jnp.maximum(g0, g1), axis=1, keepdims=True)
        row_low = jnp.min(jnp.minimum(g0, g1), axis=1, keepdims=True)
        lo0 = jnp.broadcast_to(_f32_key(row_low), (tq, LANES))
        hi0 = jnp.broadcast_to(_f32_key(row_max), (tq, LANES)) + 1
        lo_ref[...] = lo0
        hi_ref[...] = hi0
        done_ref[...] = jnp.where(hi0 - lo0 == 1, 1.0, 0.0)
        ext_ref[...] = jnp.zeros((tq, LANES), F32)
        cnt_ref[...] = jnp.zeros((tq, LANES), F32)

        def row_pass(ref, body, init, finish, needed):
            chunks = [pl.ds(rc * rows, rows) for rc in range(tq // rows)]
            flags = [needed(rs) for rs in chunks]
            for rs, flag in zip(chunks, flags):

                @pl.when(flag)
                def _(rs=rs):
                    bound = ref[rs, :]

                    def tile_step(j, carry):
                        keys = sc_ref[j, rs, :]
                        for t in range(tk // LANES):
                            carry = body(carry, keys[:, t * LANES:(t + 1) * LANES], bound)
                        return carry

                    cnt_ref[rs, :] = finish(lax.fori_loop(0, last + 1, tile_step, init))

        def search_cond(c):
            n, stop = c
            return jnp.logical_and(n < 2 * 32 + 2, stop == 0)

        def search_step(c):
            n, _ = c
            stop = (jnp.min(done_ref[...]) > 0.0).astype(jnp.int32)
            lo = lo_ref[...]
            hi = hi_ref[...]
            cand_ref[...] = (lo >> 1) + (hi >> 1) + (lo & hi & 1)
            row_pass(cand_ref, lambda acc, keys, cand: acc + jnp.where(keys >= cand, 1.0, 0.0),
                     jnp.zeros((rows, LANES), F32), lambda acc: acc,
                     lambda rs: jnp.min(done_ref[rs, :]) == 0.0)
            cnt = jnp.sum(cnt_ref[...], axis=1, keepdims=True)
            cand = cand_ref[...]
            active = done_ref[...] == 0.0
            enough = cnt >= float(topk)
            lo = jnp.where(jnp.logical_and(active, enough), cand, lo)
            hi = jnp.where(jnp.logical_and(active, jnp.logical_not(enough)), cand, hi)
            lo_ref[...] = lo
            hi_ref[...] = hi
            extract = jnp.logical_and(active, cnt == float(topk - 1))
            ext_ref[...] = jnp.where(extract, 1.0, ext_ref[...])
            finished = jnp.logical_or(jnp.logical_or(cnt == float(topk), extract), hi - lo == 1)
            done_ref[...] = jnp.where(jnp.logical_and(active, finished), 1.0, done_ref[...])
            return n + 1, stop

        lax.while_loop(search_cond, search_step, (jnp.int32(0), jnp.int32(0)))

        def key_to_f32(keys):
            return pltpu.bitcast(keys ^ ((keys >> 31) & 0x7FFFFFFF), F32)

        row_pass(hi_ref, lambda acc, keys, hi: jnp.maximum(acc, jnp.where(keys < hi, keys, KEY_NEG_INF)),
                 jnp.full((rows, LANES), KEY_NEG_INF, jnp.int32), key_to_f32,
                 lambda rs: jnp.max(ext_ref[rs, :]) > 0.0)
        below = jnp.broadcast_to(_f32_key(jnp.max(cnt_ref[...], axis=1, keepdims=True)), (tq, LANES))
        thr = jnp.where(ext_ref[...] > 0.0, below, lo_ref[...])
        tau_ref[...] = jnp.maximum(thr, INT_MIN + 1)

        m_ref[...] = jnp.full(m_ref.shape, M_INIT, F32)
        l_ref[...] = jnp.zeros(l_ref.shape, F32)
        acc_ref[...] = jnp.zeros(acc_ref.shape, F32)

    tau = tau_ref[...]
    for t in range(tk // LANES):
        ls = slice(t * LANES, (t + 1) * LANES)
        bias_ref[:, ls] = jnp.where(sc_ref[kj, :, ls] >= tau, 0.0, NEG_BIG)

    def add_bias(x, rs, ls):
        return x + bias_ref[rs, ls]

    heads = [slice(h * HEAD_DIM, (h + 1) * HEAD_DIM) for h in range(A_HEADS)]
    if bounded:
        bias = bias_ref[...]
        for h, hs in enumerate(heads):
            p = jnp.exp2(_nt_dot(qa_ref[:, hs], ka_ref[:, hs]) + bias)
            l_ref[h] = l_ref[h] + jnp.sum(p, axis=1, keepdims=True)
            acc_ref[h] = acc_ref[h] + jnp.dot(p.astype(BF16), va_ref[:, hs], preferred_element_type=F32)
    else:
        for h, hs in enumerate(heads):
            s_ref[h] = _nt_dot(qa_ref[:, hs], ka_ref[:, hs])
        for h in range(A_HEADS):
            _softmax_rows(s_ref.at[h], p_ref.at[h], alpha_ref.at[h], m_ref.at[h], l_ref.at[h], add_bias)
        for h, hs in enumerate(heads):
            acc_ref[h] = alpha_ref[h] * acc_ref[h] + jnp.dot(p_ref[h], va_ref[:, hs],
                                                             preferred_element_type=F32)

    @pl.when(kj == last)
    def _finish():
        for h in range(A_HEADS):
            hs = slice(h * HEAD_DIM, (h + 1) * HEAD_DIM)
            o = acc_ref[h] / l_ref[h]
            o_ref[:, hs] = (o * ga_ref[:, hs].astype(F32)).astype(BF16)


def _causal_steps(nq, tq, tk):
    qt, kt = [], []
    for i in range(nq):
        for j in range(((i + 1) * tq - 1) // tk + 1):
            qt.append(i)
            kt.append(j)
    return jnp.asarray(qt, jnp.int32), jnp.asarray(kt, jnp.int32)


def _attn_a(qi, wi, kab, qkr, vg, batch, seq, tq, tk, topk, bounded):
    m = batch * seq
    nq = seq // tq
    nk = seq // tk
    qt, kt = _causal_steps(nq, tq, tk)

    def qrow(b, s, qt_ref, kt_ref):
        return b * nq + qt_ref[s]

    def krow(b, s, qt_ref, kt_ref):
        return b * nk + kt_ref[s]

    grid_spec = pltpu.PrefetchScalarGridSpec(
        num_scalar_prefetch=2,
        grid=(batch, int(qt.shape[0])),
        in_specs=[
            pl.BlockSpec((IDX_HEADS // 2, tq, LANES), lambda b, s, q, k: (0, qrow(b, s, q, k), 0)),
            pl.BlockSpec((tq, LANES), lambda b, s, q, k: (qrow(b, s, q, k), 0)),
            pl.BlockSpec((2, seq, LANES), lambda b, s, q, k: (0, b, 0),
                         pipeline_mode=pl.Buffered(1)),
            pl.BlockSpec((tq, A_WIDTH), lambda b, s, q, k: (qrow(b, s, q, k), 0)),
            pl.BlockSpec((tk, A_WIDTH), lambda b, s, q, k: (krow(b, s, q, k), 1)),
            pl.BlockSpec((tk, A_WIDTH), lambda b, s, q, k: (krow(b, s, q, k), 0)),
            pl.BlockSpec((tq, A_WIDTH), lambda b, s, q, k: (qrow(b, s, q, k), 2)),
        ],
        out_specs=pl.BlockSpec((tq, A_WIDTH), lambda b, s, q, k: (qrow(b, s, q, k), 0)),
        scratch_shapes=[
            pltpu.VMEM((nk, tq, tk), jnp.int32),
            pltpu.VMEM((2, tq, LANES), F32),
            pltpu.VMEM((tq, LANES), jnp.int32),
            pltpu.VMEM((tq, LANES), jnp.int32),
            pltpu.VMEM((tq, LANES), jnp.int32),
            pltpu.VMEM((tq, LANES), jnp.int32),
            pltpu.VMEM((tq, LANES), F32),
            pltpu.VMEM((tq, LANES), F32),
            pltpu.VMEM((tq, LANES), F32),
            pltpu.VMEM((tq, tk), F32),
            pltpu.VMEM((A_HEADS, tq, tk), F32),
            pltpu.VMEM((A_HEADS, tq, tk), BF16),
            pltpu.VMEM((A_HEADS, tq, LANES), F32),
            pltpu.VMEM((A_HEADS, tq, LANES), F32),
            pltpu.VMEM((A_HEADS, tq, LANES), F32),
            pltpu.VMEM((A_HEADS, tq, HEAD_DIM), F32),
        ],
    )
    return pl.pallas_call(
        functools.partial(_a_kernel, tq=tq, tk=tk, topk=topk, rows=SELECT_ROWS, bounded=bounded),
        grid_spec=grid_spec,
        out_shape=jax.ShapeDtypeStruct((m, A_WIDTH), BF16),
        compiler_params=_params(("parallel", "arbitrary")),
        name="attn_a",
    )(qt, kt, qi, wi, kab, qkr, qkr, vg, vg)


def _b_kernel(qt_ref, kt_ref, q_ref, k_ref, v_ref, sg_ref, gain_ref, lq1_ref, lk1_ref, lq2_ref, lk2_ref,
              o_ref, s_ref, p_ref, alpha_ref, m_ref, l_ref, acc_ref, *, tq, heads, lambda_init, bounded):
    step = pl.program_id(2)
    iq = qt_ref[step]
    kj = kt_ref[step]
    hw = 2 * HEAD_DIM
    chains = [(hh, c) for hh in range(heads) for c in range(2)]

    @pl.when(kj == 0)
    def _init():
        m_ref[...] = jnp.full(m_ref.shape, M_INIT, F32)
        l_ref[...] = jnp.zeros(l_ref.shape, F32)
        acc_ref[...] = jnp.zeros(acc_ref.shape, F32)

    def causal_mask(x, rs, ls):
        row = rs.start + lax.broadcasted_iota(jnp.int32, x.shape, 0)
        col = ls.start + lax.broadcasted_iota(jnp.int32, x.shape, 1)
        return jnp.where(col <= row, x, NEG_BIG)

    def attend(diagonal):
        adjust = causal_mask if diagonal else (lambda x, rs, ls: x)
        qk = lambda hh, c: (slice(hh * hw + c * HEAD_DIM, hh * hw + (c + 1) * HEAD_DIM),) * 2
        if bounded:
            if diagonal:
                keep = (lax.broadcasted_iota(jnp.int32, (tq, tq), 1)
                        <= lax.broadcasted_iota(jnp.int32, (tq, tq), 0))
            for i, (hh, c) in enumerate(chains):
                qs, ks = qk(hh, c)
                p = jnp.exp2(_nt_dot(q_ref[:, qs], k_ref[:, ks]))
                if diagonal:
                    p = jnp.where(keep, p, 0.0)
                l_ref[i] = l_ref[i] + jnp.sum(p, axis=1, keepdims=True)
                acc_ref[i] = acc_ref[i] + jnp.dot(p.astype(BF16), v_ref[:, hh * hw:(hh + 1) * hw],
                                                  preferred_element_type=F32)
        else:
            for i, (hh, c) in enumerate(chains):
                qs, ks = qk(hh, c)
                s_ref[i] = _nt_dot(q_ref[:, qs], k_ref[:, ks])
            for i in range(len(chains)):
                _softmax_rows(s_ref.at[i], p_ref.at[i], alpha_ref.at[i], m_ref.at[i], l_ref.at[i], adjust)
            for i, (hh, c) in enumerate(chains):
                pv = jnp.dot(p_ref[i], v_ref[:, hh * hw:(hh + 1) * hw], preferred_element_type=F32)
                alpha = alpha_ref[i]
                for t in range(hw // LANES):
                    ls = slice(t * LANES, (t + 1) * LANES)
                    acc_ref[i, :, ls] = alpha * acc_ref[i, :, ls] + pv[:, ls]

    @pl.when(kj < iq)
    def _below_diagonal():
        attend(False)

    @pl.when(kj == iq)
    def _diagonal_and_finish():
        attend(True)
        lam = (jnp.exp(jnp.sum(lq1_ref[...] * lk1_ref[...], axis=1, keepdims=True))
               - jnp.exp(jnp.sum(lq2_ref[...] * lk2_ref[...], axis=1, keepdims=True))
               + lambda_init)
        for hh in range(heads):
            o0 = acc_ref[2 * hh] / _tile_lanes(l_ref[2 * hh], hw)
            o1 = acc_ref[2 * hh + 1] / _tile_lanes(l_ref[2 * hh + 1], hw)
            o = o0 - lam * o1
            ms = jnp.mean(o * o, axis=-1, keepdims=True)
            y = o * lax.rsqrt(ms + SUBLN_EPS) * gain_ref[...] * (1.0 - lambda_init)
            hs = slice(hh * hw, (hh + 1) * hw)
            o_ref[:, hs] = (y * sg_ref[:, hs].astype(F32)).astype(BF16)


def _attn_b(qkr, vg, gain, lq1, lk1, lq2, lk2, batch, seq, tq, heads, lambda_init, bounded):
    m = batch * seq
    nq = seq // tq
    bw = heads * 2 * HEAD_DIM
    groups = B_HEADS // heads
    chains = 2 * heads
    qt, kt = _causal_steps(nq, tq, tq)

    def qrow(b, s, qt_ref):
        return b * nq + qt_ref[s]

    vec = pl.BlockSpec((1, HEAD_DIM), lambda b, h, s, q, k: (0, 0))
    grid_spec = pltpu.PrefetchScalarGridSpec(
        num_scalar_prefetch=2,
        grid=(batch, groups, int(qt.shape[0])),
        in_specs=[
            pl.BlockSpec((tq, bw), lambda b, h, s, q, k: (qrow(b, s, q), 2 * groups + h)),
            pl.BlockSpec((tq, bw), lambda b, h, s, q, k: (qrow(b, s, k), 3 * groups + h)),
            pl.BlockSpec((tq, bw), lambda b, h, s, q, k: (qrow(b, s, k), groups + h)),
            pl.BlockSpec((tq, bw), lambda b, h, s, q, k: (qrow(b, s, q), 3 * groups + h)),
            pl.BlockSpec((1, 2 * HEAD_DIM), lambda b, h, s, q, k: (0, 0)),
            vec, vec, vec, vec,
        ],
        out_specs=pl.BlockSpec((tq, bw), lambda b, h, s, q, k: (qrow(b, s, q), h)),
        scratch_shapes=[
            pltpu.VMEM((chains, tq, tq), F32),
            pltpu.VMEM((chains, tq, tq), BF16),
            pltpu.VMEM((chains, tq, LANES), F32),
            pltpu.VMEM((chains, tq, LANES), F32),
            pltpu.VMEM((chains, tq, LANES), F32),
            pltpu.VMEM((chains, tq, 2 * HEAD_DIM), F32),
        ],
    )
    return pl.pallas_call(
        functools.partial(_b_kernel, tq=tq, heads=heads, lambda_init=lambda_init, bounded=bounded),
        grid_spec=grid_spec,
        out_shape=jax.ShapeDtypeStruct((m, B_WIDTH), BF16),
        compiler_params=_params(("parallel", "parallel", "arbitrary")),
        name="attn_b",
    )(qt, kt, qkr, qkr, vg, vg, gain, lq1, lk1, lq2, lk2)


def _merge_kernel(oa_ref, ob_ref, wa_ref, wb_ref, ma_ref, mb_ref, o_ref):
    ya = jnp.dot(oa_ref[...], wa_ref[...], preferred_element_type=F32)
    yb = jnp.dot(ob_ref[...], wb_ref[...], preferred_element_type=F32)
    o_ref[...] = (ma_ref[...].astype(F32) * ya + mb_ref[...].astype(F32) * yb).astype(BF16)


def _merge(oa, ob, w_a, w_b, vg, d_model, tm, tn):
    m = oa.shape[0]
    gate_a0 = (A_WIDTH + B_WIDTH) * 2 // tn
    gate_b0 = gate_a0 + d_model // tn
    return pl.pallas_call(
        _merge_kernel,
        grid=(m // tm, d_model // tn),
        in_specs=[
            pl.BlockSpec((tm, A_WIDTH), lambda i, j: (i, 0)),
            pl.BlockSpec((tm, B_WIDTH), lambda i, j: (i, 0)),
            pl.BlockSpec((A_WIDTH, tn), lambda i, j: (0, j)),
            pl.BlockSpec((B_WIDTH, tn), lambda i, j: (0, j)),
            pl.BlockSpec((tm, tn), lambda i, j: (i, gate_a0 + j)),
            pl.BlockSpec((tm, tn), lambda i, j: (i, gate_b0 + j)),
        ],
        out_specs=pl.BlockSpec((tm, tn), lambda i, j: (i, j)),
        out_shape=jax.ShapeDtypeStruct((m, d_model), BF16),
        compiler_params=_params(("parallel", "arbitrary")),
        name="merge",
    )(oa, ob, w_a, w_b, vg, vg)


def _out_kernel(mg_ref, w_ref, x_ref, o_ref):
    o_ref[...] = x_ref[...] + jnp.dot(mg_ref[...], w_ref[...], preferred_element_type=F32)


def _out_proj(merged, w_out, x2d, tm, tn):
    m, d = x2d.shape
    return pl.pallas_call(
        _out_kernel,
        grid=(m // tm, d // tn),
        in_specs=[
            pl.BlockSpec((tm, d), lambda i, j: (i, 0)),
            pl.BlockSpec((d, tn), lambda i, j: (0, j)),
            pl.BlockSpec((tm, tn), lambda i, j: (i, j)),
        ],
        out_specs=pl.BlockSpec((tm, tn), lambda i, j: (i, j)),
        out_shape=jax.ShapeDtypeStruct((m, d), F32),
        compiler_params=_params(("parallel", "arbitrary")),
        name="out_proj",
    )(merged, w_out, x2d)


def _qi_column_order():
    lane = np.arange(LANES)
    quarter, r = lane // 32, lane % 32
    dim = r + 32 * (quarter // 2)
    q_cols = np.concatenate([(2 * p + quarter % 2) * IDX_DIM + dim for p in range(IDX_HEADS // 2)])
    return q_cols, dim, (quarter % 2 == 0)


def _layer(x2d, tabs, norm_gain, w_in, a_q_gain, a_k_gain, b_q_gain, b_k_gain,
           lq1, lk1, lq2, lk2, b_subln_gain, w_o_a, w_o_b, w_out, layer, batch, seq):
    d_model = x2d.shape[1]
    cos_h, sin_h, cos_i, sin_i = tabs
    sizes = (A_WIDTH,) * 4 + (IDX_HEADS * IDX_DIM, IDX_DIM, IDX_HEADS) + (B_WIDTH,) * 4 + (d_model,) * 2
    off = np.concatenate([[0], np.cumsum(sizes)])
    w_bf = w_in.astype(BF16)
    seg = lambda i: w_bf[:, off[i]:off[i + 1]]
    (w_qa, w_ka, w_va, w_ga, w_qi, w_ki, w_wi, w_qb, w_kb, w_vb, w_gb, w_ma, w_mb) = [seg(i) for i in range(13)]

    scale = HEAD_DIM ** -0.5 * LOG2E
    w_qk = jnp.concatenate([w_qa, w_ka, w_qb, w_kb], axis=1)
    gain_qk = jnp.concatenate([jnp.tile(a_q_gain * scale, A_HEADS), jnp.tile(a_k_gain, A_HEADS),
                               jnp.tile(b_q_gain * scale, 2 * B_HEADS), jnp.tile(b_k_gain, 2 * B_HEADS)])[None, :]
    q_cols, k_dim, even = _qi_column_order()
    w_qi_p = w_qi[:, q_cols]
    w_k_rep = w_ki[:, k_dim]
    zeros = jnp.zeros_like(w_k_rep)
    w_kw = jnp.concatenate([
        jnp.where(even[None, :], w_k_rep, zeros), jnp.where(even[None, :], zeros, w_k_rep),
        jnp.pad(w_wi, ((0, 0), (0, 2 * LANES - IDX_HEADS)))], axis=1)
    w_vg = jnp.concatenate([w_va, w_vb, w_ga, w_gb, w_ma, w_mb], axis=1)

    m = x2d.shape[0]
    tm, tn = PROJ_TILE
    h = _norm(x2d, norm_gain[None, :], NORM_ROWS)

    tab_spec = pl.BlockSpec((tm, LANES), lambda i, j: (i, 0))
    qkr = _proj_qk(h, w_qk, gain_qk, cos_h, sin_h, tm, tn)
    qi = _proj(h, w_qi_p, (cos_i, sin_i), [tab_spec, tab_spec], _ep_qi,
               jax.ShapeDtypeStruct((IDX_HEADS // 2, m, LANES), BF16),
               pl.BlockSpec((tn // LANES, tm, LANES), lambda i, j: (j, i, 0)), tm, tn, "proj_qi")
    kab, wi = _proj(h, w_kw, (cos_i, sin_i), [tab_spec, tab_spec], _ep_kw,
                    (jax.ShapeDtypeStruct((2, m, LANES), BF16), jax.ShapeDtypeStruct((m, LANES), F32)),
                    (pl.BlockSpec((2, tm, LANES), lambda i, j: (0, i, 0)),
                     pl.BlockSpec((tm, LANES), lambda i, j: (i, 0))), tm, w_kw.shape[1], "proj_kw")
    n_plain = (A_WIDTH + B_WIDTH) // tn
    vg = _proj(h, w_vg, (), [], functools.partial(_ep_vg, n_plain=n_plain, n_silu=n_plain),
               jax.ShapeDtypeStruct((m, w_vg.shape[1]), BF16),
               pl.BlockSpec((tm, tn), lambda i, j: (i, j)), tm, tn, "proj_vg")

    def logit_bound(q_gain, k_gain):
        return HEAD_DIM * 1.02 * jnp.max(jnp.abs(q_gain * scale)) * jnp.max(jnp.abs(k_gain))

    def with_bound(bound, attend):
        return lax.cond(bound <= MAX_LOGIT_BOUND, lambda: attend(True), lambda: attend(False))

    topk = min(TOPK_MAX, seq // 4)
    oa = with_bound(logit_bound(a_q_gain, a_k_gain),
                    lambda f: _attn_a(qi, wi, kab, qkr, vg, batch, seq, *ATTN_A_TILE, topk, f))
    lambda_init = 0.8 - 0.6 * math.exp(-0.3 * layer)
    ob = with_bound(logit_bound(b_q_gain, b_k_gain),
                    lambda f: _attn_b(qkr, vg, b_subln_gain[None, :], lq1[None, :], lk1[None, :], lq2[None, :],
                                      lk2[None, :], batch, seq, ATTN_B_TILE, B_HEADS, lambda_init, f))
    merged = _merge(oa, ob, w_o_a.astype(BF16), w_o_b.astype(BF16), vg, d_model, *OUT_TILE)
    return _out_proj(merged, w_out.astype(BF16), x2d, *OUT_TILE)


def kernel(x, positions, norm_gain, w_in, a_q_gain, a_k_gain, b_q_gain, b_k_gain, lambda_q1, lambda_k1,
           lambda_q2, lambda_k2, b_subln_gain, w_o_a, w_o_b, w_out):
    batch, seq, d_model = x.shape
    m = batch * seq
    pos = positions.astype(F32).reshape(m, 1)
    ang_h = pos * ROPE_THETA ** (-jnp.arange(0, HEAD_DIM, 2, dtype=F32) / HEAD_DIM)
    ang_i = pos * ROPE_THETA ** (-jnp.arange(0, IDX_DIM, 2, dtype=F32) / IDX_DIM)
    ch, sh, ci, si = jnp.cos(ang_h), jnp.sin(ang_h), jnp.cos(ang_i), jnp.sin(ang_i)
    tabs = (jnp.concatenate([ch, ch], axis=1), jnp.concatenate([-sh, sh], axis=1),
            jnp.concatenate([ci] * 4, axis=1), jnp.concatenate([-si, -si, si, si], axis=1))
    x2d = x.reshape(m, d_model)
    for layer in range(norm_gain.shape[0]):
        x2d = _layer(x2d, tabs, norm_gain[layer], w_in[layer], a_q_gain[layer], a_k_gain[layer],
                     b_q_gain[layer], b_k_gain[layer], lambda_q1[layer], lambda_k1[layer],
                     lambda_q2[layer], lambda_k2[layer], b_subln_gain[layer], w_o_a[layer], w_o_b[layer],
                     w_out[layer], layer, batch, seq)
    return x2d.reshape(batch, seq, d_model)
```

```python
import functools
import math

import jax
import jax.numpy as jnp
import numpy as np
from jax import lax
from jax.experimental import pallas as pl
from jax.experimental.pallas import tpu as pltpu

F32 = jnp.float32
BF16 = jnp.bfloat16

HEAD_DIM = 128
A_HEADS = 8
A_WIDTH = A_HEADS * HEAD_DIM
IDX_HEADS = 16
IDX_DIM = 64
TOPK_MAX = 256
B_HEADS = 4
B_WIDTH = B_HEADS * 2 * HEAD_DIM
ROPE_THETA = 10000.0
NORM_EPS = 1e-6
SUBLN_EPS = 1e-5

LANES = 128
INT_MIN = -(2 ** 31)
KEY_NEG_INF = INT_MIN + 0x7FFFFF
NEG_BIG = -1e30
M_INIT = -5e29
LOG2E = math.log2(math.e)
MAX_LOGIT_BOUND = 60.0
SELECT_ROWS = 64
SOFTMAX_ROWS = 32
PROJ_SUB = 256
VMEM_LIMIT_BYTES = 56 * 1024 * 1024

NORM_ROWS = 512
PROJ_TILE = (1024, 2048)
ATTN_A_TILE = (256, 1024)
ATTN_B_TILE = 512
OUT_TILE = (1024, 1024)


def _nt_dot(a, b):
    return lax.dot_general(a, b, (((1,), (1,)), ((), ())), preferred_element_type=F32)


def _tile_lanes(v, width):
    reps = width // LANES
    return v if reps == 1 else jnp.concatenate([v] * reps, axis=1)


def _params(semantics):
    return pltpu.CompilerParams(dimension_semantics=semantics, vmem_limit_bytes=VMEM_LIMIT_BYTES)


def _norm_kernel(x_ref, g_ref, h_ref):
    x = x_ref[...]
    ms = jnp.mean(x * x, axis=-1, keepdims=True)
    h_ref[...] = (x * lax.rsqrt(ms + NORM_EPS) * g_ref[...]).astype(BF16)


def _norm(x2d, gain, tm):
    m, d = x2d.shape
    return pl.pallas_call(
        _norm_kernel,
        grid=(m // tm,),
        in_specs=[pl.BlockSpec((tm, d), lambda i: (i, 0)), pl.BlockSpec((1, d), lambda i: (0, 0))],
        out_specs=pl.BlockSpec((tm, d), lambda i: (i, 0)),
        out_shape=jax.ShapeDtypeStruct((m, d), BF16),
        compiler_params=_params(("parallel",)),
        name="norm",
    )(x2d, gain)


def _proj_body(h_ref, w_ref, *rest, n_aux, epilogue):
    tn = w_ref.shape[1]

    def sub_dot(t):
        return jnp.dot(h_ref[...], w_ref[:, t * PROJ_SUB:(t + 1) * PROJ_SUB], preferred_element_type=F32)

    epilogue(sub_dot, tn // PROJ_SUB, rest[:n_aux], rest[n_aux:])


def _proj(h, w, aux, aux_specs, epilogue, out_shape, out_specs, tm, tn, name):
    m, k = h.shape
    n = w.shape[1]
    return pl.pallas_call(
        functools.partial(_proj_body, n_aux=len(aux), epilogue=epilogue),
        grid=(m // tm, n // tn),
        in_specs=[pl.BlockSpec((tm, k), lambda i, j: (i, 0)),
                  pl.BlockSpec((k, tn), lambda i, j: (0, j))] + aux_specs,
        out_specs=out_specs,
        out_shape=out_shape,
        compiler_params=_params(("parallel", "arbitrary")),
        name=name,
    )(h, w, *aux)


def _rope128(y, cos_full, sin_signed):
    return y * cos_full + pltpu.roll(y, 64, 1) * sin_signed


def _proj_qk_kernel(h_ref, w_ref, g_ref, c_ref, s_ref, o_ref, acc_ref):
    acc_ref[...] = jnp.dot(h_ref[...], w_ref[...], preferred_element_type=F32)
    same_head = (lax.broadcasted_iota(jnp.int32, (PROJ_SUB, PROJ_SUB), 0) // HEAD_DIM
                 == lax.broadcasted_iota(jnp.int32, (PROJ_SUB, PROJ_SUB), 1) // HEAD_DIM)
    head_mean = jnp.where(same_head, 1.0 / HEAD_DIM, 0.0).astype(BF16)

    def sub_tile(t, carry):
        cs = pl.ds(pl.multiple_of(t * PROJ_SUB, PROJ_SUB), PROJ_SUB)
        acc = acc_ref[:, cs]
        ms = jnp.dot((acc * acc).astype(BF16), head_mean, preferred_element_type=F32)
        y = acc * lax.rsqrt(ms + NORM_EPS) * g_ref[:, cs]
        c = c_ref[...]
        s = s_ref[...]
        for u in range(PROJ_SUB // LANES):
            ls = pl.ds(pl.multiple_of(t * PROJ_SUB + u * LANES, LANES), LANES)
            o_ref[:, ls] = _rope128(y[:, u * LANES:(u + 1) * LANES], c, s).astype(BF16)
        return carry

    lax.fori_loop(0, acc_ref.shape[1] // PROJ_SUB, sub_tile, 0)


def _proj_qk(h, w, gain, cos_full, sin_signed, tm, tn):
    m, k = h.shape
    n = w.shape[1]
    tab_spec = pl.BlockSpec((tm, LANES), lambda i, j: (i, 0))
    return pl.pallas_call(
        _proj_qk_kernel,
        grid=(m // tm, n // tn),
        in_specs=[pl.BlockSpec((tm, k), lambda i, j: (i, 0)),
                  pl.BlockSpec((k, tn), lambda i, j: (0, j)),
                  pl.BlockSpec((1, tn), lambda i, j: (0, j)), tab_spec, tab_spec],
        out_specs=pl.BlockSpec((tm, tn), lambda i, j: (i, j)),
        out_shape=jax.ShapeDtypeStruct((m, n), BF16),
        scratch_shapes=[pltpu.VMEM((tm, tn), F32)],
        compiler_params=_params(("parallel", "arbitrary")),
        name="proj_qk",
    )(h, w, gain, cos_full, sin_signed)


def _ep_qi(sub_dot, n_sub, aux, outs):
    c_ref, s_ref = aux
    (o_ref,) = outs
    c = c_ref[...]
    s = s_ref[...]
    per = PROJ_SUB // LANES
    for t in range(n_sub):
        acc = sub_dot(t)
        for u in range(per):
            o_ref[t * per + u] = _rope128(acc[:, u * LANES:(u + 1) * LANES], c, s).astype(BF16)


def _ep_kw(sub_dot, n_sub, aux, outs):
    c_ref, s_ref = aux
    kab_ref, w_ref = outs
    c = c_ref[...]
    s = s_ref[...]
    acc = sub_dot(0)
    for u in range(2):
        kab_ref[u] = _rope128(acc[:, u * LANES:(u + 1) * LANES], c, s).astype(BF16)
    w_ref[...] = sub_dot(1)[:, :LANES] * (IDX_HEADS ** -0.5 * IDX_DIM ** -0.5)


def _ep_vg(sub_dot, n_sub, aux, outs, *, n_plain, n_silu):
    (o_ref,) = outs
    j = pl.program_id(1)

    def store(fn):
        for t in range(n_sub):
            o_ref[:, t * PROJ_SUB:(t + 1) * PROJ_SUB] = fn(sub_dot(t)).astype(BF16)

    @pl.when(j < n_plain)
    def _():
        store(lambda a: a)

    @pl.when(jnp.logical_and(j >= n_plain, j < n_plain + n_silu))
    def _():
        store(lambda a: a * jax.nn.sigmoid(a))

    @pl.when(j >= n_plain + n_silu)
    def _():
        store(jax.nn.sigmoid)


def _f32_key(v):
    bits = pltpu.bitcast(v, jnp.int32)
    return bits ^ ((bits >> 31) & 0x7FFFFFFF)


def _softmax_rows(s_ref, p_ref, alpha_ref, m_ref, l_ref, adjust):
    tq, tk = s_ref.shape
    for r in range(tq // SOFTMAX_ROWS):
        rs = slice(r * SOFTMAX_ROWS, (r + 1) * SOFTMAX_ROWS)
        lanes = [slice(t * LANES, (t + 1) * LANES) for t in range(tk // LANES)]
        s = [adjust(s_ref[rs, ls], rs, ls) for ls in lanes]
        m_prev = m_ref[rs, :]
        m_new = jnp.maximum(m_prev, jnp.max(functools.reduce(jnp.maximum, s), axis=1, keepdims=True))
        alpha = jnp.exp2(m_prev - m_new)
        p = [jnp.exp2(x - m_new) for x in s]
        l_ref[rs, :] = alpha * l_ref[rs, :] + jnp.sum(functools.reduce(jnp.add, p), axis=1, keepdims=True)
        for ls, x in zip(lanes, p):
            p_ref[rs, ls] = x.astype(BF16)
        alpha_ref[rs, :] = alpha
        m_ref[rs, :] = m_new


def _a_kernel(qt_ref, kt_ref, qi_ref, wi_ref, kab_ref, qa_ref, ka_ref, va_ref, ga_ref, o_ref,
              sc_ref, gmax_ref, tau_ref, lo_ref, hi_ref, cand_ref, done_ref, ext_ref, cnt_ref, bias_ref,
              s_ref, p_ref, alpha_ref, m_ref, l_ref, acc_ref, *, tq, tk, topk, rows, bounded):
    step = pl.program_id(1)
    iq = qt_ref[step]
    kj = kt_ref[step]
    last = ((iq + 1) * tq - 1) // tk

    @pl.when(kj == 0)
    def _index_and_select():
        w = wi_ref[...]
        row = iq * tq + lax.broadcasted_iota(jnp.int32, (tq, tk), 0)
        col0 = lax.broadcasted_iota(jnp.int32, (tq, tk), 1)
        gmax_ref[...] = jnp.full(gmax_ref.shape, -jnp.inf, F32)

        def score_tile(j, carry):
            start = pl.multiple_of(j * tk, tk)
            k_even = kab_ref[0, pl.ds(start, tk), :]
            k_odd = kab_ref[1, pl.ds(start, tk), :]
            acc = jnp.zeros((tq, tk), F32)
            for p in range(IDX_HEADS // 2):
                q_p = qi_ref[p]
                s_even = jnp.maximum(_nt_dot(q_p, k_even), 0.0)
                s_odd = jnp.maximum(_nt_dot(q_p, k_odd), 0.0)
                acc = acc + w[:, 2 * p:2 * p + 1] * s_even + w[:, 2 * p + 1:2 * p + 2] * s_odd
            causal = col0 + j * tk <= row
            sc_ref[j] = jnp.where(causal, _f32_key(acc), INT_MIN)
            accm = jnp.where(causal, acc, -jnp.inf)
            half = tk // 2
            for g in range(2):
                gm = gmax_ref[g]
                for t in range(half // LANES):
                    lo = g * half + t * LANES
                    gm = jnp.maximum(gm, accm[:, lo:lo + LANES])
                gmax_ref[g] = gm
            return carry

        lax.fori_loop(0, last + 1, score_tile, 0)

        g0 = gmax_ref[0]
        g1 = gmax_ref[1]
        row_max = jnp.max(jnp.maximum(g0, g1), axis=1, keepdims=True)
        row_low = jnp.min(jnp.minimum(g0, g1), axis=1, keepdims=True)
        lo0 = jnp.broadcast_to(_f32_key(row_low), (tq, LANES))
        hi0 = jnp.broadcast_to(_f32_key(row_max), (tq, LANES)) + 1
        lo_ref[...] = lo0
        hi_ref[...] = hi0
        done_ref[...] = jnp.where(hi0 - lo0 == 1, 1.0, 0.0)
        ext_ref[...] = jnp.zeros((tq, LANES), F32)
        cnt_ref[...] = jnp.zeros((tq, LANES), F32)

        def row_pass(ref, body, init, finish, needed):
            chunks = [pl.ds(rc * rows, rows) for rc in range(tq // rows)]
            flags = [needed(rs) for rs in chunks]
            for rs, flag in zip(chunks, flags):

                @pl.when(flag)
                def _(rs=rs):
                    bound = ref[rs, :]

                    def tile_step(j, carry):
                        keys = sc_ref[j, rs, :]
                        for t in range(tk // LANES):
                            carry = body(carry, keys[:, t * LANES:(t + 1) * LANES], bound)
                        return carry

                    cnt_ref[rs, :] = finish(lax.fori_loop(0, last + 1, tile_step, init))

        def search_cond(c):
            n, stop = c
            return jnp.logical_and(n < 2 * 32 + 2, stop == 0)

        def search_step(c):
            n, _ = c
            stop = (jnp.min(done_ref[...]) > 0.0).astype(jnp.int32)
            lo = lo_ref[...]
            hi = hi_ref[...]
            cand_ref[...] = (lo >> 1) + (hi >> 1) + (lo & hi & 1)
            row_pass(cand_ref, lambda acc, keys, cand: acc + jnp.where(keys >= cand, 1.0, 0.0),
                     jnp.zeros((rows, LANES), F32), lambda acc: acc,
                     lambda rs: jnp.min(done_ref[rs, :]) == 0.0)
            cnt = jnp.sum(cnt_ref[...], axis=1, keepdims=True)
            cand = cand_ref[...]
            active = done_ref[...] == 0.0
            enough = cnt >= float(topk)
            lo = jnp.where(jnp.logical_and(active, enough), cand, lo)
            hi = jnp.where(jnp.logical_and(active, jnp.logical_not(enough)), cand, hi)
            lo_ref[...] = lo
            hi_ref[...] = hi
            extract = jnp.logical_and(active, cnt == float(topk - 1))
            ext_ref[...] = jnp.where(extract, 1.0, ext_ref[...])
            finished = jnp.logical_or(jnp.logical_or(cnt == float(topk), extract), hi - lo == 1)
            done_ref[...] = jnp.where(jnp.logical_and(active, finished), 1.0, done_ref[...])
            return n + 1, stop

        lax.while_loop(search_cond, search_step, (jnp.int32(0), jnp.int32(0)))

        def key_to_f32(keys):
            return pltpu.bitcast(keys ^ ((keys >> 31) & 0x7FFFFFFF), F32)

        row_pass(hi_ref, lambda acc, keys, hi: jnp.maximum(acc, jnp.where(keys < hi, keys, KEY_NEG_INF)),
                 jnp.full((rows, LANES), KEY_NEG_INF, jnp.int32), key_to_f32,
                 lambda rs: jnp.max(ext_ref[rs, :]) > 0.0)
        below = jnp.broadcast_to(_f32_key(jnp.max(cnt_ref[...], axis=1, keepdims=True)), (tq, LANES))
        thr = jnp.where(ext_ref[...] > 0.0, below, lo_ref[...])
        tau_ref[...] = jnp.maximum(thr, INT_MIN + 1)

        m_ref[...] = jnp.full(m_ref.shape, M_INIT, F32)
        l_ref[...] = jnp.zeros(l_ref.shape, F32)
        acc_ref[...] = jnp.zeros(acc_ref.shape, F32)

    tau = tau_ref[...]
    for t in range(tk // LANES):
        ls = slice(t * LANES, (t + 1) * LANES)
        bias_ref[:, ls] = jnp.where(sc_ref[kj, :, ls] >= tau, 0.0, NEG_BIG)

    def add_bias(x, rs, ls):
        return x + bias_ref[rs, ls]

    heads = [slice(h * HEAD_DIM, (h + 1) * HEAD_DIM) for h in range(A_HEADS)]
    if bounded:
        bias = bias_ref[...]
        for h, hs in enumerate(heads):
            p = jnp.exp2(_nt_dot(qa_ref[:, hs], ka_ref[:, hs]) + bias)
            l_ref[h] = l_ref[h] + jnp.sum(p, axis=1, keepdims=True)
            acc_ref[h] = acc_ref[h] + jnp.dot(p.astype(BF16), va_ref[:, hs], preferred_element_type=F32)
    else:
        for h, hs in enumerate(heads):
            s_ref[h] = _nt_dot(qa_ref[:, hs], ka_ref[:, hs])
        for h in range(A_HEADS):
            _softmax_rows(s_ref.at[h], p_ref.at[h], alpha_ref.at[h], m_ref.at[h], l_ref.at[h], add_bias)
        for h, hs in enumerate(heads):
            acc_ref[h] = alpha_ref[h] * acc_ref[h] + jnp.dot(p_ref[h], va_ref[:, hs],
                                                             preferred_element_type=F32)

    @pl.when(kj == last)
    def _finish():
        for h in range(A_HEADS):
            hs = slice(h * HEAD_DIM, (h + 1) * HEAD_DIM)
            o = acc_ref[h] / l_ref[h]
            o_ref[:, hs] = (o * ga_ref[:, hs].astype(F32)).astype(BF16)


def _causal_steps(nq, tq, tk):
    qt, kt = [], []
    for i in range(nq):
        for j in range(((i + 1) * tq - 1) // tk + 1):
            qt.append(i)
            kt.append(j)
    return jnp.asarray(qt, jnp.int32), jnp.asarray(kt, jnp.int32)


def _attn_a(qi, wi, kab, qkr, vg, batch, seq, tq, tk, topk, bounded):
    m = batch * seq
    nq = seq // tq
    nk = seq // tk
    qt, kt = _causal_steps(nq, tq, tk)

    def qrow(b, s, qt_ref, kt_ref):
        return b * nq + qt_ref[s]

    def krow(b, s, qt_ref, kt_ref):
        return b * nk + kt_ref[s]

    grid_spec = pltpu.PrefetchScalarGridSpec(
        num_scalar_prefetch=2,
        grid=(batch, int(qt.shape[0])),
        in_specs=[
            pl.BlockSpec((IDX_HEADS // 2, tq, LANES), lambda b, s, q, k: (0, qrow(b, s, q, k), 0)),
            pl.BlockSpec((tq, LANES), lambda b, s, q, k: (qrow(b, s, q, k), 0)),
            pl.BlockSpec((2, seq, LANES), lambda b, s, q, k: (0, b, 0),
                         pipeline_mode=pl.Buffered(1)),
            pl.BlockSpec((tq, A_WIDTH), lambda b, s, q, k: (qrow(b, s, q, k), 0)),
            pl.BlockSpec((tk, A_WIDTH), lambda b, s, q, k: (krow(b, s, q, k), 1)),
            pl.BlockSpec((tk, A_WIDTH), lambda b, s, q, k: (krow(b, s, q, k), 0)),
            pl.BlockSpec((tq, A_WIDTH), lambda b, s, q, k: (qrow(b, s, q, k), 2)),
        ],
        out_specs=pl.BlockSpec((tq, A_WIDTH), lambda b, s, q, k: (qrow(b, s, q, k), 0)),
        scratch_shapes=[
            pltpu.VMEM((nk, tq, tk), jnp.int32),
            pltpu.VMEM((2, tq, LANES), F32),
            pltpu.VMEM((tq, LANES), jnp.int32),
            pltpu.VMEM((tq, LANES), jnp.int32),
            pltpu.VMEM((tq, LANES), jnp.int32),
            pltpu.VMEM((tq, LANES), jnp.int32),
            pltpu.VMEM((tq, LANES), F32),
            pltpu.VMEM((tq, LANES), F32),
            pltpu.VMEM((tq, LANES), F32),
            pltpu.VMEM((tq, tk), F32),
            pltpu.VMEM((A_HEADS, tq, tk), F32),
            pltpu.VMEM((A_HEADS, tq, tk), BF16),
            pltpu.VMEM((A_HEADS, tq, LANES), F32),
            pltpu.VMEM((A_HEADS, tq, LANES), F32),
            pltpu.VMEM((A_HEADS, tq, LANES), F32),
            pltpu.VMEM((A_HEADS, tq, HEAD_DIM), F32),
        ],
    )
    return pl.pallas_call(
        functools.partial(_a_kernel, tq=tq, tk=tk, topk=topk, rows=SELECT_ROWS, bounded=bounded),
        grid_spec=grid_spec,
        out_shape=jax.ShapeDtypeStruct((m, A_WIDTH), BF16),
        compiler_params=_params(("parallel", "arbitrary")),
        name="attn_a",
    )(qt, kt, qi, wi, kab, qkr, qkr, vg, vg)


def _b_kernel(qt_ref, kt_ref, q_ref, k_ref, v_ref, sg_ref, gain_ref, lq1_ref, lk1_ref, lq2_ref, lk2_ref,
              o_ref, s_ref, p_ref, alpha_ref, m_ref, l_ref, acc_ref, *, tq, heads, lambda_init, bounded):
    step = pl.program_id(2)
    iq = qt_ref[step]
    kj = kt_ref[step]
    hw = 2 * HEAD_DIM
    chains = [(hh, c) for hh in range(heads) for c in range(2)]

    @pl.when(kj == 0)
    def _init():
        m_ref[...] = jnp.full(m_ref.shape, M_INIT, F32)
        l_ref[...] = jnp.zeros(l_ref.shape, F32)
        acc_ref[...] = jnp.zeros(acc_ref.shape, F32)

    def causal_mask(x, rs, ls):
        row = rs.start + lax.broadcasted_iota(jnp.int32, x.shape, 0)
        col = ls.start + lax.broadcasted_iota(jnp.int32, x.shape, 1)
        return jnp.where(col <= row, x, NEG_BIG)

    def attend(diagonal):
        adjust = causal_mask if diagonal else (lambda x, rs, ls: x)
        qk = lambda hh, c: (slice(hh * hw + c * HEAD_DIM, hh * hw + (c + 1) * HEAD_DIM),) * 2
        if bounded:
            if diagonal:
                keep = (lax.broadcasted_iota(jnp.int32, (tq, tq), 1)
                        <= lax.broadcasted_iota(jnp.int32, (tq, tq), 0))
            for i, (hh, c) in enumerate(chains):
                qs, ks = qk(hh, c)
                p = jnp.exp2(_nt_dot(q_ref[:, qs], k_ref[:, ks]))
                if diagonal:
                    p = jnp.where(keep, p, 0.0)
                l_ref[i] = l_ref[i] + jnp.sum(p, axis=1, keepdims=True)
                acc_ref[i] = acc_ref[i] + jnp.dot(p.astype(BF16), v_ref[:, hh * hw:(hh + 1) * hw],
                                                  preferred_element_type=F32)
        else:
            for i, (hh, c) in enumerate(chains):
                qs, ks = qk(hh, c)
                s_ref[i] = _nt_dot(q_ref[:, qs], k_ref[:, ks])
            for i in range(len(chains)):
                _softmax_rows(s_ref.at[i], p_ref.at[i], alpha_ref.at[i], m_ref.at[i], l_ref.at[i], adjust)
            for i, (hh, c) in enumerate(chains):
                pv = jnp.dot(p_ref[i], v_ref[:, hh * hw:(hh + 1) * hw], preferred_element_type=F32)
                alpha = alpha_ref[i]
                for t in range(hw // LANES):
                    ls = slice(t * LANES, (t + 1) * LANES)
                    acc_ref[i, :, ls] = alpha * acc_ref[i, :, ls] + pv[:, ls]

    @pl.when(kj < iq)
    def _below_diagonal():
        attend(False)

    @pl.when(kj == iq)
    def _diagonal_and_finish():
        attend(True)
        lam = (jnp.exp(jnp.sum(lq1_ref[...] * lk1_ref[...], axis=1, keepdims=True))
               - jnp.exp(jnp.sum(lq2_ref[...] * lk2_ref[...], axis=1, keepdims=True))
               + lambda_init)
        for hh in range(heads):
            o0 = acc_ref[2 * hh] / _tile_lanes(l_ref[2 * hh], hw)
            o1 = acc_ref[2 * hh + 1] / _tile_lanes(l_ref[2 * hh + 1], hw)
            o = o0 - lam * o1
            ms = jnp.mean(o * o, axis=-1, keepdims=True)
            y = o * lax.rsqrt(ms + SUBLN_EPS) * gain_ref[...] * (1.0 - lambda_init)
            hs = slice(hh * hw, (hh + 1) * hw)
            o_ref[:, hs] = (y * sg_ref[:, hs].astype(F32)).astype(BF16)


def _attn_b(qkr, vg, gain, lq1, lk1, lq2, lk2, batch, seq, tq, heads, lambda_init, bounded):
    m = batch * seq
    nq = seq // tq
    bw = heads * 2 * HEAD_DIM
    groups = B_HEADS // heads
    chains = 2 * heads
    qt, kt = _causal_steps(nq, tq, tq)

    def qrow(b, s, qt_ref):
        return b * nq + qt_ref[s]

    vec = pl.BlockSpec((1, HEAD_DIM), lambda b, h, s, q, k: (0, 0))
    grid_spec = pltpu.PrefetchScalarGridSpec(
        num_scalar_prefetch=2,
        grid=(batch, groups, int(qt.shape[0])),
        in_specs=[
            pl.BlockSpec((tq, bw), lambda b, h, s, q, k: (qrow(b, s, q), 2 * groups + h)),
            pl.BlockSpec((tq, bw), lambda b, h, s, q, k: (qrow(b, s, k), 3 * groups + h)),
            pl.BlockSpec((tq, bw), lambda b, h, s, q, k: (qrow(b, s, k), groups + h)),
            pl.BlockSpec((tq, bw), lambda b, h, s, q, k: (qrow(b, s, q), 3 * groups + h)),
            pl.BlockSpec((1, 2 * HEAD_DIM), lambda b, h, s, q, k: (0, 0)),
            vec, vec, vec, vec,
        ],
        out_specs=pl.BlockSpec((tq, bw), lambda b, h, s, q, k: (qrow(b, s, q), h)),
        scratch_shapes=[
            pltpu.VMEM((chains, tq, tq), F32),
            pltpu.VMEM((chains, tq, tq), BF16),
            pltpu.VMEM((chains, tq, LANES), F32),
            pltpu.VMEM((chains, tq, LANES), F32),
            pltpu.VMEM((chains, tq, LANES), F32),
            pltpu.VMEM((chains, tq, 2 * HEAD_DIM), F32),
        ],
    )
    return pl.pallas_call(
        functools.partial(_b_kernel, tq=tq, heads=heads, lambda_init=lambda_init, bounded=bounded),
        grid_spec=grid_spec,
        out_shape=jax.ShapeDtypeStruct((m, B_WIDTH), BF16),
        compiler_params=_params(("parallel", "parallel", "arbitrary")),
        name="attn_b",
    )(qt, kt, qkr, qkr, vg, vg, gain, lq1, lk1, lq2, lk2)


def _merge_kernel(oa_ref, ob_ref, wa_ref, wb_ref, ma_ref, mb_ref, o_ref):
    ya = jnp.dot(oa_ref[...], wa_ref[...], preferred_element_type=F32)
    yb = jnp.dot(ob_ref[...], wb_ref[...], preferred_element_type=F32)
    o_ref[...] = (ma_ref[...].astype(F32) * ya + mb_ref[...].astype(F32) * yb).astype(BF16)


def _merge(oa, ob, w_a, w_b, vg, d_model, tm, tn):
    m = oa.shape[0]
    gate_a0 = (A_WIDTH + B_WIDTH) * 2 // tn
    gate_b0 = gate_a0 + d_model // tn
    return pl.pallas_call(
        _merge_kernel,
        grid=(m // tm, d_model // tn),
        in_specs=[
            pl.BlockSpec((tm, A_WIDTH), lambda i, j: (i, 0)),
            pl.BlockSpec((tm, B_WIDTH), lambda i, j: (i, 0)),
            pl.BlockSpec((A_WIDTH, tn), lambda i, j: (0, j)),
            pl.BlockSpec((B_WIDTH, tn), lambda i, j: (0, j)),
            pl.BlockSpec((tm, tn), lambda i, j: (i, gate_a0 + j)),
            pl.BlockSpec((tm, tn), lambda i, j: (i, gate_b0 + j)),
        ],
        out_specs=pl.BlockSpec((tm, tn), lambda i, j: (i, j)),
        out_shape=jax.ShapeDtypeStruct((m, d_model), BF16),
        compiler_params=_params(("parallel", "arbitrary")),
        name="merge",
    )(oa, ob, w_a, w_b, vg, vg)


def _out_kernel(mg_ref, w_ref, x_ref, o_ref):
    o_ref[...] = x_ref[...] + jnp.dot(mg_ref[...], w_ref[...], preferred_element_type=F32)


def _out_proj(merged, w_out, x2d, tm, tn):
    m, d = x2d.shape
    return pl.pallas_call(
        _out_kernel,
        grid=(m // tm, d // tn),
        in_specs=[
            pl.BlockSpec((tm, d), lambda i, j: (i, 0)),
            pl.BlockSpec((d, tn), lambda i, j: (0, j)),
            pl.BlockSpec((tm, tn), lambda i, j: (i, j)),
        ],
        out_specs=pl.BlockSpec((tm, tn), lambda i, j: (i, j)),
        out_shape=jax.ShapeDtypeStruct((m, d), F32),
        compiler_params=_params(("parallel", "arbitrary")),
        name="out_proj",
    )(merged, w_out, x2d)


def _qi_column_order():
    lane = np.arange(LANES)
    quarter, r = lane // 32, lane % 32
    dim = r + 32 * (quarter // 2)
    q_cols = np.concatenate([(2 * p + quarter % 2) * IDX_DIM + dim for p in range(IDX_HEADS // 2)])
    return q_cols, dim, (quarter % 2 == 0)


def _layer(x2d, tabs, norm_gain, w_in, a_q_gain, a_k_gain, b_q_gain, b_k_gain,
           lq1, lk1, lq2, lk2, b_subln_gain, w_o_a, w_o_b, w_out, layer, batch, seq):
    d_model = x2d.shape[1]
    cos_h, sin_h, cos_i, sin_i = tabs
    sizes = (A_WIDTH,) * 4 + (IDX_HEADS * IDX_DIM, IDX_DIM, IDX_HEADS) + (B_WIDTH,) * 4 + (d_model,) * 2
    off = np.concatenate([[0], np.cumsum(sizes)])
    w_bf = w_in.astype(BF16)
    seg = lambda i: w_bf[:, off[i]:off[i + 1]]
    (w_qa, w_ka, w_va, w_ga, w_qi, w_ki, w_wi, w_qb, w_kb, w_vb, w_gb, w_ma, w_mb) = [seg(i) for i in range(13)]

    scale = HEAD_DIM ** -0.5 * LOG2E
    w_qk = jnp.concatenate([w_qa, w_ka, w_qb, w_kb], axis=1)
    gain_qk = jnp.concatenate([jnp.tile(a_q_gain * scale, A_HEADS), jnp.tile(a_k_gain, A_HEADS),
                               jnp.tile(b_q_gain * scale, 2 * B_HEADS), jnp.tile(b_k_gain, 2 * B_HEADS)])[None, :]
    q_cols, k_dim, even = _qi_column_order()
    w_qi_p = w_qi[:, q_cols]
    w_k_rep = w_ki[:, k_dim]
    zeros = jnp.zeros_like(w_k_rep)
    w_kw = jnp.concatenate([
        jnp.where(even[None, :], w_k_rep, zeros), jnp.where(even[None, :], zeros, w_k_rep),
        jnp.pad(w_wi, ((0, 0), (0, 2 * LANES - IDX_HEADS)))], axis=1)
    w_vg = jnp.concatenate([w_va, w_vb, w_ga, w_gb, w_ma, w_mb], axis=1)

    m = x2d.shape[0]
    tm, tn = PROJ_TILE
    h = _norm(x2d, norm_gain[None, :], NORM_ROWS)

    tab_spec = pl.BlockSpec((tm, LANES), lambda i, j: (i, 0))
    qkr = _proj_qk(h, w_qk, gain_qk, cos_h, sin_h, tm, tn)
    tn_qi = min(tn, w_qi_p.shape[1])
    qi = _proj(h, w_qi_p, (cos_i, sin_i), [tab_spec, tab_spec], _ep_qi,
               jax.ShapeDtypeStruct((IDX_HEADS // 2, m, LANES), BF16),
               pl.BlockSpec((tn_qi // LANES, tm, LANES), lambda i, j: (j, i, 0)), tm, tn_qi, "proj_qi")
    kab, wi = _proj(h, w_kw, (cos_i, sin_i), [tab_spec, tab_spec], _ep_kw,
                    (jax.ShapeDtypeStruct((2, m, LANES), BF16), jax.ShapeDtypeStruct((m, LANES), F32)),
                    (pl.BlockSpec((2, tm, LANES), lambda i, j: (0, i, 0)),
                     pl.BlockSpec((tm, LANES), lambda i, j: (i, 0))), tm, w_kw.shape[1], "proj_kw")
    n_plain = (A_WIDTH + B_WIDTH) // tn
    vg = _proj(h, w_vg, (), [], functools.partial(_ep_vg, n_plain=n_plain, n_silu=n_plain),
               jax.ShapeDtypeStruct((m, w_vg.shape[1]), BF16),
               pl.BlockSpec((tm, tn), lambda i, j: (i, j)), tm, tn, "proj_vg")

    def logit_bound(q_gain, k_gain):
        return HEAD_DIM * 1.02 * jnp.max(jnp.abs(q_gain * scale)) * jnp.max(jnp.abs(k_gain))

    def with_bound(bound, attend):
        return lax.cond(bound <= MAX_LOGIT_BOUND, lambda: attend(True), lambda: attend(False))

    topk = min(TOPK_MAX, seq // 4)
    oa = with_bound(logit_bound(a_q_gain, a_k_gain),
                    lambda f: _attn_a(qi, wi, kab, qkr, vg, batch, seq, *ATTN_A_TILE, topk, f))
    lambda_init = 0.8 - 0.6 * math.exp(-0.3 * layer)
    ob = with_bound(logit_bound(b_q_gain, b_k_gain),
                    lambda f: _attn_b(qkr, vg, b_subln_gain[None, :], lq1[None, :], lk1[None, :], lq2[None, :],
                                      lk2[None, :], batch, seq, ATTN_B_TILE, B_HEADS, lambda_init, f))
    merged = _merge(oa, ob, w_o_a.astype(BF16), w_o_b.astype(BF16), vg, d_model, *OUT_TILE)
    return _out_proj(merged, w_out.astype(BF16), x2d, *OUT_TILE)


def kernel(x, positions, norm_gain, w_in, a_q_gain, a_k_gain, b_q_gain, b_k_gain, lambda_q1, lambda_k1,
           lambda_q2, lambda_k2, b_subln_gain, w_o_a, w_o_b, w_out):
    batch, seq, d_model = x.shape
    m = batch * seq
    pos = positions.astype(F32).reshape(m, 1)
    ang_h = pos * ROPE_THETA ** (-jnp.arange(0, HEAD_DIM, 2, dtype=F32) / HEAD_DIM)
    ang_i = pos * ROPE_THETA ** (-jnp.arange(0, IDX_DIM, 2, dtype=F32) / IDX_DIM)
    ch, sh, ci, si = jnp.cos(ang_h), jnp.sin(ang_h), jnp.cos(ang_i), jnp.sin(ang_i)
    tabs = (jnp.concatenate([ch, ch], axis=1), jnp.concatenate([-sh, sh], axis=1),
            jnp.concatenate([ci] * 4, axis=1), jnp.concatenate([-si, -si, si, si], axis=1))
    x2d = x.reshape(m, d_model)
    for layer in range(norm_gain.shape[0]):
        x2d = _layer(x2d, tabs, norm_gain[layer], w_in[layer], a_q_gain[layer], a_k_gain[layer],
                     b_q_gain[layer], b_k_gain[layer], lambda_q1[layer], lambda_k1[layer],
                     lambda_q2[layer], lambda_k2[layer], b_subln_gain[layer], w_o_a[layer], w_o_b[layer],
                     w_out[layer], layer, batch, seq)
    return x2d.reshape(batch, seq, d_model)
```

```python
import functools
import math

import jax
import jax.numpy as jnp
import numpy as np
from jax import lax
from jax.experimental import pallas as pl
from jax.experimental.pallas import tpu as pltpu

F32 = jnp.float32
BF16 = jnp.bfloat16

HEAD_DIM = 128
A_HEADS = 8
A_WIDTH = A_HEADS * HEAD_DIM
IDX_HEADS = 16
IDX_DIM = 64
TOPK_MAX = 256
B_HEADS = 4
B_WIDTH = B_HEADS * 2 * HEAD_DIM
ROPE_THETA = 10000.0
NORM_EPS = 1e-6
SUBLN_EPS = 1e-5

LANES = 128
INT_MIN = -(2 ** 31)
KEY_NEG_INF = INT_MIN + 0x7FFFFF
NEG_BIG = -1e30
M_INIT = -5e29
LOG2E = math.log2(math.e)
MAX_LOGIT_BOUND = 60.0
SELECT_ROWS = 64
SOFTMAX_ROWS = 32
PROJ_SUB = 256
VMEM_LIMIT_BYTES = 56 * 1024 * 1024

NORM_ROWS = 512
PROJ_TILE = (1024, 2048)
ATTN_A_TILE = (256, 1024)
ATTN_B_TILE = 512
MERGE_OUT_TILE = (512, 1024)
MERGE_SUB = 512
OUT_TILE = (1024, 1024)


def _nt_dot(a, b):
    return lax.dot_general(a, b, (((1,), (1,)), ((), ())), preferred_element_type=F32)


def _tile_lanes(v, width):
    reps = width // LANES
    return v if reps == 1 else jnp.concatenate([v] * reps, axis=1)


def _params(semantics):
    return pltpu.CompilerParams(dimension_semantics=semantics, vmem_limit_bytes=VMEM_LIMIT_BYTES)


def _norm_kernel(x_ref, g_ref, h_ref):
    x = x_ref[...]
    ms = jnp.mean(x * x, axis=-1, keepdims=True)
    h_ref[...] = (x * lax.rsqrt(ms + NORM_EPS) * g_ref[...]).astype(BF16)


def _norm(x2d, gain, tm):
    m, d = x2d.shape
    return pl.pallas_call(
        _norm_kernel,
        grid=(m // tm,),
        in_specs=[pl.BlockSpec((tm, d), lambda i: (i, 0)), pl.BlockSpec((1, d), lambda i: (0, 0))],
        out_specs=pl.BlockSpec((tm, d), lambda i: (i, 0)),
        out_shape=jax.ShapeDtypeStruct((m, d), BF16),
        compiler_params=_params(("parallel",)),
        name="norm",
    )(x2d, gain)


def _proj_body(h_ref, w_ref, *rest, n_aux, epilogue):
    tn = w_ref.shape[1]

    def sub_dot(t):
        return jnp.dot(h_ref[...], w_ref[:, t * PROJ_SUB:(t + 1) * PROJ_SUB], preferred_element_type=F32)

    epilogue(sub_dot, tn // PROJ_SUB, rest[:n_aux], rest[n_aux:])


def _proj(h, w, aux, aux_specs, epilogue, out_shape, out_specs, tm, tn, name):
    m, k = h.shape
    n = w.shape[1]
    return pl.pallas_call(
        functools.partial(_proj_body, n_aux=len(aux), epilogue=epilogue),
        grid=(m // tm, n // tn),
        in_specs=[pl.BlockSpec((tm, k), lambda i, j: (i, 0)),
                  pl.BlockSpec((k, tn), lambda i, j: (0, j))] + aux_specs,
        out_specs=out_specs,
        out_shape=out_shape,
        compiler_params=_params(("parallel", "arbitrary")),
        name=name,
    )(h, w, *aux)


def _rope128(y, cos_full, sin_signed):
    return y * cos_full + pltpu.roll(y, 64, 1) * sin_signed


def _proj_qk_kernel(h_ref, w_ref, g_ref, c_ref, s_ref, o_ref, acc_ref):
    acc_ref[...] = jnp.dot(h_ref[...], w_ref[...], preferred_element_type=F32)
    same_head = (lax.broadcasted_iota(jnp.int32, (PROJ_SUB, PROJ_SUB), 0) // HEAD_DIM
                 == lax.broadcasted_iota(jnp.int32, (PROJ_SUB, PROJ_SUB), 1) // HEAD_DIM)
    head_mean = jnp.where(same_head, 1.0 / HEAD_DIM, 0.0).astype(BF16)

    def sub_tile(t, carry):
        cs = pl.ds(pl.multiple_of(t * PROJ_SUB, PROJ_SUB), PROJ_SUB)
        acc = acc_ref[:, cs]
        ms = jnp.dot((acc * acc).astype(BF16), head_mean, preferred_element_type=F32)
        y = acc * lax.rsqrt(ms + NORM_EPS) * g_ref[:, cs]
        c = c_ref[...]
        s = s_ref[...]
        for u in range(PROJ_SUB // LANES):
            ls = pl.ds(pl.multiple_of(t * PROJ_SUB + u * LANES, LANES), LANES)
            o_ref[:, ls] = _rope128(y[:, u * LANES:(u + 1) * LANES], c, s).astype(BF16)
        return carry

    lax.fori_loop(0, acc_ref.shape[1] // PROJ_SUB, sub_tile, 0)


def _proj_qk(h, w, gain, cos_full, sin_signed, tm, tn):
    m, k = h.shape
    n = w.shape[1]
    tab_spec = pl.BlockSpec((tm, LANES), lambda i, j: (i, 0))
    return pl.pallas_call(
        _proj_qk_kernel,
        grid=(m // tm, n // tn),
        in_specs=[pl.BlockSpec((tm, k), lambda i, j: (i, 0)),
                  pl.BlockSpec((k, tn), lambda i, j: (0, j)),
                  pl.BlockSpec((1, tn), lambda i, j: (0, j)), tab_spec, tab_spec],
        out_specs=pl.BlockSpec((tm, tn), lambda i, j: (i, j)),
        out_shape=jax.ShapeDtypeStruct((m, n), BF16),
        scratch_shapes=[pltpu.VMEM((tm, tn), F32)],
        compiler_params=_params(("parallel", "arbitrary")),
        name="proj_qk",
    )(h, w, gain, cos_full, sin_signed)


def _ep_qi(sub_dot, n_sub, aux, outs):
    c_ref, s_ref = aux
    (o_ref,) = outs
    c = c_ref[...]
    s = s_ref[...]
    per = PROJ_SUB // LANES
    for t in range(n_sub):
        acc = sub_dot(t)
        for u in range(per):
            o_ref[t * per + u] = _rope128(acc[:, u * LANES:(u + 1) * LANES], c, s).astype(BF16)


def _ep_kw(sub_dot, n_sub, aux, outs):
    c_ref, s_ref = aux
    kab_ref, w_ref = outs
    c = c_ref[...]
    s = s_ref[...]
    acc = sub_dot(0)
    for u in range(2):
        kab_ref[u] = _rope128(acc[:, u * LANES:(u + 1) * LANES], c, s).astype(BF16)
    w_ref[...] = sub_dot(1)[:, :LANES] * (IDX_HEADS ** -0.5 * IDX_DIM ** -0.5)


def _ep_vg(sub_dot, n_sub, aux, outs, *, n_plain, n_silu):
    (o_ref,) = outs
    j = pl.program_id(1)

    def store(fn):
        for t in range(n_sub):
            o_ref[:, t * PROJ_SUB:(t + 1) * PROJ_SUB] = fn(sub_dot(t)).astype(BF16)

    @pl.when(j < n_plain)
    def _():
        store(lambda a: a)

    @pl.when(jnp.logical_and(j >= n_plain, j < n_plain + n_silu))
    def _():
        store(lambda a: a * jax.nn.sigmoid(a))

    @pl.when(j >= n_plain + n_silu)
    def _():
        store(jax.nn.sigmoid)


def _f32_key(v):
    bits = pltpu.bitcast(v, jnp.int32)
    return bits ^ ((bits >> 31) & 0x7FFFFFFF)


def _softmax_rows(s_ref, p_ref, alpha_ref, m_ref, l_ref, adjust):
    tq, tk = s_ref.shape
    for r in range(tq // SOFTMAX_ROWS):
        rs = slice(r * SOFTMAX_ROWS, (r + 1) * SOFTMAX_ROWS)
        lanes = [slice(t * LANES, (t + 1) * LANES) for t in range(tk // LANES)]
        s = [adjust(s_ref[rs, ls], rs, ls) for ls in lanes]
        m_prev = m_ref[rs, :]
        m_new = jnp.maximum(m_prev, jnp.max(functools.reduce(jnp.maximum, s), axis=1, keepdims=True))
        alpha = jnp.exp2(m_prev - m_new)
        p = [jnp.exp2(x - m_new) for x in s]
        l_ref[rs, :] = alpha * l_ref[rs, :] + jnp.sum(functools.reduce(jnp.add, p), axis=1, keepdims=True)
        for ls, x in zip(lanes, p):
            p_ref[rs, ls] = x.astype(BF16)
        alpha_ref[rs, :] = alpha
        m_ref[rs, :] = m_new


def _a_kernel(qt_ref, kt_ref, qi_ref, wi_ref, kab_ref, qa_ref, ka_ref, va_ref, ga_ref, o_ref,
              sc_ref, gmax_ref, tau_ref, lo_ref, hi_ref, cand_ref, done_ref, ext_ref, cnt_ref, bias_ref,
              s_ref, p_ref, alpha_ref, m_ref, l_ref, acc_ref, *, tq, tk, topk, rows, bounded):
    step = pl.program_id(1)
    iq = qt_ref[step]
    kj = kt_ref[step]
    last = ((iq + 1) * tq - 1) // tk

    @pl.when(kj == 0)
    def _index_and_select():
        w = wi_ref[...]
        row = iq * tq + lax.broadcasted_iota(jnp.int32, (tq, tk), 0)
        col0 = lax.broadcasted_iota(jnp.int32, (tq, tk), 1)
        gmax_ref[...] = jnp.full(gmax_ref.shape, -jnp.inf, F32)

        def score_tile(j, carry):
            start = pl.multiple_of(j * tk, tk)
            k_even = kab_ref[0, pl.ds(start, tk), :]
            k_odd = kab_ref[1, pl.ds(start, tk), :]
            acc = jnp.zeros((tq, tk), F32)
            for p in range(IDX_HEADS // 2):
                q_p = qi_ref[p]
                s_even = jnp.maximum(_nt_dot(q_p, k_even), 0.0)
                s_odd = jnp.maximum(_nt_dot(q_p, k_odd), 0.0)
                acc = acc + w[:, 2 * p:2 * p + 1] * s_even + w[:, 2 * p + 1:2 * p + 2] * s_odd
            causal = col0 + j * tk <= row
            sc_ref[j] = jnp.where(causal, _f32_key(acc), INT_MIN)
            accm = jnp.where(causal, acc, -jnp.inf)
            half = tk // 2
            for g in range(2):
                gm = gmax_ref[g]
                for t in range(half // LANES):
                    lo = g * half + t * LANES
                    gm = jnp.maximum(gm, accm[:, lo:lo + LANES])
                gmax_ref[g] = gm
            return carry

        lax.fori_loop(0, last + 1, score_tile, 0)

        g0 = gmax_ref[0]
        g1 = gmax_ref[1]
        row_max = jnp.max(jnp.maximum(g0, g1), axis=1, keepdims=True)
        row_low = jnp.min(jnp.minimum(g0, g1), axis=1, keepdims=True)
        lo0 = jnp.broadcast_to(_f32_key(row_low), (tq, LANES))
        hi0 = jnp.broadcast_to(_f32_key(row_max), (tq, LANES)) + 1
        lo_ref[...] = lo0
        hi_ref[...] = hi0
        done_ref[...] = jnp.where(hi0 - lo0 == 1, 1.0, 0.0)
        ext_ref[...] = jnp.zeros((tq, LANES), F32)
        cnt_ref[...] = jnp.zeros((tq, LANES), F32)

        def row_pass(ref, body, init, finish, needed):
            chunks = [pl.ds(rc * rows, rows) for rc in range(tq // rows)]
            flags = [needed(rs) for rs in chunks]
            for rs, flag in zip(chunks, flags):

                @pl.when(flag)
                def _(rs=rs):
                    bound = ref[rs, :]

                    def tile_step(j, carry):
                        keys = sc_ref[j, rs, :]
                        for t in range(tk // LANES):
                            carry = body(carry, keys[:, t * LANES:(t + 1) * LANES], bound)
                        return carry

                    cnt_ref[rs, :] = finish(lax.fori_loop(0, last + 1, tile_step, init))

        def search_cond(c):
            n, stop = c
            return jnp.logical_and(n < 2 * 32 + 2, stop == 0)

        def search_step(c):
            n, _ = c
            stop = (jnp.min(done_ref[...]) > 0.0).astype(jnp.int32)
            lo = lo_ref[...]
            hi = hi_ref[...]
            cand_ref[...] = (lo >> 1) + (hi >> 1) + (lo & hi & 1)
            row_pass(cand_ref, lambda acc, keys, cand: acc + jnp.where(keys >= cand, 1.0, 0.0),
                     jnp.zeros((rows, LANES), F32), lambda acc: acc,
                     lambda rs: jnp.min(done_ref[rs, :]) == 0.0)
            cnt = jnp.sum(cnt_ref[...], axis=1, keepdims=True)
            cand = cand_ref[...]
            active = done_ref[...] == 0.0
            enough = cnt >= float(topk)
            lo = jnp.where(jnp.logical_and(active, enough), cand, lo)
            hi = jnp.where(jnp.logical_and(active, jnp.logical_not(enough)), cand, hi)
            lo_ref[...] = lo
            hi_ref[...] = hi
            extract = jnp.logical_and(active, cnt == float(topk - 1))
            ext_ref[...] = jnp.where(extract, 1.0, ext_ref[...])
            finished = jnp.logical_or(jnp.logical_or(cnt == float(topk), extract), hi - lo == 1)
            done_ref[...] = jnp.where(jnp.logical_and(active, finished), 1.0, done_ref[...])
            return n + 1, stop

        lax.while_loop(search_cond, search_step, (jnp.int32(0), jnp.int32(0)))

        def key_to_f32(keys):
            return pltpu.bitcast(keys ^ ((keys >> 31) & 0x7FFFFFFF), F32)

        row_pass(hi_ref, lambda acc, keys, hi: jnp.maximum(acc, jnp.where(keys < hi, keys, KEY_NEG_INF)),
                 jnp.full((rows, LANES), KEY_NEG_INF, jnp.int32), key_to_f32,
                 lambda rs: jnp.max(ext_ref[rs, :]) > 0.0)
        below = jnp.broadcast_to(_f32_key(jnp.max(cnt_ref[...], axis=1, keepdims=True)), (tq, LANES))
        thr = jnp.where(ext_ref[...] > 0.0, below, lo_ref[...])
        tau_ref[...] = jnp.maximum(thr, INT_MIN + 1)

        m_ref[...] = jnp.full(m_ref.shape, M_INIT, F32)
        l_ref[...] = jnp.zeros(l_ref.shape, F32)
        acc_ref[...] = jnp.zeros(acc_ref.shape, F32)

    tau = tau_ref[...]
    for t in range(tk // LANES):
        ls = slice(t * LANES, (t + 1) * LANES)
        bias_ref[:, ls] = jnp.where(sc_ref[kj, :, ls] >= tau, 0.0, NEG_BIG)

    def add_bias(x, rs, ls):
        return x + bias_ref[rs, ls]

    heads = [slice(h * HEAD_DIM, (h + 1) * HEAD_DIM) for h in range(A_HEADS)]
    if bounded:
        bias = bias_ref[...]
        for h, hs in enumerate(heads):
            p = jnp.exp2(_nt_dot(qa_ref[:, hs], ka_ref[:, hs]) + bias)
            l_ref[h] = l_ref[h] + jnp.sum(p, axis=1, keepdims=True)
            acc_ref[h] = acc_ref[h] + jnp.dot(p.astype(BF16), va_ref[:, hs], preferred_element_type=F32)
    else:
        for h, hs in enumerate(heads):
            s_ref[h] = _nt_dot(qa_ref[:, hs], ka_ref[:, hs])
        for h in range(A_HEADS):
            _softmax_rows(s_ref.at[h], p_ref.at[h], alpha_ref.at[h], m_ref.at[h], l_ref.at[h], add_bias)
        for h, hs in enumerate(heads):
            acc_ref[h] = alpha_ref[h] * acc_ref[h] + jnp.dot(p_ref[h], va_ref[:, hs],
                                                             preferred_element_type=F32)

    @pl.when(kj == last)
    def _finish():
        for h in range(A_HEADS):
            hs = slice(h * HEAD_DIM, (h + 1) * HEAD_DIM)
            o = acc_ref[h] / l_ref[h]
            o_ref[:, hs] = (o * ga_ref[:, hs].astype(F32)).astype(BF16)


def _causal_steps(nq, tq, tk):
    qt, kt = [], []
    for i in range(nq):
        for j in range(((i + 1) * tq - 1) // tk + 1):
            qt.append(i)
            kt.append(j)
    return jnp.asarray(qt, jnp.int32), jnp.asarray(kt, jnp.int32)


def _attn_a(qi, wi, kab, qkr, vg, batch, seq, tq, tk, topk, bounded):
    m = batch * seq
    nq = seq // tq
    nk = seq // tk
    qt, kt = _causal_steps(nq, tq, tk)

    def qrow(b, s, qt_ref, kt_ref):
        return b * nq + qt_ref[s]

    def krow(b, s, qt_ref, kt_ref):
        return b * nk + kt_ref[s]

    grid_spec = pltpu.PrefetchScalarGridSpec(
        num_scalar_prefetch=2,
        grid=(batch, int(qt.shape[0])),
        in_specs=[
            pl.BlockSpec((IDX_HEADS // 2, tq, LANES), lambda b, s, q, k: (0, qrow(b, s, q, k), 0)),
            pl.BlockSpec((tq, LANES), lambda b, s, q, k: (qrow(b, s, q, k), 0)),
            pl.BlockSpec((2, seq, LANES), lambda b, s, q, k: (0, b, 0),
                         pipeline_mode=pl.Buffered(1)),
            pl.BlockSpec((tq, A_WIDTH), lambda b, s, q, k: (qrow(b, s, q, k), 0)),
            pl.BlockSpec((tk, A_WIDTH), lambda b, s, q, k: (krow(b, s, q, k), 1)),
            pl.BlockSpec((tk, A_WIDTH), lambda b, s, q, k: (krow(b, s, q, k), 0)),
            pl.BlockSpec((tq, A_WIDTH), lambda b, s, q, k: (qrow(b, s, q, k), 2)),
        ],
        out_specs=pl.BlockSpec((tq, A_WIDTH), lambda b, s, q, k: (qrow(b, s, q, k), 0)),
        scratch_shapes=[
            pltpu.VMEM((nk, tq, tk), jnp.int32),
            pltpu.VMEM((2, tq, LANES), F32),
            pltpu.VMEM((tq, LANES), jnp.int32),
            pltpu.VMEM((tq, LANES), jnp.int32),
            pltpu.VMEM((tq, LANES), jnp.int32),
            pltpu.VMEM((tq, LANES), jnp.int32),
            pltpu.VMEM((tq, LANES), F32),
            pltpu.VMEM((tq, LANES), F32),
            pltpu.VMEM((tq, LANES), F32),
            pltpu.VMEM((tq, tk), F32),
            pltpu.VMEM((A_HEADS, tq, tk), F32),
            pltpu.VMEM((A_HEADS, tq, tk), BF16),
            pltpu.VMEM((A_HEADS, tq, LANES), F32),
            pltpu.VMEM((A_HEADS, tq, LANES), F32),
            pltpu.VMEM((A_HEADS, tq, LANES), F32),
            pltpu.VMEM((A_HEADS, tq, HEAD_DIM), F32),
        ],
    )
    return pl.pallas_call(
        functools.partial(_a_kernel, tq=tq, tk=tk, topk=topk, rows=SELECT_ROWS, bounded=bounded),
        grid_spec=grid_spec,
        out_shape=jax.ShapeDtypeStruct((m, A_WIDTH), BF16),
        compiler_params=_params(("parallel", "arbitrary")),
        name="attn_a",
    )(qt, kt, qi, wi, kab, qkr, qkr, vg, vg)


def _b_kernel(qt_ref, kt_ref, q_ref, k_ref, v_ref, sg_ref, gain_ref, lq1_ref, lk1_ref, lq2_ref, lk2_ref,
              o_ref, s_ref, p_ref, alpha_ref, m_ref, l_ref, acc_ref, *, tq, heads, lambda_init, bounded):
    step = pl.program_id(2)
    iq = qt_ref[step]
    kj = kt_ref[step]
    hw = 2 * HEAD_DIM
    chains = [(hh, c) for hh in range(heads) for c in range(2)]

    @pl.when(kj == 0)
    def _init():
        m_ref[...] = jnp.full(m_ref.shape, M_INIT, F32)
        l_ref[...] = jnp.zeros(l_ref.shape, F32)
        acc_ref[...] = jnp.zeros(acc_ref.shape, F32)

    def causal_mask(x, rs, ls):
        row = rs.start + lax.broadcasted_iota(jnp.int32, x.shape, 0)
        col = ls.start + lax.broadcasted_iota(jnp.int32, x.shape, 1)
        return jnp.where(col <= row, x, NEG_BIG)

    def attend(diagonal):
        adjust = causal_mask if diagonal else (lambda x, rs, ls: x)
        qk = lambda hh, c: (slice(hh * hw + c * HEAD_DIM, hh * hw + (c + 1) * HEAD_DIM),) * 2
        if bounded:
            if diagonal:
                keep = (lax.broadcasted_iota(jnp.int32, (tq, tq), 1)
                        <= lax.broadcasted_iota(jnp.int32, (tq, tq), 0))
            for i, (hh, c) in enumerate(chains):
                qs, ks = qk(hh, c)
                p = jnp.exp2(_nt_dot(q_ref[:, qs], k_ref[:, ks]))
                if diagonal:
                    p = jnp.where(keep, p, 0.0)
                l_ref[i] = l_ref[i] + jnp.sum(p, axis=1, keepdims=True)
                acc_ref[i] = acc_ref[i] + jnp.dot(p.astype(BF16), v_ref[:, hh * hw:(hh + 1) * hw],
                                                  preferred_element_type=F32)
        else:
            for i, (hh, c) in enumerate(chains):
                qs, ks = qk(hh, c)
                s_ref[i] = _nt_dot(q_ref[:, qs], k_ref[:, ks])
            for i in range(len(chains)):
                _softmax_rows(s_ref.at[i], p_ref.at[i], alpha_ref.at[i], m_ref.at[i], l_ref.at[i], adjust)
            for i, (hh, c) in enumerate(chains):
                pv = jnp.dot(p_ref[i], v_ref[:, hh * hw:(hh + 1) * hw], preferred_element_type=F32)
                alpha = alpha_ref[i]
                for t in range(hw // LANES):
                    ls = slice(t * LANES, (t + 1) * LANES)
                    acc_ref[i, :, ls] = alpha * acc_ref[i, :, ls] + pv[:, ls]

    @pl.when(kj < iq)
    def _below_diagonal():
        attend(False)

    @pl.when(kj == iq)
    def _diagonal_and_finish():
        attend(True)
        lam = (jnp.exp(jnp.sum(lq1_ref[...] * lk1_ref[...], axis=1, keepdims=True))
               - jnp.exp(jnp.sum(lq2_ref[...] * lk2_ref[...], axis=1, keepdims=True))
               + lambda_init)
        for hh in range(heads):
            o0 = acc_ref[2 * hh] / _tile_lanes(l_ref[2 * hh], hw)
            o1 = acc_ref[2 * hh + 1] / _tile_lanes(l_ref[2 * hh + 1], hw)
            o = o0 - lam * o1
            ms = jnp.mean(o * o, axis=-1, keepdims=True)
            y = o * lax.rsqrt(ms + SUBLN_EPS) * gain_ref[...] * (1.0 - lambda_init)
            hs = slice(hh * hw, (hh + 1) * hw)
            o_ref[:, hs] = (y * sg_ref[:, hs].astype(F32)).astype(BF16)


def _attn_b(qkr, vg, gain, lq1, lk1, lq2, lk2, batch, seq, tq, heads, lambda_init, bounded):
    m = batch * seq
    nq = seq // tq
    bw = heads * 2 * HEAD_DIM
    groups = B_HEADS // heads
    chains = 2 * heads
    qt, kt = _causal_steps(nq, tq, tq)

    def qrow(b, s, qt_ref):
        return b * nq + qt_ref[s]

    vec = pl.BlockSpec((1, HEAD_DIM), lambda b, h, s, q, k: (0, 0))
    grid_spec = pltpu.PrefetchScalarGridSpec(
        num_scalar_prefetch=2,
        grid=(batch, groups, int(qt.shape[0])),
        in_specs=[
            pl.BlockSpec((tq, bw), lambda b, h, s, q, k: (qrow(b, s, q), 2 * groups + h)),
            pl.BlockSpec((tq, bw), lambda b, h, s, q, k: (qrow(b, s, k), 3 * groups + h)),
            pl.BlockSpec((tq, bw), lambda b, h, s, q, k: (qrow(b, s, k), groups + h)),
            pl.BlockSpec((tq, bw), lambda b, h, s, q, k: (qrow(b, s, q), 3 * groups + h)),
            pl.BlockSpec((1, 2 * HEAD_DIM), lambda b, h, s, q, k: (0, 0)),
            vec, vec, vec, vec,
        ],
        out_specs=pl.BlockSpec((tq, bw), lambda b, h, s, q, k: (qrow(b, s, q), h)),
        scratch_shapes=[
            pltpu.VMEM((chains, tq, tq), F32),
            pltpu.VMEM((chains, tq, tq), BF16),
            pltpu.VMEM((chains, tq, LANES), F32),
            pltpu.VMEM((chains, tq, LANES), F32),
            pltpu.VMEM((chains, tq, LANES), F32),
            pltpu.VMEM((chains, tq, 2 * HEAD_DIM), F32),
        ],
    )
    return pl.pallas_call(
        functools.partial(_b_kernel, tq=tq, heads=heads, lambda_init=lambda_init, bounded=bounded),
        grid_spec=grid_spec,
        out_shape=jax.ShapeDtypeStruct((m, B_WIDTH), BF16),
        compiler_params=_params(("parallel", "parallel", "arbitrary")),
        name="attn_b",
    )(qt, kt, qkr, qkr, vg, vg, gain, lq1, lk1, lq2, lk2)


def _merge_kernel(oa_ref, ob_ref, wa_ref, wb_ref, ma_ref, mb_ref, o_ref):
    ya = jnp.dot(oa_ref[...], wa_ref[...], preferred_element_type=F32)
    yb = jnp.dot(ob_ref[...], wb_ref[...], preferred_element_type=F32)
    o_ref[...] = (ma_ref[...].astype(F32) * ya + mb_ref[...].astype(F32) * yb).astype(BF16)


def _merge(oa, ob, w_a, w_b, vg, d_model, tm, tn):
    m = oa.shape[0]
    gate_a0 = (A_WIDTH + B_WIDTH) * 2 // tn
    gate_b0 = gate_a0 + d_model // tn
    return pl.pallas_call(
        _merge_kernel,
        grid=(m // tm, d_model // tn),
        in_specs=[
            pl.BlockSpec((tm, A_WIDTH), lambda i, j: (i, 0)),
            pl.BlockSpec((tm, B_WIDTH), lambda i, j: (i, 0)),
            pl.BlockSpec((A_WIDTH, tn), lambda i, j: (0, j)),
            pl.BlockSpec((B_WIDTH, tn), lambda i, j: (0, j)),
            pl.BlockSpec((tm, tn), lambda i, j: (i, gate_a0 + j)),
            pl.BlockSpec((tm, tn), lambda i, j: (i, gate_b0 + j)),
        ],
        out_specs=pl.BlockSpec((tm, tn), lambda i, j: (i, j)),
        out_shape=jax.ShapeDtypeStruct((m, d_model), BF16),
        compiler_params=_params(("parallel", "arbitrary")),
        name="merge",
    )(oa, ob, w_a, w_b, vg, vg)


def _out_kernel(mg_ref, w_ref, x_ref, o_ref):
    o_ref[...] = x_ref[...] + jnp.dot(mg_ref[...], w_ref[...], preferred_element_type=F32)


def _out_proj(merged, w_out, x2d, tm, tn):
    m, d = x2d.shape
    return pl.pallas_call(
        _out_kernel,
        grid=(m // tm, d // tn),
        in_specs=[
            pl.BlockSpec((tm, d), lambda i, j: (i, 0)),
            pl.BlockSpec((d, tn), lambda i, j: (0, j)),
            pl.BlockSpec((tm, tn), lambda i, j: (i, j)),
        ],
        out_specs=pl.BlockSpec((tm, tn), lambda i, j: (i, j)),
        out_shape=jax.ShapeDtypeStruct((m, d), F32),
        compiler_params=_params(("parallel", "arbitrary")),
        name="out_proj",
    )(merged, w_out, x2d)


def _merge_out_kernel(oa_ref, ob_ref, wa_ref, wb_ref, ma_ref, mb_ref, wo_ref, x_ref, o_ref, mg_ref):
    @pl.when(pl.program_id(1) == 0)
    def _():
        for t in range(mg_ref.shape[1] // MERGE_SUB):
            cs = slice(t * MERGE_SUB, (t + 1) * MERGE_SUB)
            ya = jnp.dot(oa_ref[...], wa_ref[:, cs], preferred_element_type=F32)
            yb = jnp.dot(ob_ref[...], wb_ref[:, cs], preferred_element_type=F32)
            mg_ref[:, cs] = (ma_ref[:, cs].astype(F32) * ya + mb_ref[:, cs].astype(F32) * yb).astype(BF16)

    o_ref[...] = x_ref[...] + jnp.dot(mg_ref[...], wo_ref[...], preferred_element_type=F32)


def _merge_out(oa, ob, w_a, w_b, vg, w_out, x2d, tm, tn):
    m, d = x2d.shape
    gate_a = (A_WIDTH + B_WIDTH) * 2 // d
    once = pl.Buffered(1)
    return pl.pallas_call(
        _merge_out_kernel,
        grid=(m // tm, d // tn),
        in_specs=[
            pl.BlockSpec((tm, A_WIDTH), lambda i, j: (i, 0)),
            pl.BlockSpec((tm, B_WIDTH), lambda i, j: (i, 0)),
            pl.BlockSpec((A_WIDTH, d), lambda i, j: (0, 0), pipeline_mode=once),
            pl.BlockSpec((B_WIDTH, d), lambda i, j: (0, 0), pipeline_mode=once),
            pl.BlockSpec((tm, d), lambda i, j: (i, gate_a)),
            pl.BlockSpec((tm, d), lambda i, j: (i, gate_a + 1)),
            pl.BlockSpec((d, tn), lambda i, j: (0, j)),
            pl.BlockSpec((tm, tn), lambda i, j: (i, j)),
        ],
        out_specs=pl.BlockSpec((tm, tn), lambda i, j: (i, j)),
        out_shape=jax.ShapeDtypeStruct((m, d), F32),
        scratch_shapes=[pltpu.VMEM((tm, d), BF16)],
        compiler_params=_params(("parallel", "arbitrary")),
        name="merge_out",
    )(oa, ob, w_a, w_b, vg, vg, w_out, x2d)


def _qi_column_order():
    lane = np.arange(LANES)
    quarter, r = lane // 32, lane % 32
    dim = r + 32 * (quarter // 2)
    q_cols = np.concatenate([(2 * p + quarter % 2) * IDX_DIM + dim for p in range(IDX_HEADS // 2)])
    return q_cols, dim, (quarter % 2 == 0)


def _layer(x2d, tabs, norm_gain, w_in, a_q_gain, a_k_gain, b_q_gain, b_k_gain,
           lq1, lk1, lq2, lk2, b_subln_gain, w_o_a, w_o_b, w_out, layer, batch, seq):
    d_model = x2d.shape[1]
    cos_h, sin_h, cos_i, sin_i = tabs
    sizes = (A_WIDTH,) * 4 + (IDX_HEADS * IDX_DIM, IDX_DIM, IDX_HEADS) + (B_WIDTH,) * 4 + (d_model,) * 2
    off = np.concatenate([[0], np.cumsum(sizes)])
    w_bf = w_in.astype(BF16)
    seg = lambda i: w_bf[:, off[i]:off[i + 1]]
    (w_qa, w_ka, w_va, w_ga, w_qi, w_ki, w_wi, w_qb, w_kb, w_vb, w_gb, w_ma, w_mb) = [seg(i) for i in range(13)]

    scale = HEAD_DIM ** -0.5 * LOG2E
    w_qk = jnp.concatenate([w_qa, w_ka, w_qb, w_kb], axis=1)
    gain_qk = jnp.concatenate([jnp.tile(a_q_gain * scale, A_HEADS), jnp.tile(a_k_gain, A_HEADS),
                               jnp.tile(b_q_gain * scale, 2 * B_HEADS), jnp.tile(b_k_gain, 2 * B_HEADS)])[None, :]
    q_cols, k_dim, even = _qi_column_order()
    w_qi_p = w_qi[:, q_cols]
    w_k_rep = w_ki[:, k_dim]
    zeros = jnp.zeros_like(w_k_rep)
    w_kw = jnp.concatenate([
        jnp.where(even[None, :], w_k_rep, zeros), jnp.where(even[None, :], zeros, w_k_rep),
        jnp.pad(w_wi, ((0, 0), (0, 2 * LANES - IDX_HEADS)))], axis=1)
    w_vg = jnp.concatenate([w_va, w_vb, w_ga, w_gb, w_ma, w_mb], axis=1)

    m = x2d.shape[0]
    tm, tn = PROJ_TILE
    h = _norm(x2d, norm_gain[None, :], NORM_ROWS)

    tab_spec = pl.BlockSpec((tm, LANES), lambda i, j: (i, 0))
    qkr = _proj_qk(h, w_qk, gain_qk, cos_h, sin_h, tm, tn)
    tn_qi = min(tn, w_qi_p.shape[1])
    qi = _proj(h, w_qi_p, (cos_i, sin_i), [tab_spec, tab_spec], _ep_qi,
               jax.ShapeDtypeStruct((IDX_HEADS // 2, m, LANES), BF16),
               pl.BlockSpec((tn_qi // LANES, tm, LANES), lambda i, j: (j, i, 0)), tm, tn_qi, "proj_qi")
    kab, wi = _proj(h, w_kw, (cos_i, sin_i), [tab_spec, tab_spec], _ep_kw,
                    (jax.ShapeDtypeStruct((2, m, LANES), BF16), jax.ShapeDtypeStruct((m, LANES), F32)),
                    (pl.BlockSpec((2, tm, LANES), lambda i, j: (0, i, 0)),
                     pl.BlockSpec((tm, LANES), lambda i, j: (i, 0))), tm, w_kw.shape[1], "proj_kw")
    n_plain = (A_WIDTH + B_WIDTH) // tn
    vg = _proj(h, w_vg, (), [], functools.partial(_ep_vg, n_plain=n_plain, n_silu=n_plain),
               jax.ShapeDtypeStruct((m, w_vg.shape[1]), BF16),
               pl.BlockSpec((tm, tn), lambda i, j: (i, j)), tm, tn, "proj_vg")

    def logit_bound(q_gain, k_gain):
        return HEAD_DIM * 1.02 * jnp.max(jnp.abs(q_gain * scale)) * jnp.max(jnp.abs(k_gain))

    def with_bound(bound, attend):
        return lax.cond(bound <= MAX_LOGIT_BOUND, lambda: attend(True), lambda: attend(False))

    topk = min(TOPK_MAX, seq // 4)
    oa = with_bound(logit_bound(a_q_gain, a_k_gain),
                    lambda f: _attn_a(qi, wi, kab, qkr, vg, batch, seq, *ATTN_A_TILE, topk, f))
    lambda_init = 0.8 - 0.6 * math.exp(-0.3 * layer)
    ob = with_bound(logit_bound(b_q_gain, b_k_gain),
                    lambda f: _attn_b(qkr, vg, b_subln_gain[None, :], lq1[None, :], lk1[None, :], lq2[None, :],
                                      lk2[None, :], batch, seq, ATTN_B_TILE, B_HEADS, lambda_init, f))
    return _merge_out(oa, ob, w_o_a.astype(BF16), w_o_b.astype(BF16), vg, w_out.astype(BF16), x2d, *MERGE_OUT_TILE)


def kernel(x, positions, norm_gain, w_in, a_q_gain, a_k_gain, b_q_gain, b_k_gain, lambda_q1, lambda_k1,
           lambda_q2, lambda_k2, b_subln_gain, w_o_a, w_o_b, w_out):
    batch, seq, d_model = x.shape
    m = batch * seq
    pos = positions.astype(F32).reshape(m, 1)
    ang_h = pos * ROPE_THETA ** (-jnp.arange(0, HEAD_DIM, 2, dtype=F32) / HEAD_DIM)
    ang_i = pos * ROPE_THETA ** (-jnp.arange(0, IDX_DIM, 2, dtype=F32) / IDX_DIM)
    ch, sh, ci, si = jnp.cos(ang_h), jnp.sin(ang_h), jnp.cos(ang_i), jnp.sin(ang_i)
    tabs = (jnp.concatenate([ch, ch], axis=1), jnp.concatenate([-sh, sh], axis=1),
            jnp.concatenate([ci] * 4, axis=1), jnp.concatenate([-si, -si, si, si], axis=1))
    x2d = x.reshape(m, d_model)
    for layer in range(norm_gain.shape[0]):
        x2d = _layer(x2d, tabs, norm_gain[layer], w_in[layer], a_q_gain[layer], a_k_gain[layer],
                     b_q_gain[layer], b_k_gain[layer], lambda_q1[layer], lambda_k1[layer],
                     lambda_q2[layer], lambda_k2[layer], b_subln_gain[layer], w_o_a[layer], w_o_b[layer],
                     w_out[layer], layer, batch, seq)
    return x2d.reshape(batch, seq, d_model)
```
